```python
import math
import jax, jax.numpy as jnp
from jax import lax
import numpy as np

D_MODEL = 2048
BATCH = 8
SEQ = 2048
DEPTH = 1

GRID_W = 64
CTX_LEN = 256
MIX_WIDTH = D_MODEL
DA_WIDTH = MIX_WIDTH // 2
HG_WIDTH = MIX_WIDTH - DA_WIDTH
DA_HEAD_V = 128
DA_HEADS = DA_WIDTH // DA_HEAD_V
DA_HEAD_QK = DA_HEAD_V // 2
HG_EXPAND = 128
HG_HEADS = HG_WIDTH // HG_EXPAND
HG_HEAD_I = HG_WIDTH // HG_HEADS
IN_WIDTH = 3 * DA_WIDTH + 5 * HG_WIDTH
D_FF = 4 * D_MODEL
ROT_AXIS = DA_HEAD_QK // 2
ROPE_BASE = 10000.0
Q_BLOCK = 128
HG_CHUNK = 64
N_ADA = 6
EPS = 1e-5
ALPHA = (2.0 * DEPTH) ** 0.25
BETA = (8.0 * DEPTH) ** -0.25

kernel_name = 'hybrid_diffattn_hgrn2_dit_block'


def layer_norm(x, g, b):
    xf = x.astype(jnp.float32)
    mu = jnp.mean(xf, axis=-1, keepdims=True)
    var = jnp.mean(jnp.square(xf - mu), axis=-1, keepdims=True)
    return ((xf - mu) * lax.rsqrt(var + EPS) * g.astype(jnp.float32) + b.astype(jnp.float32)).astype(x.dtype)


def rms_norm(x, w):
    xf = x.astype(jnp.float32)
    return xf * lax.rsqrt(jnp.mean(xf * xf, axis=-1, keepdims=True) + EPS) * w.astype(jnp.float32)


def modulate(h, shift, scale):
    return h * (1 + scale) + shift


def adaln(cond, w, b):
    m = jax.nn.silu(cond) @ w + b
    return m.reshape(m.shape[0], N_ADA, D_MODEL)


def axial_rope_tables(rows):
    f32 = jnp.float32
    row = jnp.repeat(jnp.arange(rows, dtype=f32), GRID_W)
    col = jnp.tile(jnp.arange(GRID_W, dtype=f32), rows)
    inv_freq = ROPE_BASE ** (-jnp.arange(0, ROT_AXIS, 2, dtype=f32) / ROT_AXIS)
    ang_r = row[:, None] * inv_freq
    ang_c = col[:, None] * inv_freq
    return (jnp.cos(ang_r), jnp.sin(ang_r), jnp.cos(ang_c), jnp.sin(ang_c))


def rotate_pairs(x, cos, sin):
    x1, x2 = jnp.split(x, 2, axis=-1)
    return jnp.concatenate([x1 * cos - x2 * sin, x2 * cos + x1 * sin], axis=-1)


def apply_axial_rope(x, tables):
    cr, sr, cc, sc = [t[:, None, None, :] for t in tables]
    xr, xc = jnp.split(x.astype(jnp.float32), 2, axis=-1)
    return jnp.concatenate([rotate_pairs(xr, cr, sr), rotate_pairs(xc, cc, sc)], axis=-1).astype(x.dtype)


def split_projection(p):
    B, T, _ = p.shape
    sizes = [DA_WIDTH] * 3 + [HG_WIDTH] * 5
    offsets = [int(o) for o in np.cumsum(sizes)[:-1]]
    qd, kd, vd, qh, zf, zb, ih, gh = jnp.split(p, offsets, axis=-1)
    qd = qd.reshape(B, T, DA_HEADS, 2, DA_HEAD_QK)
    kd = kd.reshape(B, T, DA_HEADS, 2, DA_HEAD_QK)
    vd = vd.reshape(B, T, DA_HEADS, DA_HEAD_V)
    qh = jax.nn.silu(qh).reshape(B, T, HG_HEADS, HG_EXPAND)
    zf = zf.reshape(B, T, HG_HEADS, HG_EXPAND)
    zb = zb.reshape(B, T, HG_HEADS, HG_EXPAND)
    ih = ih.reshape(B, T, HG_HEADS, HG_HEAD_I)
    return qd, kd, vd, qh, zf, zb, ih, gh


def diff_attend(q, keys, vals, lam):
    B, Tq, H = q.shape[:3]
    nb = Tq // Q_BLOCK
    scale = DA_HEAD_QK ** -0.5
    qb = jnp.moveaxis(q.reshape(B, nb, Q_BLOCK, H, 2, DA_HEAD_QK), 1, 0)

    def block(qblk):
        s = jnp.einsum('bqhmd,bkhmd->bhmqk', qblk, keys, preferred_element_type=jnp.float32) * scale
        p = jax.nn.softmax(s, axis=-1)
        a = p[:, :, 0] - lam * p[:, :, 1]
        return jnp.einsum('bhqk,bkhv->bqhv', a.astype(vals.dtype), vals)

    o = lax.map(block, qb)
    return jnp.moveaxis(o, 0, 1).reshape(B, Tq, H, DA_HEAD_V)


def da_output(o, w, lam_init, dtype):
    B, T = o.shape[:2]
    return (rms_norm(o, w) * (1.0 - lam_init)).reshape(B, T, DA_WIDTH).astype(dtype)


def hgrn_gates(z, lb):
    zf = z.astype(jnp.float32)
    f = lb + (1.0 - lb) * jax.nn.sigmoid(zf)
    return jnp.log(f), (1.0 - lb) * jax.nn.sigmoid(-zf)


def hgrn2_chunk_scan(q, k, logf, v, s0):
    f32 = jnp.float32
    B, T, H, _ = q.shape
    nc = T // HG_CHUNK
    L = HG_CHUNK

    def to_chunks(a):
        return a.astype(f32).reshape(B, nc, L, H, a.shape[-1]).transpose(1, 0, 3, 2, 4)

    mask = jnp.tril(jnp.ones((L, L), dtype=bool))[:, :, None]

    def step(S, xs):
        qc, kc, gc, vc = xs
        b = jnp.cumsum(gc, axis=-2)
        rel = jnp.where(mask, b[:, :, :, None, :] - b[:, :, None, :, :], -jnp.inf)
        decay = jnp.exp(rel)
        intra = jnp.einsum('bhtd,bhsd,bhtsd->bhts', qc, kc, decay)
        o = jnp.einsum('bhts,bhsv->bhtv', intra, vc) + jnp.einsum('bhtd,bhdv->bhtv', qc * jnp.exp(b), S)
        b_last = b[:, :, -1:, :]
        S_new = jnp.exp(b_last[:, :, 0, :, None]) * S + jnp.einsum('bhsd,bhsv->bhdv', kc * jnp.exp(b_last - b), vc)
        return S_new, o

    S, o = lax.scan(step, s0, (to_chunks(q), to_chunks(k), to_chunks(logf), to_chunks(v)))
    o = o.transpose(1, 0, 3, 2, 4).reshape(B, T, H, v.shape[-1])
    return o, S


def hg_output(o, g, w, dtype):
    B, T = o.shape[:2]
    return (rms_norm(o, w).reshape(B, T, HG_WIDTH) * jax.nn.silu(g.astype(jnp.float32))).astype(dtype)


def finish_layer(h, mixed, gt1, sh2, sc2, gt2, w_out, ln1_g, ln1_b, w_ff1, w_ff2, ln2_g, ln2_b):
    y = mixed @ w_out
    h = layer_norm(ALPHA * h + gt1 * y, ln1_g, ln1_b)
    u = modulate(h, sh2, sc2)
    ff = jnp.square(jax.nn.relu(u @ w_ff1)) @ w_ff2
    return layer_norm(ALPHA * h + gt2 * ff, ln2_g, ln2_b)


def setup_inputs(seed: int = 0) -> dict:
    key = jax.random.key(seed)
    ks = jax.random.split(key, 22)
    f32 = jnp.float32

    def nrm(k, shape, std):
        return std * jax.random.normal(k, shape, f32)

    return {
        'x': nrm(ks[0], (BATCH, SEQ, D_MODEL), 1.0),
        'c': nrm(ks[1], (BATCH, D_MODEL), 1.0),
        'ctx': nrm(ks[2], (BATCH, CTX_LEN, D_MODEL), 1.0),
        'c_ctx': nrm(ks[3], (D_MODEL,), 1.0),
        'w_ada': nrm(ks[4], (DEPTH, D_MODEL, N_ADA * D_MODEL), 0.5 * D_MODEL ** -0.5),
        'b_ada': nrm(ks[5], (DEPTH, N_ADA * D_MODEL), 0.02),
        'w_in': nrm(ks[6], (DEPTH, D_MODEL, IN_WIDTH), D_MODEL ** -0.5),
        'lam_q1': nrm(ks[7], (DEPTH, DA_HEAD_QK), 0.1),
        'lam_k1': nrm(ks[8], (DEPTH, DA_HEAD_QK), 0.1),
        'lam_q2': nrm(ks[9], (DEPTH, DA_HEAD_QK), 0.1),
        'lam_k2': nrm(ks[10], (DEPTH, DA_HEAD_QK), 0.1),
        'da_norm_w': 1.0 + nrm(ks[11], (DEPTH, DA_HEAD_V), 0.02),
        'hg_lb_fwd': 1.0 + nrm(ks[12], (DEPTH + 1, HG_WIDTH), 0.1),
        'hg_lb_bwd': 1.0 + nrm(ks[13], (DEPTH + 1, HG_WIDTH), 0.1),
        'hg_norm_w': 1.0 + nrm(ks[14], (DEPTH, HG_HEAD_I), 0.02),
        'w_out': nrm(ks[15], (DEPTH, MIX_WIDTH, D_MODEL), BETA * MIX_WIDTH ** -0.5),
        'ln1_g': 1.0 + nrm(ks[16], (DEPTH, D_MODEL), 0.02),
        'ln1_b': nrm(ks[17], (DEPTH, D_MODEL), 0.02),
        'w_ff1': nrm(ks[18], (DEPTH, D_MODEL, D_FF), D_MODEL ** -0.5),
        'w_ff2': nrm(ks[19], (DEPTH, D_FF, D_MODEL), BETA * D_FF ** -0.5),
        'ln2_g': 1.0 + nrm(ks[20], (DEPTH, D_MODEL), 0.02),
        'ln2_b': nrm(ks[21], (DEPTH, D_MODEL), 0.02),
    }


def reference(x, c, ctx, c_ctx, w_ada, b_ada, w_in, lam_q1, lam_k1, lam_q2, lam_k2,
              da_norm_w, hg_lb_fwd, hg_lb_bwd, hg_norm_w, w_out, ln1_g, ln1_b,
              w_ff1, w_ff2, ln2_g, ln2_b):
    f32 = jnp.float32
    B, T, _ = x.shape
    rows = T // GRID_W
    rope = axial_rope_tables(rows)
    lb_fwd_all = jnp.cumsum(jax.nn.softmax(hg_lb_fwd.astype(f32), axis=0), axis=0)
    lb_bwd_all = jnp.cumsum(jax.nn.softmax(hg_lb_bwd.astype(f32), axis=0), axis=0)
    s0 = jnp.zeros((B, HG_HEADS, HG_EXPAND, HG_HEAD_I), f32)
    flip = lambda a: jnp.flip(a, axis=1)

    h, hc = x, ctx
    for l in range(DEPTH):
        lam_init = 0.8 - 0.6 * math.exp(-0.3 * l)
        lam = (jnp.exp(jnp.sum(lam_q1[l].astype(f32) * lam_k1[l].astype(f32)))
               - jnp.exp(jnp.sum(lam_q2[l].astype(f32) * lam_k2[l].astype(f32))) + lam_init)
        m = adaln(c, w_ada[l], b_ada[l])
        mc = adaln(c_ctx[None], w_ada[l], b_ada[l])
        sh1, sc1, gt1, sh2, sc2, gt2 = [m[:, j, None, :] for j in range(N_ADA)]
        csh1, csc1, cgt1, csh2, csc2, cgt2 = [mc[:, j, None, :] for j in range(N_ADA)]

        qd, kd, vd, qh, zf, zb, ih, gh = split_projection(modulate(h, sh1, sc1) @ w_in[l])
        qd_c, kd_c, vd_c, qh_c, zf_c, zb_c, ih_c, gh_c = split_projection(modulate(hc, csh1, csc1) @ w_in[l])

        qd = apply_axial_rope(qd, rope)
        kd = apply_axial_rope(kd, rope)
        keys = jnp.concatenate([kd, kd_c], axis=1)
        vals = jnp.concatenate([vd, vd_c], axis=1)
        o_da = da_output(diff_attend(qd, keys, vals, lam), da_norm_w[l], lam_init, h.dtype)

        lbf = lb_fwd_all[l].reshape(HG_HEADS, HG_EXPAND)
        lbb = lb_bwd_all[l].reshape(HG_HEADS, HG_EXPAND)
        logf_f, k_f = hgrn_gates(zf, lbf)
        logf_b, k_b = hgrn_gates(zb, lbb)
        logf_fc, k_fc = hgrn_gates(zf_c, lbf)
        logf_bc, k_bc = hgrn_gates(zb_c, lbb)
        o_fc, s_fc = hgrn2_chunk_scan(qh_c, k_fc, logf_fc, ih_c, s0)
        o_f, _ = hgrn2_chunk_scan(qh, k_f, logf_f, ih, s_fc)
        o_bc, s_bc = hgrn2_chunk_scan(flip(qh_c), flip(k_bc), flip(logf_bc), flip(ih_c), s0)
        o_b, _ = hgrn2_chunk_scan(flip(qh), flip(k_b), flip(logf_b), flip(ih), s_bc)
        o_hg = hg_output(o_f + flip(o_b), gh, hg_norm_w[l], h.dtype)

        h_new = finish_layer(h, jnp.concatenate([o_da, o_hg], axis=-1), gt1, sh2, sc2, gt2,
                             w_out[l], ln1_g[l], ln1_b[l], w_ff1[l], w_ff2[l], ln2_g[l], ln2_b[l])

        if l + 1 < DEPTH:
            o_da_c = da_output(diff_attend(qd_c, kd_c, vd_c, lam), da_norm_w[l], lam_init, hc.dtype)
            o_hg_c = hg_output(o_fc + flip(o_bc), gh_c, hg_norm_w[l], hc.dtype)
            hc = finish_layer(hc, jnp.concatenate([o_da_c, o_hg_c], axis=-1), cgt1, csh2, csc2, cgt2,
                              w_out[l], ln1_g[l], ln1_b[l], w_ff1[l], w_ff2[l], ln2_g[l], ln2_b[l])
        h = h_new
    return h
```

```python
import functools
import math

import jax
import jax.numpy as jnp
from jax import lax
from jax.experimental import pallas as pl
from jax.experimental.pallas import tpu as pltpu

F32 = jnp.float32
BF16 = jnp.bfloat16

D_MODEL = 2048
GRID_W = 64
HEAD = 128
N_HEADS = 8
GROUP_W = N_HEADS * HEAD
N_GROUPS = 8
D_QK = 64
ROT_AXIS = 32
ROPE_BASE = 10000.0
D_FF = 4 * D_MODEL
N_ADA = 6
EPS = 1e-5
DEPTH = 1
ALPHA = (2.0 * DEPTH) ** 0.25
LAM_INIT = 0.8 - 0.6 * math.exp(-0.3 * 0)
HG_CHUNK = 64
HG_SUB = 16
NEG_BIG = -1e30

VMEM_LIMIT = 56 * 1024 * 1024


def _cparams(sem):
    return pltpu.CompilerParams(dimension_semantics=sem, vmem_limit_bytes=VMEM_LIMIT)


def _sigmoid(z):
    return 1.0 / (1.0 + jnp.exp(-z))


def _dot_nt(a, b):
    return lax.dot_general(a, b, (((1,), (1,)), ((), ())), preferred_element_type=F32)


def _dot_tn(a, b):
    return lax.dot_general(a, b, (((0,), (0,)), ((), ())), preferred_element_type=F32)


def _adaln_kernel(cond_ref, w_ref, b_ref, o_ref):
    c = cond_ref[...]
    s = (c * _sigmoid(c)).astype(BF16)
    o_ref[...] = jnp.dot(s, w_ref[...].astype(BF16), preferred_element_type=F32) + b_ref[...]


def _adaln(cond, w, b):
    rows, d = cond.shape
    n = w.shape[1]
    tn = 1024
    return pl.pallas_call(
        _adaln_kernel,
        grid=(n // tn,),
        in_specs=[pl.BlockSpec((rows, d), lambda j: (0, 0)),
                  pl.BlockSpec((d, tn), lambda j: (0, j)),
                  pl.BlockSpec((1, tn), lambda j: (0, j))],
        out_specs=pl.BlockSpec((rows, tn), lambda j: (0, j)),
        out_shape=jax.ShapeDtypeStruct((rows, n), F32),
        compiler_params=_cparams(("arbitrary",)),
        name="adaln",
    )(cond, w, b)


def _rope_tables(seq, scale):
    t = jnp.arange(seq, dtype=jnp.int32)
    row = (t // GRID_W).astype(F32)
    col = (t % GRID_W).astype(F32)
    lane = jnp.arange(HEAD, dtype=jnp.int32)
    inv_freq = ROPE_BASE ** (-jnp.arange(0, ROT_AXIS, 2, dtype=F32) / ROT_AXIS)
    freq = inv_freq[lane % (ROT_AXIS // 2)]
    use_col = ((lane % D_QK) // ROT_AXIS) == 1
    pos = jnp.where(use_col[None, :], col[:, None], row[:, None])
    ang = pos * freq[None, :]
    first_half = (lane % ROT_AXIS) < (ROT_AXIS // 2)
    cos, sin = jnp.cos(ang), jnp.sin(ang)
    t1 = jnp.where(first_half[None, :], -sin, 0.0)
    t2 = jnp.where(first_half[None, :], 0.0, sin)
    return (jnp.stack([cos, t1, t2]) * scale).astype(F32)


def _inproj_kernel(*refs, kinds, n_rope):
    x_ref, sh_ref, sc_ref, w_ref = refs[:4]
    rope_refs = refs[4:4 + n_rope]
    lb_ref = refs[4 + n_rope] if "logf" in kinds else None
    o_ref, u_scr = refs[-2], refs[-1]
    j = pl.program_id(1)

    @pl.when(j == 0)
    def _():
        u_scr[...] = (x_ref[...] * (1.0 + sc_ref[0]) + sh_ref[0]).astype(BF16)

    acc = jnp.dot(u_scr[...], w_ref[...], preferred_element_type=F32)

    def epilogue(kind, rope_ref):
        if kind == "rope":
            for h in range(N_HEADS):
                xh = acc[:, h * HEAD:(h + 1) * HEAD]
                y = (xh * rope_ref[0]
                     + pltpu.roll(xh, HEAD - ROT_AXIS // 2, 1) * rope_ref[1]
                     + pltpu.roll(xh, ROT_AXIS // 2, 1) * rope_ref[2])
                o_ref[:, h * HEAD:(h + 1) * HEAD] = y.astype(o_ref.dtype)
        elif kind == "copy":
            o_ref[...] = acc.astype(o_ref.dtype)
        elif kind == "silu":
            o_ref[...] = (acc * _sigmoid(acc)).astype(o_ref.dtype)
        elif kind == "logf":
            a = lb_ref[0]
            e = jnp.exp(a - jnp.max(a, axis=0, keepdims=True))
            lb = e[0:1] / jnp.sum(e, axis=0, keepdims=True)
            o_ref[...] = jnp.log(lb + (1.0 - lb) * _sigmoid(acc)).astype(o_ref.dtype)
        else:
            raise ValueError(kind)

    if len(set(kinds)) == 1 and kinds[0] != "rope":
        epilogue(kinds[0], None)
    else:
        ri = 0
        for g, kind in enumerate(kinds):
            rope_ref = None
            if kind == "rope":
                rope_ref = rope_refs[ri]
                ri += 1
            pl.when(j == g)(functools.partial(epilogue, kind, rope_ref))


def _inproj(x2d, sh, sc, w, groups, kinds, out_dtype, *, rows_per_mod, tm, ropes=(), seq=None, lbs=None):
    m, d = x2d.shape
    ng = len(groups)
    assert m % tm == 0 and rows_per_mod % tm == 0
    tiles_per_mod = rows_per_mod // tm
    def wmap(i, j):
        col = groups[0]
        for k in range(1, ng):
            col = jnp.where(j == k, groups[k], col)
        return (0, col)

    in_specs = [pl.BlockSpec((tm, d), lambda i, j: (i, 0)),
                pl.BlockSpec((1, 1, d), lambda i, j: (i // tiles_per_mod, 0, 0)),
                pl.BlockSpec((1, 1, d), lambda i, j: (i // tiles_per_mod, 0, 0)),
                pl.BlockSpec((d, GROUP_W), wmap)]
    args = [x2d, sh, sc, w]
    for tab in ropes:
        tiles_per_seq = seq // tm
        in_specs.append(pl.BlockSpec((3, tm, HEAD), lambda i, j: (0, i % tiles_per_seq, 0)))
        args.append(tab)
    if lbs is not None:
        in_specs.append(pl.BlockSpec((1, DEPTH + 1, GROUP_W), lambda i, j: (j, 0, 0)))
        args.append(lbs)
    return pl.pallas_call(
        functools.partial(_inproj_kernel, kinds=tuple(kinds), n_rope=len(ropes)),
        grid=(m // tm, ng),
        in_specs=in_specs,
        out_specs=pl.BlockSpec((tm, GROUP_W), lambda i, j: (i, j)),
        out_shape=jax.ShapeDtypeStruct((m, ng * GROUP_W), out_dtype),
        scratch_shapes=[pltpu.VMEM((tm, d), BF16)],
        compiler_params=_cparams(("arbitrary", "arbitrary")),
        name="inproj_" + "_".join(kinds),
    )(*args)


def _attn_kernel(q_ref, k_ref, v_ref, kc_ref, vc_ref, lq1_ref, lk1_ref, lq2_ref, lk2_ref, nw_ref, o_ref):
    tq = q_ref.shape[0]
    lam = (jnp.exp(jnp.sum(lq1_ref[...] * lk1_ref[...], axis=1, keepdims=True))
           - jnp.exp(jnp.sum(lq2_ref[...] * lk2_ref[...], axis=1, keepdims=True)) + LAM_INIT)

    q = q_ref[...]
    lane = lax.broadcasted_iota(jnp.int32, q.shape, 1)
    zero = jnp.zeros_like(q)
    q2 = jnp.concatenate([jnp.where(lane < D_QK, q, zero), jnp.where(lane >= D_QK, q, zero)], axis=0)
    s = _dot_nt(q2, k_ref[...])
    sc = _dot_nt(q2, kc_ref[...])
    m = jnp.maximum(jnp.max(s, axis=1, keepdims=True), jnp.max(sc, axis=1, keepdims=True))
    p = jnp.exp(s - m)
    pc = jnp.exp(sc - m)
    l = jnp.sum(p, axis=1, keepdims=True) + jnp.sum(pc, axis=1, keepdims=True)
    r = 1.0 / l
    r1 = r[:tq]
    r2 = r[tq:] * lam
    a = (p[:tq] * r1 - p[tq:] * r2).astype(BF16)
    ac = (pc[:tq] * r1 - pc[tq:] * r2).astype(BF16)
    o = (jnp.dot(a, v_ref[...], preferred_element_type=F32)
         + jnp.dot(ac, vc_ref[...], preferred_element_type=F32))
    ms = jnp.mean(o * o, axis=1, keepdims=True)
    o_ref[...] = (o * lax.rsqrt(ms + EPS) * nw_ref[...] * (1.0 - LAM_INIT)).astype(o_ref.dtype)


def _attention(qkv, kv_ctx, lq1, lk1, lq2, lk2, norm_w, *, batch, seq, ctx_len, tq):
    nq = seq // tq
    vec = lambda n: pl.BlockSpec((1, n), lambda b, h, i: (0, 0))
    return pl.pallas_call(
        _attn_kernel,
        grid=(batch, N_HEADS, nq),
        in_specs=[pl.BlockSpec((tq, HEAD), lambda b, h, i: (b * nq + i, h)),
                  pl.BlockSpec((seq, HEAD), lambda b, h, i: (b, N_HEADS + h)),
                  pl.BlockSpec((seq, HEAD), lambda b, h, i: (b, 2 * N_HEADS + h)),
                  pl.BlockSpec((ctx_len, HEAD), lambda b, h, i: (b, h)),
                  pl.BlockSpec((ctx_len, HEAD), lambda b, h, i: (b, N_HEADS + h)),
                  vec(D_QK), vec(D_QK), vec(D_QK), vec(D_QK), vec(HEAD)],
        out_specs=pl.BlockSpec((tq, HEAD), lambda b, h, i: (b * nq + i, h)),
        out_shape=jax.ShapeDtypeStruct((batch * seq, GROUP_W), BF16),
        compiler_params=_cparams(("arbitrary", "arbitrary", "arbitrary")),
        name="diff_attn",
    )(qkv, qkv, qkv, kv_ctx, kv_ctx, lq1, lk1, lq2, lk2, norm_w)


def _split3(x):
    hi = x.astype(BF16)
    r1 = x - hi.astype(F32)
    mid = r1.astype(BF16)
    lo = (r1 - mid.astype(F32)).astype(BF16)
    return hi, mid, lo


def _cumsum_rows(tri, x):
    hi, mid, lo = _split3(x)
    return (jnp.dot(tri, hi, preferred_element_type=F32)
            + jnp.dot(tri, mid, preferred_element_type=F32)
            + jnp.dot(tri, lo, preferred_element_type=F32))


def _tri(n, lower):
    r = lax.broadcasted_iota(jnp.int32, (n, n), 0)
    c = lax.broadcasted_iota(jnp.int32, (n, n), 1)
    return jnp.where((r >= c) if lower else (r <= c), 1.0, 0.0).astype(BF16)


def _hgrn_ctx_state(logf, v, forward):
    n = logf.shape[0]
    b = _cumsum_rows(_tri(n, forward), logf)
    b_end = b[n - 1:n] if forward else b[0:1]
    k_hat = ((1.0 - jnp.exp(logf)) * jnp.exp(b_end - b)).astype(BF16)
    return _dot_tn(v, k_hat)


def _hgrn_chunk(q, logf, v, s_ref, tri, forward):
    L = HG_CHUNK
    b = _cumsum_rows(tri, logf)
    b_end = b[L - 1:L] if forward else b[0:1]
    k = 1.0 - jnp.exp(logf)
    qf = q.astype(F32)

    s_t = s_ref[...]
    o = _dot_nt((qf * jnp.exp(b)).astype(BF16), s_t.astype(BF16))

    row = lax.broadcasted_iota(jnp.int32, (L, HEAD), 0)
    qs, ks = [], []
    for j in range(L // HG_SUB):
        lo, hi = j * HG_SUB, (j + 1) * HG_SUB
        ref = b[lo + HG_SUB // 2:lo + HG_SUB // 2 + 1]
        q_rows = (row >= lo) if forward else (row < hi)
        k_rows = (row >= lo) & (row < hi)
        qs.append((qf * jnp.exp(jnp.where(q_rows, b - ref, NEG_BIG))).astype(BF16))
        ks.append((k * jnp.exp(jnp.where(k_rows, ref - b, NEG_BIG))).astype(BF16))
    a = _dot_nt(jnp.concatenate(qs, axis=1), jnp.concatenate(ks, axis=1))
    ar = lax.broadcasted_iota(jnp.int32, (L, L), 0)
    ac = lax.broadcasted_iota(jnp.int32, (L, L), 1)
    a = jnp.where((ar >= ac) if forward else (ar <= ac), a, 0.0)
    o = o + jnp.dot(a.astype(BF16), v, preferred_element_type=F32)

    k_hat = (k * jnp.exp(b_end - b)).astype(BF16)
    s_ref[...] = s_t * jnp.exp(b_end) + _dot_tn(v, k_hat)
    return o


def _hgrn_kernel(q_ref, lf_ref, lb_ref, v_ref, g_ref, lfc_ref, lbc_ref, vc_ref, nw_ref, o_ref,
                 of_scr, ob_scr, sf_scr, sb_scr):
    seq = q_ref.shape[0]
    nc = seq // HG_CHUNK
    vc = vc_ref[...]
    sf_scr[...] = _hgrn_ctx_state(lfc_ref[...], vc, True)
    sb_scr[...] = _hgrn_ctx_state(lbc_ref[...], vc, False)
    tri_lo = _tri(HG_CHUNK, True)
    tri_up = _tri(HG_CHUNK, False)

    def body(c, carry):
        rf = pl.ds(pl.multiple_of(c * HG_CHUNK, HG_CHUNK), HG_CHUNK)
        rb = pl.ds(pl.multiple_of((nc - 1 - c) * HG_CHUNK, HG_CHUNK), HG_CHUNK)
        of_scr[rf, :] = _hgrn_chunk(q_ref[rf, :], lf_ref[rf, :], v_ref[rf, :], sf_scr, tri_lo, True)
        ob_scr[rb, :] = _hgrn_chunk(q_ref[rb, :], lb_ref[rb, :], v_ref[rb, :], sb_scr, tri_up, False)
        return carry

    lax.fori_loop(0, nc, body, 0)

    o = of_scr[...] + ob_scr[...]
    ms = jnp.mean(o * o, axis=1, keepdims=True)
    o_ref[...] = (o * lax.rsqrt(ms + EPS) * nw_ref[...] * g_ref[...].astype(F32)).astype(o_ref.dtype)


def _hgrn(hq, logf, hq_ctx, logf_ctx, norm_w, *, batch, seq, ctx_len):
    nh = N_HEADS
    return pl.pallas_call(
        _hgrn_kernel,
        grid=(batch, nh),
        in_specs=[pl.BlockSpec((seq, HEAD), lambda b, h: (b, h)),
                  pl.BlockSpec((seq, HEAD), lambda b, h: (b, h)),
                  pl.BlockSpec((seq, HEAD), lambda b, h: (b, nh + h)),
                  pl.BlockSpec((seq, HEAD), lambda b, h: (b, nh + h)),
                  pl.BlockSpec((seq, HEAD), lambda b, h: (b, 2 * nh + h)),
                  pl.BlockSpec((ctx_len, HEAD), lambda b, h: (b, h)),
                  pl.BlockSpec((ctx_len, HEAD), lambda b, h: (b, nh + h)),
                  pl.BlockSpec((ctx_len, HEAD), lambda b, h: (b, 2 * nh + h)),
                  pl.BlockSpec((1, HEAD), lambda b, h: (0, 0))],
        out_specs=pl.BlockSpec((seq, HEAD), lambda b, h: (b, h)),
        out_shape=jax.ShapeDtypeStruct((batch * seq, GROUP_W), BF16),
        scratch_shapes=[pltpu.VMEM((seq, HEAD), F32), pltpu.VMEM((seq, HEAD), F32),
                        pltpu.VMEM((HEAD, HEAD), F32), pltpu.VMEM((HEAD, HEAD), F32)],
        compiler_params=_cparams(("arbitrary", "arbitrary")),
        name="hgrn2",
    )(hq, logf, logf, hq, hq, logf_ctx, logf_ctx, hq_ctx, norm_w)


def _layer_norm(z, g, b):
    mu = jnp.mean(z, axis=1, keepdims=True)
    zc = z - mu
    var = jnp.mean(zc * zc, axis=1, keepdims=True)
    return zc * lax.rsqrt(var + EPS) * g + b


def _outproj_kernel(oa_ref, oh_ref, x_ref, w_ref, gt_ref, sh_ref, sc_ref, g_ref, b_ref, h_ref, u_ref):
    y = (jnp.dot(oa_ref[...], w_ref[:GROUP_W, :], preferred_element_type=F32)
         + jnp.dot(oh_ref[...], w_ref[GROUP_W:, :], preferred_element_type=F32))
    h = _layer_norm(ALPHA * x_ref[...] + gt_ref[0] * y, g_ref[...], b_ref[...])
    h_ref[...] = h
    u_ref[...] = (h * (1.0 + sc_ref[0]) + sh_ref[0]).astype(u_ref.dtype)


def _outproj(o_da, o_hg, x2d, w_out, gt1, sh2, sc2, ln_g, ln_b, *, seq, tm):
    m, d = x2d.shape
    tiles_per_seq = seq // tm
    mod = pl.BlockSpec((1, 1, d), lambda i: (i // tiles_per_seq, 0, 0))
    vec = pl.BlockSpec((1, d), lambda i: (0, 0))
    return pl.pallas_call(
        _outproj_kernel,
        grid=(m // tm,),
        in_specs=[pl.BlockSpec((tm, GROUP_W), lambda i: (i, 0)),
                  pl.BlockSpec((tm, GROUP_W), lambda i: (i, 0)),
                  pl.BlockSpec((tm, d), lambda i: (i, 0)),
                  pl.BlockSpec((d, d), lambda i: (0, 0)),
                  mod, mod, mod, vec, vec],
        out_specs=[pl.BlockSpec((tm, d), lambda i: (i, 0)), pl.BlockSpec((tm, d), lambda i: (i, 0))],
        out_shape=[jax.ShapeDtypeStruct((m, d), F32), jax.ShapeDtypeStruct((m, d), BF16)],
        compiler_params=_cparams(("arbitrary",)),
        name="outproj_ln1",
    )(o_da, o_hg, x2d, w_out, gt1, sh2, sc2, ln_g, ln_b)


def _ffn_kernel(u_ref, h_ref, w1_ref, w2_ref, gt_ref, g_ref, b_ref, o_ref, acc_scr):
    j = pl.program_id(1)
    a = jnp.maximum(jnp.dot(u_ref[...], w1_ref[...], preferred_element_type=F32), 0.0)
    part = jnp.dot((a * a).astype(BF16), w2_ref[...], preferred_element_type=F32)

    @pl.when(j == 0)
    def _():
        acc_scr[...] = part

    @pl.when(j > 0)
    def _():
        acc_scr[...] += part

    @pl.when(j == pl.num_programs(1) - 1)
    def _():
        o_ref[...] = _layer_norm(ALPHA * h_ref[...] + gt_ref[0] * acc_scr[...], g_ref[...], b_ref[...])


def _ffn(u, h, w1, w2, gt2, ln_g, ln_b, *, seq, tm, tf):
    m, d = h.shape
    dff = w1.shape[1]
    tiles_per_seq = seq // tm
    return pl.pallas_call(
        _ffn_kernel,
        grid=(m // tm, dff // tf),
        in_specs=[pl.BlockSpec((tm, d), lambda i, j: (i, 0)),
                  pl.BlockSpec((tm, d), lambda i, j: (i, 0)),
                  pl.BlockSpec((d, tf), lambda i, j: (0, j)),
                  pl.BlockSpec((tf, d), lambda i, j: (j, 0)),
                  pl.BlockSpec((1, 1, d), lambda i, j: (i // tiles_per_seq, 0, 0)),
                  pl.BlockSpec((1, d), lambda i, j: (0, 0)),
                  pl.BlockSpec((1, d), lambda i, j: (0, 0))],
        out_specs=pl.BlockSpec((tm, d), lambda i, j: (i, 0)),
        out_shape=jax.ShapeDtypeStruct((m, d), F32),
        scratch_shapes=[pltpu.VMEM((tm, d), F32)],
        compiler_params=_cparams(("arbitrary", "arbitrary")),
        name="ffn_ln2",
    )(u, h, w1, w2, gt2, ln_g, ln_b)


def kernel(x, c, ctx, c_ctx, w_ada, b_ada, w_in, lam_q1, lam_k1, lam_q2, lam_k2, da_norm_w,
           hg_lb_fwd, hg_lb_bwd, hg_norm_w, w_out, ln1_g, ln1_b, w_ff1, w_ff2, ln2_g, ln2_b):
    batch, seq, d = x.shape
    ctx_len = ctx.shape[1]
    assert d == D_MODEL and w_in.shape[0] == DEPTH and seq % GRID_W == 0
    l = 0

    n_cond = batch + 1
    pad = (-n_cond) % 8
    cond = jnp.concatenate([c, c_ctx[None], jnp.zeros((pad, d), F32)], axis=0)
    mod = _adaln(cond, w_ada[l], b_ada[l][None]).reshape(n_cond + pad, N_ADA, 1, d)
    sh1, sc1, gt1, sh2, sc2, gt2 = [mod[:batch, j] for j in range(N_ADA)]
    csh1, csc1 = mod[batch:batch + 1, 0], mod[batch:batch + 1, 1]

    w_in_b = w_in[l].astype(BF16)
    x2d = x.reshape(batch * seq, d)
    ctx2d = ctx.reshape(batch * ctx_len, d)
    lbs = jnp.stack([hg_lb_fwd.reshape(DEPTH + 1, GROUP_W), hg_lb_bwd.reshape(DEPTH + 1, GROUP_W)])

    tm = min(512, seq)
    tmc = min(512, ctx_len)
    rope_q = _rope_tables(seq, D_QK ** -0.5)
    rope_k = _rope_tables(seq, 1.0)
    qkv = _inproj(x2d, sh1, sc1, w_in_b, (0, 1, 2), ("rope", "rope", "copy"), BF16,
                  rows_per_mod=seq, tm=tm, ropes=(rope_q, rope_k), seq=seq)
    hq = _inproj(x2d, sh1, sc1, w_in_b, (3, 6, 7), ("silu", "copy", "silu"), BF16, rows_per_mod=seq, tm=tm)
    logf = _inproj(x2d, sh1, sc1, w_in_b, (4, 5), ("logf", "logf"), F32, rows_per_mod=seq, tm=tm, lbs=lbs)
    m_ctx = batch * ctx_len
    kvh_ctx = _inproj(ctx2d, csh1, csc1, w_in_b, (1, 2, 6), ("copy", "copy", "copy"), BF16,
                      rows_per_mod=m_ctx, tm=tmc)
    logf_ctx = _inproj(ctx2d, csh1, csc1, w_in_b, (4, 5), ("logf", "logf"), F32,
                       rows_per_mod=m_ctx, tm=tmc, lbs=lbs)

    o_da = _attention(qkv, kvh_ctx, lam_q1[l][None], lam_k1[l][None], lam_q2[l][None], lam_k2[l][None],
                      da_norm_w[l][None], batch=batch, seq=seq, ctx_len=ctx_len, tq=min(256, seq))
    o_hg = _hgrn(hq, logf, kvh_ctx, logf_ctx, hg_norm_w[l][None], batch=batch, seq=seq, ctx_len=ctx_len)

    h1, u2 = _outproj(o_da, o_hg, x2d, w_out[l].astype(BF16), gt1, sh2, sc2, ln1_g[l][None], ln1_b[l][None],
                      seq=seq, tm=min(256, seq))
    out = _ffn(u2, h1, w_ff1[l].astype(BF16), w_ff2[l].astype(BF16), gt2, ln2_g[l][None], ln2_b[l][None],
               seq=seq, tm=tm, tf=1024)
    return out.reshape(batch, seq, d)
```

```python
import functools
import math

import jax
import jax.numpy as jnp
from jax import lax
from jax.experimental import pallas as pl
from jax.experimental.pallas import tpu as pltpu

F32 = jnp.float32
BF16 = jnp.bfloat16

D_MODEL = 2048
GRID_W = 64
HEAD = 128
N_HEADS = 8
GROUP_W = N_HEADS * HEAD
N_GROUPS = 8
D_QK = 64
ROT_AXIS = 32
ROPE_BASE = 10000.0
D_FF = 4 * D_MODEL
N_ADA = 6
EPS = 1e-5
DEPTH = 1
ALPHA = (2.0 * DEPTH) ** 0.25
LAM_INIT = 0.8 - 0.6 * math.exp(-0.3 * 0)
HG_CHUNK = 64
HG_SUB = 16
HG_GROUP = 8
ATTN_SUB = 128
INPROJ_SUB = 256
LOG2E = 1.4426950408889634
NEG_BIG = -1e30

VMEM_LIMIT = 56 * 1024 * 1024


def _cparams(sem):
    return pltpu.CompilerParams(dimension_semantics=sem, vmem_limit_bytes=VMEM_LIMIT)


def _sigmoid(z):
    return 1.0 / (1.0 + jnp.exp(-z))


def _dot_nt(a, b):
    return lax.dot_general(a, b, (((1,), (1,)), ((), ())), preferred_element_type=F32)


def _dot_tn(a, b):
    return lax.dot_general(a, b, (((0,), (0,)), ((), ())), preferred_element_type=F32)


def _adaln_kernel(cond_ref, w_ref, b_ref, o_ref):
    c = cond_ref[...]
    s = (c * _sigmoid(c)).astype(BF16)
    o_ref[...] = jnp.dot(s, w_ref[...].astype(BF16), preferred_element_type=F32) + b_ref[...]


def _adaln(cond, w, b):
    rows, d = cond.shape
    n = w.shape[1]
    tn = 1024
    return pl.pallas_call(
        _adaln_kernel,
        grid=(n // tn,),
        in_specs=[pl.BlockSpec((rows, d), lambda j: (0, 0)),
                  pl.BlockSpec((d, tn), lambda j: (0, j)),
                  pl.BlockSpec((1, tn), lambda j: (0, j))],
        out_specs=pl.BlockSpec((rows, tn), lambda j: (0, j)),
        out_shape=jax.ShapeDtypeStruct((rows, n), F32),
        compiler_params=_cparams(("arbitrary",)),
        name="adaln",
    )(cond, w, b)


def _rope_tables(seq, scale):
    t = jnp.arange(seq, dtype=jnp.int32)
    row = (t // GRID_W).astype(F32)
    col = (t % GRID_W).astype(F32)
    lane = jnp.arange(HEAD, dtype=jnp.int32)
    inv_freq = ROPE_BASE ** (-jnp.arange(0, ROT_AXIS, 2, dtype=F32) / ROT_AXIS)
    freq = inv_freq[lane % (ROT_AXIS // 2)]
    use_col = ((lane % D_QK) // ROT_AXIS) == 1
    pos = jnp.where(use_col[None, :], col[:, None], row[:, None])
    ang = pos * freq[None, :]
    first_half = (lane % ROT_AXIS) < (ROT_AXIS // 2)
    cos, sin = jnp.cos(ang), jnp.sin(ang)
    t1 = jnp.where(first_half[None, :], -sin, 0.0)
    t2 = jnp.where(first_half[None, :], 0.0, sin)
    return (jnp.stack([cos, t1, t2]) * scale).astype(F32)


def _inproj_kernel(*refs, kinds, slots, n_rope, n_out):
    x_ref, sh_ref, sc_ref, w_ref = refs[:4]
    rope_refs = refs[4:4 + n_rope]
    lb_ref = refs[4 + n_rope] if "logf" in kinds else None
    out_refs = refs[-1 - n_out:-1]
    u_scr = refs[-1]
    j = pl.program_id(1)
    n_sub = GROUP_W // INPROJ_SUB
    heads_per_sub = INPROJ_SUB // HEAD

    @pl.when(j == 0)
    def _():
        u_scr[...] = (x_ref[...] * (1.0 + sc_ref[0]) + sh_ref[0]).astype(BF16)

    def sub_dot(k):
        return jnp.dot(u_scr[...], w_ref[:, k * INPROJ_SUB:(k + 1) * INPROJ_SUB], preferred_element_type=F32)

    def epilogue(kind, acc, k, o_ref, rope_ref):
        cols = slice(k * INPROJ_SUB, (k + 1) * INPROJ_SUB)
        if kind == "rope":
            for h in range(heads_per_sub):
                xh = acc[:, h * HEAD:(h + 1) * HEAD]
                y = (xh * rope_ref[0]
                     + pltpu.roll(xh, HEAD - ROT_AXIS // 2, 1) * rope_ref[1]
                     + pltpu.roll(xh, ROT_AXIS // 2, 1) * rope_ref[2])
                c0 = k * INPROJ_SUB + h * HEAD
                o_ref[:, c0:c0 + HEAD] = y.astype(o_ref.dtype)
        elif kind == "copy":
            o_ref[:, cols] = acc.astype(o_ref.dtype)
        elif kind == "silu":
            o_ref[:, cols] = (acc * _sigmoid(acc)).astype(o_ref.dtype)
        elif kind == "logf":
            a = lb_ref[0][:, cols]
            e = jnp.exp(a - jnp.max(a, axis=0, keepdims=True))
            lb = e[0:1] / jnp.sum(e, axis=0, keepdims=True)
            o_ref[:, cols] = jnp.log(lb + (1.0 - lb) * _sigmoid(acc)).astype(o_ref.dtype)
        else:
            raise ValueError(kind)

    def group(kind, o_ref, rope_ref):
        nxt = sub_dot(0)
        for k in range(n_sub):
            acc = nxt
            if k + 1 < n_sub:
                nxt = sub_dot(k + 1)
            epilogue(kind, acc, k, o_ref, rope_ref)

    ri = 0
    for g, kind in enumerate(kinds):
        rope_ref = None
        if kind == "rope":
            rope_ref = rope_refs[ri]
            ri += 1
        pl.when(j == g)(functools.partial(group, kind, out_refs[slots[g]], rope_ref))


def _inproj(x2d, sh, sc, w, groups, kinds, outs, *, rows_per_mod, tm, ropes=(), seq=None, lbs=None, name):
    m, d = x2d.shape
    ng = len(groups)
    assert m % tm == 0 and rows_per_mod % tm == 0 and sum(n for _, n in outs) == ng
    tiles_per_mod = rows_per_mod // tm

    def wmap(i, j):
        col = groups[0]
        for k in range(1, ng):
            col = jnp.where(j == k, groups[k], col)
        return (0, col)

    in_specs = [pl.BlockSpec((tm, d), lambda i, j: (i, 0)),
                pl.BlockSpec((1, 1, d), lambda i, j: (i // tiles_per_mod, 0, 0)),
                pl.BlockSpec((1, 1, d), lambda i, j: (i // tiles_per_mod, 0, 0)),
                pl.BlockSpec((d, GROUP_W), wmap)]
    args = [x2d, sh, sc, w]
    for tab in ropes:
        tiles_per_seq = seq // tm
        in_specs.append(pl.BlockSpec((3, tm, HEAD), lambda i, j: (0, i % tiles_per_seq, 0)))
        args.append(tab)
    if lbs is not None:
        first_logf = kinds.index("logf")
        assert all(k == "logf" for k in kinds[first_logf:]) and lbs.shape[0] == ng - first_logf
        in_specs.append(pl.BlockSpec((1, DEPTH + 1, GROUP_W),
                                     lambda i, j: (jnp.clip(j - first_logf, 0, ng - first_logf - 1), 0, 0)))
        args.append(lbs)
    out_specs, out_shape, slots, start = [], [], [], 0
    for o, (dtype, n) in enumerate(outs):
        out_specs.append(pl.BlockSpec((tm, GROUP_W), functools.partial(
            lambda i, j, start, n: (i, jnp.clip(j - start, 0, n - 1)), start=start, n=n)))
        out_shape.append(jax.ShapeDtypeStruct((m, n * GROUP_W), dtype))
        slots += [o] * n
        start += n
    return pl.pallas_call(
        functools.partial(_inproj_kernel, kinds=tuple(kinds), slots=tuple(slots), n_rope=len(ropes),
                          n_out=len(outs)),
        grid=(m // tm, ng),
        in_specs=in_specs,
        out_specs=out_specs,
        out_shape=out_shape,
        scratch_shapes=[pltpu.VMEM((tm, d), BF16)],
        compiler_params=_cparams(("arbitrary", "arbitrary")),
        name=name,
    )(*args)


def _attn_kernel(q_ref, k_ref, v_ref, kc_ref, vc_ref, lq1_ref, lk1_ref, lq2_ref, lk2_ref, nw_ref, o_ref):
    lam = (jnp.exp(jnp.sum(lq1_ref[...] * lk1_ref[...], axis=1, keepdims=True))
           - jnp.exp(jnp.sum(lq2_ref[...] * lk2_ref[...], axis=1, keepdims=True)) + LAM_INIT)
    ts = min(ATTN_SUB, q_ref.shape[0])
    lane = lax.broadcasted_iota(jnp.int32, (ts, HEAD), 1)
    zero = jnp.zeros((ts, HEAD), BF16)
    n_sub = q_ref.shape[0] // ts

    def scores(r):
        q = q_ref[r * ts:(r + 1) * ts, :]
        q2 = jnp.concatenate([jnp.where(lane < D_QK, q, zero), jnp.where(lane >= D_QK, q, zero)], axis=0)
        return _dot_nt(q2, k_ref[...]), _dot_nt(q2, kc_ref[...])

    nxt = scores(0)
    for r in range(n_sub):
        s, sc = nxt
        if r + 1 < n_sub:
            nxt = scores(r + 1)
        m = jnp.maximum(jnp.max(s, axis=1, keepdims=True), jnp.max(sc, axis=1, keepdims=True))
        p = jnp.exp2(s - m)
        pc = jnp.exp2(sc - m)
        l = jnp.sum(p, axis=1, keepdims=True) + jnp.sum(pc, axis=1, keepdims=True)
        rcp = 1.0 / l
        r1 = rcp[:ts]
        r2 = rcp[ts:] * lam
        a = (p[:ts] * r1 - p[ts:] * r2).astype(BF16)
        ac = (pc[:ts] * r1 - pc[ts:] * r2).astype(BF16)
        o = (jnp.dot(a, v_ref[...], preferred_element_type=F32)
             + jnp.dot(ac, vc_ref[...], preferred_element_type=F32))
        ms = jnp.mean(o * o, axis=1, keepdims=True)
        o_ref[r * ts:(r + 1) * ts, :] = (
            o * lax.rsqrt(ms + EPS) * nw_ref[...] * (1.0 - LAM_INIT)).astype(o_ref.dtype)


def _attention(qkv, kv_ctx, lq1, lk1, lq2, lk2, norm_w, *, batch, seq, ctx_len, tq):
    nq = seq // tq
    vec = lambda n: pl.BlockSpec((1, n), lambda b, h, i: (0, 0))
    return pl.pallas_call(
        _attn_kernel,
        grid=(batch, N_HEADS, nq),
        in_specs=[pl.BlockSpec((tq, HEAD), lambda b, h, i: (b * nq + i, h)),
                  pl.BlockSpec((seq, HEAD), lambda b, h, i: (b, N_HEADS + h)),
                  pl.BlockSpec((seq, HEAD), lambda b, h, i: (b, 2 * N_HEADS + h)),
                  pl.BlockSpec((ctx_len, HEAD), lambda b, h, i: (b, h)),
                  pl.BlockSpec((ctx_len, HEAD), lambda b, h, i: (b, N_HEADS + h)),
                  vec(D_QK), vec(D_QK), vec(D_QK), vec(D_QK), vec(HEAD)],
        out_specs=pl.BlockSpec((tq, HEAD), lambda b, h, i: (b * nq + i, h)),
        out_shape=jax.ShapeDtypeStruct((batch * seq, GROUP_W), BF16),
        compiler_params=_cparams(("arbitrary", "arbitrary", "arbitrary")),
        name="diff_attn",
    )(qkv, qkv, qkv, kv_ctx, kv_ctx, lq1, lk1, lq2, lk2, norm_w)


def _split3(x):
    hi = x.astype(BF16)
    r1 = x - hi.astype(F32)
    mid = r1.astype(BF16)
    lo = (r1 - mid.astype(F32)).astype(BF16)
    return hi, mid, lo


def _cumsum_rows(tri, x):
    hi, mid, lo = _split3(x)
    return (jnp.dot(tri, hi, preferred_element_type=F32)
            + jnp.dot(tri, mid, preferred_element_type=F32)
            + jnp.dot(tri, lo, preferred_element_type=F32))


def _tri(n, lower):
    r = lax.broadcasted_iota(jnp.int32, (n, n), 0)
    c = lax.broadcasted_iota(jnp.int32, (n, n), 1)
    return jnp.where((r >= c) if lower else (r <= c), 1.0, 0.0).astype(BF16)


def _hgrn_ctx_state(logf, v, forward):
    n = logf.shape[0]
    b = _cumsum_rows(_tri(n, forward), logf)
    b_end = b[n - 1:n] if forward else b[0:1]
    k_hat = ((1.0 - jnp.exp(logf)) * jnp.exp(b_end - b)).astype(BF16)
    return _dot_tn(v, k_hat)


def _hgrn_group(q, logf, v, s_ref, tri, forward):
    L = HG_CHUNK
    n_sub = L // HG_SUB
    G = q.shape[0] // L
    x3 = logf.reshape(G, L, HEAD)
    hi, mid, lo = _split3(x3)
    b = jnp.stack([jnp.dot(tri, hi[c], preferred_element_type=F32)
                   + jnp.dot(tri, mid[c], preferred_element_type=F32)
                   + jnp.dot(tri, lo[c], preferred_element_type=F32) for c in range(G)])
    b_end = b[:, L - 1:L] if forward else b[:, 0:1]
    k = 1.0 - jnp.exp(x3)
    qf = q.reshape(G, L, HEAD).astype(F32)
    q_in = (qf * jnp.exp(b)).astype(BF16)
    k_hat = (k * jnp.exp(b_end - b)).astype(BF16)
    e_end = jnp.exp(b_end)

    row = lax.broadcasted_iota(jnp.int32, (1, L, HEAD), 1)
    refs = [b[:, j * HG_SUB + HG_SUB // 2:j * HG_SUB + HG_SUB // 2 + 1] for j in range(n_sub)]
    ref_own = jnp.concatenate([jnp.broadcast_to(r, (G, HG_SUB, HEAD)) for r in refs], axis=1)
    k_own = k * jnp.exp(ref_own - b)
    qs, ks = [], []
    for j in range(n_sub):
        lo_r, hi_r = j * HG_SUB, (j + 1) * HG_SUB
        q_rows = (row >= lo_r) if forward else (row < hi_r)
        qs.append((qf * jnp.exp(jnp.where(q_rows, b - refs[j], NEG_BIG))).astype(BF16))
        ks.append(jnp.where((row >= lo_r) & (row < hi_r), k_own, 0.0).astype(BF16))
    q_cat = jnp.concatenate(qs, axis=2)
    k_cat = jnp.concatenate(ks, axis=2)
    ar = lax.broadcasted_iota(jnp.int32, (L, L), 0)
    ac = lax.broadcasted_iota(jnp.int32, (L, L), 1)
    causal = (ar >= ac) if forward else (ar <= ac)
    v3 = v.reshape(G, L, HEAD)

    o_intra, ds = [], []
    for c in range(G):
        a = jnp.where(causal, _dot_nt(q_cat[c], k_cat[c]), 0.0)
        o_intra.append(jnp.dot(a.astype(BF16), v3[c], preferred_element_type=F32))
        ds.append(_dot_tn(v3[c], k_hat[c]))

    p = s_ref[...]
    p_before = [None] * G
    for c in (range(G) if forward else reversed(range(G))):
        p_before[c] = p.astype(BF16)
        p = p * e_end[c] + ds[c]
    s_ref[...] = p
    return jnp.concatenate([o_intra[c] + _dot_nt(q_in[c], p_before[c]) for c in range(G)], axis=0)


def _hgrn_kernel(q_ref, lf_ref, lb_ref, v_ref, g_ref, lfc_ref, lbc_ref, vc_ref, nw_ref, o_ref,
                 of_scr, ob_scr, sf_scr, sb_scr):
    seq = q_ref.shape[0]
    rows = min(HG_GROUP * HG_CHUNK, seq)
    ng = seq // rows
    vc = vc_ref[...]
    sf_scr[...] = _hgrn_ctx_state(lfc_ref[...], vc, True)
    sb_scr[...] = _hgrn_ctx_state(lbc_ref[...], vc, False)
    tri_lo = _tri(HG_CHUNK, True)
    tri_up = _tri(HG_CHUNK, False)

    def body(g, carry):
        rf = pl.ds(pl.multiple_of(g * rows, rows), rows)
        rb = pl.ds(pl.multiple_of((ng - 1 - g) * rows, rows), rows)
        of_scr[rf, :] = _hgrn_group(q_ref[rf, :], lf_ref[rf, :], v_ref[rf, :], sf_scr, tri_lo, True)
        ob_scr[rb, :] = _hgrn_group(q_ref[rb, :], lb_ref[rb, :], v_ref[rb, :], sb_scr, tri_up, False)
        return carry

    lax.fori_loop(0, ng, body, 0)

    o = of_scr[...] + ob_scr[...]
    ms = jnp.mean(o * o, axis=1, keepdims=True)
    o_ref[...] = (o * lax.rsqrt(ms + EPS) * nw_ref[...] * g_ref[...].astype(F32)).astype(o_ref.dtype)


def _hgrn(hq, logf, hq_ctx, logf_ctx, norm_w, *, batch, seq, ctx_len):
    nh = N_HEADS
    return pl.pallas_call(
        _hgrn_kernel,
        grid=(batch, nh),
        in_specs=[pl.BlockSpec((seq, HEAD), lambda b, h: (b, h)),
                  pl.BlockSpec((seq, HEAD), lambda b, h: (b, h)),
                  pl.BlockSpec((seq, HEAD), lambda b, h: (b, nh + h)),
                  pl.BlockSpec((seq, HEAD), lambda b, h: (b, nh + h)),
                  pl.BlockSpec((seq, HEAD), lambda b, h: (b, 2 * nh + h)),
                  pl.BlockSpec((ctx_len, HEAD), lambda b, h: (b, h)),
                  pl.BlockSpec((ctx_len, HEAD), lambda b, h: (b, nh + h)),
                  pl.BlockSpec((ctx_len, HEAD), lambda b, h: (b, 2 * nh + h)),
                  pl.BlockSpec((1, HEAD), lambda b, h: (0, 0))],
        out_specs=pl.BlockSpec((seq, HEAD), lambda b, h: (b, h)),
        out_shape=jax.ShapeDtypeStruct((batch * seq, GROUP_W), BF16),
        scratch_shapes=[pltpu.VMEM((seq, HEAD), F32), pltpu.VMEM((seq, HEAD), F32),
                        pltpu.VMEM((HEAD, HEAD), F32), pltpu.VMEM((HEAD, HEAD), F32)],
        compiler_params=_cparams(("arbitrary", "arbitrary")),
        name="hgrn2",
    )(hq, logf, logf, hq, hq, logf_ctx, logf_ctx, hq_ctx, norm_w)


def _layer_norm(z, g, b):
    mu = jnp.mean(z, axis=1, keepdims=True)
    zc = z - mu
    var = jnp.mean(zc * zc, axis=1, keepdims=True)
    return zc * lax.rsqrt(var + EPS) * g + b


def _outproj_kernel(oa_ref, oh_ref, x_ref, w_ref, gt_ref, sh_ref, sc_ref, g_ref, b_ref, h_ref, u_ref):
    y = (jnp.dot(oa_ref[...], w_ref[:GROUP_W, :], preferred_element_type=F32)
         + jnp.dot(oh_ref[...], w_ref[GROUP_W:, :], preferred_element_type=F32))
    h = _layer_norm(ALPHA * x_ref[...] + gt_ref[0] * y, g_ref[...], b_ref[...])
    h_ref[...] = h
    u_ref[...] = (h * (1.0 + sc_ref[0]) + sh_ref[0]).astype(u_ref.dtype)


def _outproj(o_da, o_hg, x2d, w_out, gt1, sh2, sc2, ln_g, ln_b, *, seq, tm):
    m, d = x2d.shape
    tiles_per_seq = seq // tm
    mod = pl.BlockSpec((1, 1, d), lambda i: (i // tiles_per_seq, 0, 0))
    vec = pl.BlockSpec((1, d), lambda i: (0, 0))
    return pl.pallas_call(
        _outproj_kernel,
        grid=(m // tm,),
        in_specs=[pl.BlockSpec((tm, GROUP_W), lambda i: (i, 0)),
                  pl.BlockSpec((tm, GROUP_W), lambda i: (i, 0)),
                  pl.BlockSpec((tm, d), lambda i: (i, 0)),
                  pl.BlockSpec((d, d), lambda i: (0, 0)),
                  mod, mod, mod, vec, vec],
        out_specs=[pl.BlockSpec((tm, d), lambda i: (i, 0)), pl.BlockSpec((tm, d), lambda i: (i, 0))],
        out_shape=[jax.ShapeDtypeStruct((m, d), F32), jax.ShapeDtypeStruct((m, d), BF16)],
        compiler_params=_cparams(("arbitrary",)),
        name="outproj_ln1",
    )(o_da, o_hg, x2d, w_out, gt1, sh2, sc2, ln_g, ln_b)


def _ffn_kernel(u_ref, h_ref, w1_ref, w2_ref, gt_ref, g_ref, b_ref, o_ref, acc_scr):
    j = pl.program_id(1)

    @pl.when(j == 0)
    def _():
        acc_scr[...] = jnp.zeros_like(acc_scr)

    a = jnp.maximum(jnp.dot(u_ref[...], w1_ref[...], preferred_element_type=F32), 0.0)
    acc_scr[...] += jnp.dot((a * a).astype(BF16), w2_ref[...], preferred_element_type=F32)

    @pl.when(j == pl.num_programs(1) - 1)
    def _():
        o_ref[...] = _layer_norm(ALPHA * h_ref[...] + gt_ref[0] * acc_scr[...], g_ref[...], b_ref[...])


def _ffn(u, h, w1, w2, gt2, ln_g, ln_b, *, seq, tm, tf):
    m, d = h.shape
    dff = w1.shape[1]
    tiles_per_seq = seq // tm
    return pl.pallas_call(
        _ffn_kernel,
        grid=(m // tm, dff // tf),
        in_specs=[pl.BlockSpec((tm, d), lambda i, j: (i, 0)),
                  pl.BlockSpec((tm, d), lambda i, j: (i, 0)),
                  pl.BlockSpec((d, tf), lambda i, j: (0, j)),
                  pl.BlockSpec((tf, d), lambda i, j: (j, 0)),
                  pl.BlockSpec((1, 1, d), lambda i, j: (i // tiles_per_seq, 0, 0)),
                  pl.BlockSpec((1, d), lambda i, j: (0, 0)),
                  pl.BlockSpec((1, d), lambda i, j: (0, 0))],
        out_specs=pl.BlockSpec((tm, d), lambda i, j: (i, 0)),
        out_shape=jax.ShapeDtypeStruct((m, d), F32),
        scratch_shapes=[pltpu.VMEM((tm, d), F32)],
        compiler_params=_cparams(("arbitrary", "arbitrary")),
        name="ffn_ln2",
    )(u, h, w1, w2, gt2, ln_g, ln_b)


def kernel(x, c, ctx, c_ctx, w_ada, b_ada, w_in, lam_q1, lam_k1, lam_q2, lam_k2, da_norm_w,
           hg_lb_fwd, hg_lb_bwd, hg_norm_w, w_out, ln1_g, ln1_b, w_ff1, w_ff2, ln2_g, ln2_b):
    batch, seq, d = x.shape
    ctx_len = ctx.shape[1]
    assert d == D_MODEL and w_in.shape[0] == DEPTH and seq % GRID_W == 0
    l = 0

    n_cond = batch + 1
    pad = (-n_cond) % 8
    cond = jnp.concatenate([c, c_ctx[None], jnp.zeros((pad, d), F32)], axis=0)
    mod = _adaln(cond, w_ada[l], b_ada[l][None]).reshape(n_cond + pad, N_ADA, 1, d)
    sh1, sc1, gt1, sh2, sc2, gt2 = [mod[:batch, j] for j in range(N_ADA)]
    csh1, csc1 = mod[batch:batch + 1, 0], mod[batch:batch + 1, 1]

    w_in_b = w_in[l].astype(BF16)
    x2d = x.reshape(batch * seq, d)
    ctx2d = ctx.reshape(batch * ctx_len, d)
    lbs = jnp.stack([hg_lb_fwd.reshape(DEPTH + 1, GROUP_W), hg_lb_bwd.reshape(DEPTH + 1, GROUP_W)])

    tm = min(512, seq)
    tmc = min(512, ctx_len)
    rope_q = _rope_tables(seq, D_QK ** -0.5 * LOG2E)
    rope_k = _rope_tables(seq, 1.0)
    qkv, hq, logf = _inproj(
        x2d, sh1, sc1, w_in_b, (0, 1, 2, 3, 6, 7, 4, 5),
        ("rope", "rope", "copy", "silu", "copy", "silu", "logf", "logf"),
        ((BF16, 3), (BF16, 3), (F32, 2)),
        rows_per_mod=seq, tm=tm, ropes=(rope_q, rope_k), seq=seq, lbs=lbs, name="inproj")
    m_ctx = batch * ctx_len
    kvh_ctx, logf_ctx = _inproj(
        ctx2d, csh1, csc1, w_in_b, (1, 2, 6, 4, 5), ("copy", "copy", "copy", "logf", "logf"),
        ((BF16, 3), (F32, 2)), rows_per_mod=m_ctx, tm=tmc, lbs=lbs, name="inproj_ctx")

    o_da = _attention(qkv, kvh_ctx, lam_q1[l][None], lam_k1[l][None], lam_q2[l][None], lam_k2[l][None],
                      da_norm_w[l][None], batch=batch, seq=seq, ctx_len=ctx_len, tq=min(512, seq))
    o_hg = _hgrn(hq, logf, kvh_ctx, logf_ctx, hg_norm_w[l][None], batch=batch, seq=seq, ctx_len=ctx_len)

    h1, u2 = _outproj(o_da, o_hg, x2d, w_out[l].astype(BF16), gt1, sh2, sc2, ln1_g[l][None], ln1_b[l][None],
                      seq=seq, tm=min(256, seq))
    out = _ffn(u2, h1, w_ff1[l].astype(BF16), w_ff2[l].astype(BF16), gt2, ln2_g[l][None], ln2_b[l][None],
               seq=seq, tm=tm, tf=1024)
    return out.reshape(batch, seq, d)
```

```python
import functools
import math

import jax
import jax.numpy as jnp
from jax import lax
from jax.experimental import pallas as pl
from jax.experimental.pallas import tpu as pltpu

F32 = jnp.float32
BF16 = jnp.bfloat16

D_MODEL = 2048
GRID_W = 64
HEAD = 128
N_HEADS = 8
GROUP_W = N_HEADS * HEAD
N_GROUPS = 8
D_QK = 64
ROT_AXIS = 32
ROPE_BASE = 10000.0
D_FF = 4 * D_MODEL
N_ADA = 6
EPS = 1e-5
DEPTH = 1
ALPHA = (2.0 * DEPTH) ** 0.25
LAM_INIT = 0.8 - 0.6 * math.exp(-0.3 * 0)
HG_CHUNK = 64
HG_SUB = 16
HG_GROUP = 8
ATTN_SUB = 128
ATTN_ONES_ROWS = 16
INPROJ_SUB = 256
LOG2E = 1.4426950408889634
NEG_BIG = -1e30

VMEM_LIMIT = 56 * 1024 * 1024


def _cparams(sem):
    return pltpu.CompilerParams(dimension_semantics=sem, vmem_limit_bytes=VMEM_LIMIT)


def _sigmoid(z):
    return 1.0 / (1.0 + jnp.exp(-z))


def _dot_nt(a, b):
    return lax.dot_general(a, b, (((1,), (1,)), ((), ())), preferred_element_type=F32)


def _dot_tn(a, b):
    return lax.dot_general(a, b, (((0,), (0,)), ((), ())), preferred_element_type=F32)


def _adaln_kernel(cond_ref, w_ref, b_ref, o_ref):
    c = cond_ref[...]
    s = (c * _sigmoid(c)).astype(BF16)
    o_ref[...] = jnp.dot(s, w_ref[...].astype(BF16), preferred_element_type=F32) + b_ref[...]


def _adaln(cond, w, b):
    rows, d = cond.shape
    n = w.shape[1]
    tn = 1024
    return pl.pallas_call(
        _adaln_kernel,
        grid=(n // tn,),
        in_specs=[pl.BlockSpec((rows, d), lambda j: (0, 0)),
                  pl.BlockSpec((d, tn), lambda j: (0, j)),
                  pl.BlockSpec((1, tn), lambda j: (0, j))],
        out_specs=pl.BlockSpec((rows, tn), lambda j: (0, j)),
        out_shape=jax.ShapeDtypeStruct((rows, n), F32),
        compiler_params=_cparams(("arbitrary",)),
        name="adaln",
    )(cond, w, b)


def _rope_tables(seq, scale):
    t = jnp.arange(seq, dtype=jnp.int32)
    row = (t // GRID_W).astype(F32)
    col = (t % GRID_W).astype(F32)
    lane = jnp.arange(HEAD, dtype=jnp.int32)
    inv_freq = ROPE_BASE ** (-jnp.arange(0, ROT_AXIS, 2, dtype=F32) / ROT_AXIS)
    freq = inv_freq[lane % (ROT_AXIS // 2)]
    use_col = ((lane % D_QK) // ROT_AXIS) == 1
    pos = jnp.where(use_col[None, :], col[:, None], row[:, None])
    ang = pos * freq[None, :]
    first_half = (lane % ROT_AXIS) < (ROT_AXIS // 2)
    cos, sin = jnp.cos(ang), jnp.sin(ang)
    t1 = jnp.where(first_half[None, :], -sin, 0.0)
    t2 = jnp.where(first_half[None, :], 0.0, sin)
    return (jnp.stack([cos, t1, t2]) * scale).astype(F32)


def _inproj_kernel(*refs, groups, kinds, places, n_rope, n_out):
    x_ref, sh_ref, sc_ref, w_ref = refs[:4]
    rope_refs = refs[4:4 + n_rope]
    lb_ref = refs[4 + n_rope] if "logf" in kinds else None
    out_refs = refs[-1 - n_out:-1]
    u_scr = refs[-1]
    n_sub = GROUP_W // INPROJ_SUB
    heads_per_sub = INPROJ_SUB // HEAD
    first_logf = kinds.index("logf") if "logf" in kinds else None

    u_scr[...] = (x_ref[...] * (1.0 + sc_ref[0]) + sh_ref[0]).astype(BF16)

    def sub_dot(g, k):
        c0 = groups[g] * GROUP_W + k * INPROJ_SUB
        return jnp.dot(u_scr[...], w_ref[:, c0:c0 + INPROJ_SUB], preferred_element_type=F32)

    def epilogue(g, k, acc, rope_ref):
        kind = kinds[g]
        o_ref = out_refs[places[g][0]]
        c0 = places[g][1] * GROUP_W + k * INPROJ_SUB
        if kind == "rope":
            for h in range(heads_per_sub):
                xh = acc[:, h * HEAD:(h + 1) * HEAD]
                y = (xh * rope_ref[0]
                     + pltpu.roll(xh, HEAD - ROT_AXIS // 2, 1) * rope_ref[1]
                     + pltpu.roll(xh, ROT_AXIS // 2, 1) * rope_ref[2])
                o_ref[:, c0 + h * HEAD:c0 + (h + 1) * HEAD] = y.astype(o_ref.dtype)
        elif kind == "copy":
            o_ref[:, c0:c0 + INPROJ_SUB] = acc.astype(o_ref.dtype)
        elif kind == "silu":
            o_ref[:, c0:c0 + INPROJ_SUB] = (acc * _sigmoid(acc)).astype(o_ref.dtype)
        elif kind == "logf":
            a = lb_ref[g - first_logf][:, k * INPROJ_SUB:(k + 1) * INPROJ_SUB]
            e = jnp.exp(a - jnp.max(a, axis=0, keepdims=True))
            lb = e[0:1] / jnp.sum(e, axis=0, keepdims=True)
            o_ref[:, c0:c0 + INPROJ_SUB] = jnp.log(lb + (1.0 - lb) * _sigmoid(acc)).astype(o_ref.dtype)
        else:
            raise ValueError(kind)

    rope_of, ri = {}, 0
    for g, kind in enumerate(kinds):
        if kind == "rope":
            rope_of[g] = rope_refs[ri]
            ri += 1
    jobs = [(g, k) for g in range(len(kinds)) for k in range(n_sub)]
    nxt = sub_dot(*jobs[0])
    for idx, (g, k) in enumerate(jobs):
        acc = nxt
        if idx + 1 < len(jobs):
            nxt = sub_dot(*jobs[idx + 1])
        epilogue(g, k, acc, rope_of.get(g))


def _inproj(x2d, sh, sc, w, groups, kinds, outs, *, rows_per_mod, tm, ropes=(), seq=None, lbs=None, name):
    m, d = x2d.shape
    ng = len(groups)
    assert m % tm == 0 and rows_per_mod % tm == 0 and sum(n for _, n in outs) == ng
    tiles_per_mod = rows_per_mod // tm
    in_specs = [pl.BlockSpec((tm, d), lambda i: (i, 0)),
                pl.BlockSpec((1, 1, d), lambda i: (i // tiles_per_mod, 0, 0)),
                pl.BlockSpec((1, 1, d), lambda i: (i // tiles_per_mod, 0, 0)),
                pl.BlockSpec(w.shape, lambda i: (0, 0), pipeline_mode=pl.Buffered(1))]
    args = [x2d, sh, sc, w]
    for tab in ropes:
        tiles_per_seq = seq // tm
        in_specs.append(pl.BlockSpec((3, tm, HEAD), lambda i: (0, i % tiles_per_seq, 0)))
        args.append(tab)
    if lbs is not None:
        first_logf = kinds.index("logf")
        assert all(k == "logf" for k in kinds[first_logf:]) and lbs.shape[0] == ng - first_logf
        in_specs.append(pl.BlockSpec(lbs.shape, lambda i: (0, 0, 0)))
        args.append(lbs)
    out_specs, out_shape, places = [], [], []
    for o, (dtype, n) in enumerate(outs):
        out_specs.append(pl.BlockSpec((tm, n * GROUP_W), lambda i: (i, 0)))
        out_shape.append(jax.ShapeDtypeStruct((m, n * GROUP_W), dtype))
        places += [(o, p) for p in range(n)]
    return pl.pallas_call(
        functools.partial(_inproj_kernel, groups=tuple(groups), kinds=tuple(kinds), places=tuple(places),
                          n_rope=len(ropes), n_out=len(outs)),
        grid=(m // tm,),
        in_specs=in_specs,
        out_specs=out_specs,
        out_shape=out_shape,
        scratch_shapes=[pltpu.VMEM((tm, d), BF16)],
        compiler_params=_cparams(("arbitrary",)),
        name=name,
    )(*args)


def _attn_kernel(q_ref, k_ref, v_ref, kc_ref, vc_ref, lq1_ref, lk1_ref, lq2_ref, lk2_ref, nw_ref, o_ref,
                 k_scr, v_scr):
    seq, ctx_len = k_ref.shape[0], kc_ref.shape[0]
    n_keys = seq + ctx_len
    k_split = max(256, (n_keys // 2) // 256 * 256)

    @pl.when(pl.program_id(2) == 0)
    def _():
        k_scr[0:seq, :] = k_ref[...]
        k_scr[seq:n_keys, :] = kc_ref[...]
        v_scr[0:HEAD, 0:seq] = v_ref[...].astype(F32).T.astype(BF16)
        v_scr[0:HEAD, seq:n_keys] = vc_ref[...].astype(F32).T.astype(BF16)
        v_scr[HEAD:, :] = jnp.ones((v_scr.shape[0] - HEAD, n_keys), BF16)

    lam = (jnp.exp(jnp.sum(lq1_ref[...] * lk1_ref[...], axis=1, keepdims=True))
           - jnp.exp(jnp.sum(lq2_ref[...] * lk2_ref[...], axis=1, keepdims=True)) + LAM_INIT)
    ts = min(ATTN_SUB, q_ref.shape[0])
    lane = lax.broadcasted_iota(jnp.int32, (ts, HEAD), 1)
    zero = jnp.zeros((ts, HEAD), BF16)
    n_sub = q_ref.shape[0] // ts

    def scores(r):
        q = q_ref[r * ts:(r + 1) * ts, :]
        q2 = jnp.concatenate([jnp.where(lane < D_QK, q, zero), jnp.where(lane >= D_QK, q, zero)], axis=0)
        sa = _dot_nt(k_scr[0:k_split, :], q2)
        sb = _dot_nt(k_scr[k_split:n_keys, :], q2)
        return sa, sb, jnp.maximum(jnp.max(sa, axis=0, keepdims=True), jnp.max(sb, axis=0, keepdims=True))

    def probs(sa, sb, m):
        return jnp.exp2(sa - m).astype(BF16), jnp.exp2(sb - m).astype(BF16)

    def attend(r, pa, pb):
        ox = (jnp.dot(v_scr[:, 0:k_split], pa, preferred_element_type=F32)
              + jnp.dot(v_scr[:, k_split:n_keys], pb, preferred_element_type=F32))
        on = ox[0:HEAD] / ox[HEAD:HEAD + 1]
        o = (on[:, :ts] - lam * on[:, ts:]).T
        ms = jnp.mean(o * o, axis=1, keepdims=True)
        o_ref[r * ts:(r + 1) * ts, :] = (
            o * lax.rsqrt(ms + EPS) * nw_ref[...] * (1.0 - LAM_INIT)).astype(o_ref.dtype)

    st_a, st_b = {}, {}
    for t in range(n_sub + 2):
        if t < n_sub:
            st_a[t] = scores(t)
        if 0 <= t - 1 < n_sub:
            st_b[t - 1] = probs(*st_a.pop(t - 1))
        if 0 <= t - 2 < n_sub:
            attend(t - 2, *st_b.pop(t - 2))


def _attention(qkv, kv_ctx, lq1, lk1, lq2, lk2, norm_w, *, batch, seq, ctx_len, tq):
    nq = seq // tq
    vec = lambda n: pl.BlockSpec((1, n), lambda b, h, i: (0, 0))
    return pl.pallas_call(
        _attn_kernel,
        grid=(batch, N_HEADS, nq),
        in_specs=[pl.BlockSpec((tq, HEAD), lambda b, h, i: (b * nq + i, h)),
                  pl.BlockSpec((seq, HEAD), lambda b, h, i: (b, N_HEADS + h)),
                  pl.BlockSpec((seq, HEAD), lambda b, h, i: (b, 2 * N_HEADS + h)),
                  pl.BlockSpec((ctx_len, HEAD), lambda b, h, i: (b, h)),
                  pl.BlockSpec((ctx_len, HEAD), lambda b, h, i: (b, N_HEADS + h)),
                  vec(D_QK), vec(D_QK), vec(D_QK), vec(D_QK), vec(HEAD)],
        out_specs=pl.BlockSpec((tq, HEAD), lambda b, h, i: (b * nq + i, h)),
        out_shape=jax.ShapeDtypeStruct((batch * seq, GROUP_W), BF16),
        scratch_shapes=[pltpu.VMEM((seq + ctx_len, HEAD), BF16),
                        pltpu.VMEM((HEAD + ATTN_ONES_ROWS, seq + ctx_len), BF16)],
        compiler_params=_cparams(("arbitrary", "arbitrary", "arbitrary")),
        name="diff_attn",
    )(qkv, qkv, qkv, kv_ctx, kv_ctx, lq1, lk1, lq2, lk2, norm_w)


def _split3(x):
    hi = x.astype(BF16)
    r1 = x - hi.astype(F32)
    mid = r1.astype(BF16)
    lo = (r1 - mid.astype(F32)).astype(BF16)
    return hi, mid, lo


def _cumsum_rows(tri, x):
    hi, mid, lo = _split3(x)
    return (jnp.dot(tri, hi, preferred_element_type=F32)
            + jnp.dot(tri, mid, preferred_element_type=F32)
            + jnp.dot(tri, lo, preferred_element_type=F32))


def _tri(n, lower):
    r = lax.broadcasted_iota(jnp.int32, (n, n), 0)
    c = lax.broadcasted_iota(jnp.int32, (n, n), 1)
    return jnp.where((r >= c) if lower else (r <= c), 1.0, 0.0).astype(BF16)


def _hgrn_ctx_state(logf, v, forward):
    n = logf.shape[0]
    b = _cumsum_rows(_tri(n, forward), logf)
    b_end = b[n - 1:n] if forward else b[0:1]
    k_hat = ((1.0 - jnp.exp(logf)) * jnp.exp(b_end - b)).astype(BF16)
    return _dot_tn(v, k_hat)


def _hgrn_group(q, logf, v, s_ref, tri, forward):
    L = HG_CHUNK
    n_sub = L // HG_SUB
    G = q.shape[0] // L
    x3 = logf.reshape(G, L, HEAD)
    hi, mid, lo = _split3(x3)
    b = jnp.stack([jnp.dot(tri, hi[c], preferred_element_type=F32)
                   + jnp.dot(tri, mid[c], preferred_element_type=F32)
                   + jnp.dot(tri, lo[c], preferred_element_type=F32) for c in range(G)])
    b_end = b[:, L - 1:L] if forward else b[:, 0:1]
    k = 1.0 - jnp.exp(x3)
    qf = q.reshape(G, L, HEAD).astype(F32)
    q_in = (qf * jnp.exp(b)).astype(BF16)
    k_hat = (k * jnp.exp(b_end - b)).astype(BF16)
    e_end = jnp.exp(b_end)

    row = lax.broadcasted_iota(jnp.int32, (1, L, HEAD), 1)
    refs = [b[:, j * HG_SUB + HG_SUB // 2:j * HG_SUB + HG_SUB // 2 + 1] for j in range(n_sub)]
    ref_own = jnp.concatenate([jnp.broadcast_to(r, (G, HG_SUB, HEAD)) for r in refs], axis=1)
    k_own = k * jnp.exp(ref_own - b)
    qs, ks = [], []
    for j in range(n_sub):
        lo_r, hi_r = j * HG_SUB, (j + 1) * HG_SUB
        q_rows = (row >= lo_r) if forward else (row < hi_r)
        qs.append((qf * jnp.exp(jnp.where(q_rows, b - refs[j], NEG_BIG))).astype(BF16))
        ks.append(jnp.where((row >= lo_r) & (row < hi_r), k_own, 0.0).astype(BF16))
    q_cat = jnp.concatenate(qs, axis=2)
    k_cat = jnp.concatenate(ks, axis=2)
    ar = lax.broadcasted_iota(jnp.int32, (L, L), 0)
    ac = lax.broadcasted_iota(jnp.int32, (L, L), 1)
    causal = (ar >= ac) if forward else (ar <= ac)
    v3 = v.reshape(G, L, HEAD)

    o_intra, ds = [], []
    for c in range(G):
        a = jnp.where(causal, _dot_nt(q_cat[c], k_cat[c]), 0.0)
        o_intra.append(jnp.dot(a.astype(BF16), v3[c], preferred_element_type=F32))
        ds.append(_dot_tn(v3[c], k_hat[c]))

    p = s_ref[...]
    p_before = [None] * G
    for c in (range(G) if forward else reversed(range(G))):
        p_before[c] = p.astype(BF16)
        p = p * e_end[c] + ds[c]
    s_ref[...] = p
    return jnp.concatenate([o_intra[c] + _dot_nt(q_in[c], p_before[c]) for c in range(G)], axis=0)


def _hgrn_kernel(q_ref, lf_ref, lb_ref, v_ref, g_ref, lfc_ref, lbc_ref, vc_ref, nw_ref, o_ref,
                 of_scr, ob_scr, sf_scr, sb_scr):
    seq = q_ref.shape[0]
    rows = min(HG_GROUP * HG_CHUNK, seq)
    ng = seq // rows
    vc = vc_ref[...]
    sf_scr[...] = _hgrn_ctx_state(lfc_ref[...], vc, True)
    sb_scr[...] = _hgrn_ctx_state(lbc_ref[...], vc, False)
    tri_lo = _tri(HG_CHUNK, True)
    tri_up = _tri(HG_CHUNK, False)

    def body(g, carry):
        rf = pl.ds(pl.multiple_of(g * rows, rows), rows)
        rb = pl.ds(pl.multiple_of((ng - 1 - g) * rows, rows), rows)
        of_scr[rf, :] = _hgrn_group(q_ref[rf, :], lf_ref[rf, :], v_ref[rf, :], sf_scr, tri_lo, True)
        ob_scr[rb, :] = _hgrn_group(q_ref[rb, :], lb_ref[rb, :], v_ref[rb, :], sb_scr, tri_up, False)
        return carry

    lax.fori_loop(0, ng, body, 0)

    o = of_scr[...] + ob_scr[...]
    ms = jnp.mean(o * o, axis=1, keepdims=True)
    o_ref[...] = (o * lax.rsqrt(ms + EPS) * nw_ref[...] * g_ref[...].astype(F32)).astype(o_ref.dtype)


def _hgrn(hq, logf, hq_ctx, logf_ctx, norm_w, *, batch, seq, ctx_len):
    nh = N_HEADS
    return pl.pallas_call(
        _hgrn_kernel,
        grid=(batch, nh),
        in_specs=[pl.BlockSpec((seq, HEAD), lambda b, h: (b, h)),
                  pl.BlockSpec((seq, HEAD), lambda b, h: (b, h)),
                  pl.BlockSpec((seq, HEAD), lambda b, h: (b, nh + h)),
                  pl.BlockSpec((seq, HEAD), lambda b, h: (b, nh + h)),
                  pl.BlockSpec((seq, HEAD), lambda b, h: (b, 2 * nh + h)),
                  pl.BlockSpec((ctx_len, HEAD), lambda b, h: (b, h)),
                  pl.BlockSpec((ctx_len, HEAD), lambda b, h: (b, nh + h)),
                  pl.BlockSpec((ctx_len, HEAD), lambda b, h: (b, 2 * nh + h)),
                  pl.BlockSpec((1, HEAD), lambda b, h: (0, 0))],
        out_specs=pl.BlockSpec((seq, HEAD), lambda b, h: (b, h)),
        out_shape=jax.ShapeDtypeStruct((batch * seq, GROUP_W), BF16),
        scratch_shapes=[pltpu.VMEM((seq, HEAD), F32), pltpu.VMEM((seq, HEAD), F32),
                        pltpu.VMEM((HEAD, HEAD), F32), pltpu.VMEM((HEAD, HEAD), F32)],
        compiler_params=_cparams(("arbitrary", "arbitrary")),
        name="hgrn2",
    )(hq, logf, logf, hq, hq, logf_ctx, logf_ctx, hq_ctx, norm_w)


def _layer_norm(z, g, b):
    mu = jnp.mean(z, axis=1, keepdims=True)
    zc = z - mu
    var = jnp.mean(zc * zc, axis=1, keepdims=True)
    return zc * lax.rsqrt(var + EPS) * g + b


def _outproj_kernel(oa_ref, oh_ref, x_ref, w_ref, gt_ref, sh_ref, sc_ref, g_ref, b_ref, h_ref, u_ref):
    y = (jnp.dot(oa_ref[...], w_ref[:GROUP_W, :], preferred_element_type=F32)
         + jnp.dot(oh_ref[...], w_ref[GROUP_W:, :], preferred_element_type=F32))
    h = _layer_norm(ALPHA * x_ref[...] + gt_ref[0] * y, g_ref[...], b_ref[...])
    h_ref[...] = h
    u_ref[...] = (h * (1.0 + sc_ref[0]) + sh_ref[0]).astype(u_ref.dtype)


def _outproj(o_da, o_hg, x2d, w_out, gt1, sh2, sc2, ln_g, ln_b, *, seq, tm):
    m, d = x2d.shape
    tiles_per_seq = seq // tm
    mod = pl.BlockSpec((1, 1, d), lambda i: (i // tiles_per_seq, 0, 0))
    vec = pl.BlockSpec((1, d), lambda i: (0, 0))
    return pl.pallas_call(
        _outproj_kernel,
        grid=(m // tm,),
        in_specs=[pl.BlockSpec((tm, GROUP_W), lambda i: (i, 0)),
                  pl.BlockSpec((tm, GROUP_W), lambda i: (i, 0)),
                  pl.BlockSpec((tm, d), lambda i: (i, 0)),
                  pl.BlockSpec((d, d), lambda i: (0, 0), pipeline_mode=pl.Buffered(1)),
                  mod, mod, mod, vec, vec],
        out_specs=[pl.BlockSpec((tm, d), lambda i: (i, 0)), pl.BlockSpec((tm, d), lambda i: (i, 0))],
        out_shape=[jax.ShapeDtypeStruct((m, d), F32), jax.ShapeDtypeStruct((m, d), BF16)],
        compiler_params=_cparams(("arbitrary",)),
        name="outproj_ln1",
    )(o_da, o_hg, x2d, w_out, gt1, sh2, sc2, ln_g, ln_b)


def _ffn_kernel(u_ref, h_ref, w1_ref, w2_ref, gt_ref, g_ref, b_ref, o_ref, acc_scr):
    j = pl.program_id(1)

    @pl.when(j == 0)
    def _():
        acc_scr[...] = jnp.zeros_like(acc_scr)

    a = jnp.maximum(jnp.dot(u_ref[...], w1_ref[...], preferred_element_type=F32), 0.0)
    acc_scr[...] += jnp.dot((a * a).astype(BF16), w2_ref[...], preferred_element_type=F32)

    @pl.when(j == pl.num_programs(1) - 1)
    def _():
        o_ref[...] = _layer_norm(ALPHA * h_ref[...] + gt_ref[0] * acc_scr[...], g_ref[...], b_ref[...])


def _ffn(u, h, w1, w2, gt2, ln_g, ln_b, *, seq, tm, tf):
    m, d = h.shape
    dff = w1.shape[1]
    tiles_per_seq = seq // tm
    return pl.pallas_call(
        _ffn_kernel,
        grid=(m // tm, dff // tf),
        in_specs=[pl.BlockSpec((tm, d), lambda i, j: (i, 0)),
                  pl.BlockSpec((tm, d), lambda i, j: (i, 0)),
                  pl.BlockSpec((d, tf), lambda i, j: (0, j)),
                  pl.BlockSpec((tf, d), lambda i, j: (j, 0)),
                  pl.BlockSpec((1, 1, d), lambda i, j: (i // tiles_per_seq, 0, 0)),
                  pl.BlockSpec((1, d), lambda i, j: (0, 0)),
                  pl.BlockSpec((1, d), lambda i, j: (0, 0))],
        out_specs=pl.BlockSpec((tm, d), lambda i, j: (i, 0)),
        out_shape=jax.ShapeDtypeStruct((m, d), F32),
        scratch_shapes=[pltpu.VMEM((tm, d), F32)],
        compiler_params=_cparams(("arbitrary", "arbitrary")),
        name="ffn_ln2",
    )(u, h, w1, w2, gt2, ln_g, ln_b)


def kernel(x, c, ctx, c_ctx, w_ada, b_ada, w_in, lam_q1, lam_k1, lam_q2, lam_k2, da_norm_w,
           hg_lb_fwd, hg_lb_bwd, hg_norm_w, w_out, ln1_g, ln1_b, w_ff1, w_ff2, ln2_g, ln2_b):
    batch, seq, d = x.shape
    ctx_len = ctx.shape[1]
    assert d == D_MODEL and w_in.shape[0] == DEPTH and seq % GRID_W == 0
    l = 0

    n_cond = batch + 1
    pad = (-n_cond) % 8
    cond = jnp.concatenate([c, c_ctx[None], jnp.zeros((pad, d), F32)], axis=0)
    mod = _adaln(cond, w_ada[l], b_ada[l][None]).reshape(n_cond + pad, N_ADA, 1, d)
    sh1, sc1, gt1, sh2, sc2, gt2 = [mod[:batch, j] for j in range(N_ADA)]
    csh1, csc1 = mod[batch:batch + 1, 0], mod[batch:batch + 1, 1]

    w_in_b = w_in[l].astype(BF16)
    x2d = x.reshape(batch * seq, d)
    ctx2d = ctx.reshape(batch * ctx_len, d)
    lbs = jnp.stack([hg_lb_fwd.reshape(DEPTH + 1, GROUP_W), hg_lb_bwd.reshape(DEPTH + 1, GROUP_W)])

    tm = min(512, seq)
    tmi = min(256, seq)
    tmc = min(256, ctx_len)
    rope_q = _rope_tables(seq, D_QK ** -0.5 * LOG2E)
    rope_k = _rope_tables(seq, 1.0)
    qkv, hq, logf = _inproj(
        x2d, sh1, sc1, w_in_b, (0, 1, 2, 3, 6, 7, 4, 5),
        ("rope", "rope", "copy", "silu", "copy", "silu", "logf", "logf"),
        ((BF16, 3), (BF16, 3), (F32, 2)),
        rows_per_mod=seq, tm=tmi, ropes=(rope_q, rope_k), seq=seq, lbs=lbs, name="inproj")
    m_ctx = batch * ctx_len
    kvh_ctx, logf_ctx = _inproj(
        ctx2d, csh1, csc1, w_in_b, (1, 2, 6, 4, 5), ("copy", "copy", "copy", "logf", "logf"),
        ((BF16, 3), (F32, 2)), rows_per_mod=m_ctx, tm=tmc, lbs=lbs, name="inproj_ctx")

    o_da = _attention(qkv, kvh_ctx, lam_q1[l][None], lam_k1[l][None], lam_q2[l][None], lam_k2[l][None],
                      da_norm_w[l][None], batch=batch, seq=seq, ctx_len=ctx_len, tq=min(1024, seq))
    o_hg = _hgrn(hq, logf, kvh_ctx, logf_ctx, hg_norm_w[l][None], batch=batch, seq=seq, ctx_len=ctx_len)

    h1, u2 = _outproj(o_da, o_hg, x2d, w_out[l].astype(BF16), gt1, sh2, sc2, ln1_g[l][None], ln1_b[l][None],
                      seq=seq, tm=min(256, seq))
    out = _ffn(u2, h1, w_ff1[l].astype(BF16), w_ff2[l].astype(BF16), gt2, ln2_g[l][None], ln2_b[l][None],
               seq=seq, tm=tm, tf=1024)
    return out.reshape(batch, seq, d)
```

```python
import functools
import math

import jax
import jax.numpy as jnp
import numpy as np
from jax import lax
from jax.experimental import pallas as pl
from jax.experimental.pallas import tpu as pltpu

F32 = jnp.float32
BF16 = jnp.bfloat16

D_MODEL = 2048
GRID_W = 64
HEAD = 128
N_HEADS = 8
GROUP_W = N_HEADS * HEAD
N_GROUPS = 8
D_QK = 64
ROT_AXIS = 32
ROPE_BASE = 10000.0
D_FF = 4 * D_MODEL
N_ADA = 6
EPS = 1e-5
DEPTH = 1
ALPHA = (2.0 * DEPTH) ** 0.25
LAM_INIT = 0.8 - 0.6 * math.exp(-0.3 * 0)
HG_CHUNK = 64
HG_SUB = 16
HG_GROUP = 32
ATTN_SUB = 128
ATTN_ONES_ROWS = 16
INPROJ_SUB = 256
LOG2E = 1.4426950408889634

VMEM_LIMIT = 56 * 1024 * 1024


def _cparams(sem):
    return pltpu.CompilerParams(dimension_semantics=sem, vmem_limit_bytes=VMEM_LIMIT)


def _sigmoid(z):
    return 1.0 / (1.0 + jnp.exp(-z))


def _dot_nt(a, b):
    return lax.dot_general(a, b, (((1,), (1,)), ((), ())), preferred_element_type=F32)


def _dot_tn(a, b):
    return lax.dot_general(a, b, (((0,), (0,)), ((), ())), preferred_element_type=F32)


def _adaln_kernel(cond_ref, w_ref, b_ref, o_ref):
    c = cond_ref[...]
    s = (c * _sigmoid(c)).astype(BF16)
    o_ref[...] = jnp.dot(s, w_ref[...].astype(BF16), preferred_element_type=F32) + b_ref[...]


def _adaln(cond, w, b):
    rows, d = cond.shape
    n = w.shape[1]
    tn = 1024
    return pl.pallas_call(
        _adaln_kernel,
        grid=(n // tn,),
        in_specs=[pl.BlockSpec((rows, d), lambda j: (0, 0)),
                  pl.BlockSpec((d, tn), lambda j: (0, j)),
                  pl.BlockSpec((1, tn), lambda j: (0, j))],
        out_specs=pl.BlockSpec((rows, tn), lambda j: (0, j)),
        out_shape=jax.ShapeDtypeStruct((rows, n), F32),
        compiler_params=_cparams(("arbitrary",)),
        name="adaln",
    )(cond, w, b)


def _rope_tables(seq, scale):
    t = np.arange(seq)
    row = (t // GRID_W).astype(np.float64)
    col = (t % GRID_W).astype(np.float64)
    lane = np.arange(HEAD)
    inv_freq = ROPE_BASE ** (-np.arange(0, ROT_AXIS, 2, dtype=np.float64) / ROT_AXIS)
    freq = inv_freq[lane % (ROT_AXIS // 2)]
    use_col = ((lane % D_QK) // ROT_AXIS) == 1
    pos = np.where(use_col[None, :], col[:, None], row[:, None])
    ang = pos * freq[None, :]
    first_half = (lane % ROT_AXIS) < (ROT_AXIS // 2)
    cos, sin = np.cos(ang), np.sin(ang)
    t1 = np.where(first_half[None, :], -sin, 0.0)
    t2 = np.where(first_half[None, :], 0.0, sin)
    return jnp.asarray((np.stack([cos, t1, t2]) * scale).astype(np.float32))


def _inproj_kernel(*refs, groups, kinds, places, n_rope, n_out):
    x_ref, sh_ref, sc_ref, w_ref = refs[:4]
    rope_refs = refs[4:4 + n_rope]
    lb_ref = refs[4 + n_rope] if "logf" in kinds else None
    out_refs = refs[-1 - n_out:-1]
    u_scr = refs[-1]
    n_sub = GROUP_W // INPROJ_SUB
    heads_per_sub = INPROJ_SUB // HEAD
    first_logf = kinds.index("logf") if "logf" in kinds else None

    u_scr[...] = (x_ref[...] * (1.0 + sc_ref[0]) + sh_ref[0]).astype(BF16)

    def sub_dot(g, k):
        c0 = groups[g] * GROUP_W + k * INPROJ_SUB
        return jnp.dot(u_scr[...], w_ref[:, c0:c0 + INPROJ_SUB], preferred_element_type=F32)

    def epilogue(g, k, acc, rope_ref):
        kind = kinds[g]
        o_ref = out_refs[places[g][0]]
        c0 = places[g][1] * GROUP_W + k * INPROJ_SUB
        if kind == "rope":
            for h in range(heads_per_sub):
                xh = acc[:, h * HEAD:(h + 1) * HEAD]
                y = (xh * rope_ref[0]
                     + pltpu.roll(xh, HEAD - ROT_AXIS // 2, 1) * rope_ref[1]
                     + pltpu.roll(xh, ROT_AXIS // 2, 1) * rope_ref[2])
                o_ref[:, c0 + h * HEAD:c0 + (h + 1) * HEAD] = y.astype(o_ref.dtype)
        elif kind == "copy":
            o_ref[:, c0:c0 + INPROJ_SUB] = acc.astype(o_ref.dtype)
        elif kind == "silu":
            o_ref[:, c0:c0 + INPROJ_SUB] = (acc * _sigmoid(acc)).astype(o_ref.dtype)
        elif kind == "logf":
            a = lb_ref[g - first_logf][:, k * INPROJ_SUB:(k + 1) * INPROJ_SUB]
            e = jnp.exp(a - jnp.max(a, axis=0, keepdims=True))
            lb = e[0:1] / jnp.sum(e, axis=0, keepdims=True)
            o_ref[:, c0:c0 + INPROJ_SUB] = jnp.log2(lb + (1.0 - lb) * _sigmoid(acc)).astype(o_ref.dtype)
        else:
            raise ValueError(kind)

    rope_of, ri = {}, 0
    for g, kind in enumerate(kinds):
        if kind == "rope":
            rope_of[g] = rope_refs[ri]
            ri += 1
    jobs = [(g, k) for g in range(len(kinds)) for k in range(n_sub)]
    nxt = sub_dot(*jobs[0])
    for idx, (g, k) in enumerate(jobs):
        acc = nxt
        if idx + 1 < len(jobs):
            nxt = sub_dot(*jobs[idx + 1])
        epilogue(g, k, acc, rope_of.get(g))


def _inproj(x2d, sh, sc, w, groups, kinds, outs, *, rows_per_mod, tm, ropes=(), seq=None, lbs=None, name):
    m, d = x2d.shape
    ng = len(groups)
    assert m % tm == 0 and rows_per_mod % tm == 0 and sum(n for _, n in outs) == ng
    tiles_per_mod = rows_per_mod // tm
    in_specs = [pl.BlockSpec((tm, d), lambda i: (i, 0)),
                pl.BlockSpec((1, 1, d), lambda i: (i // tiles_per_mod, 0, 0)),
                pl.BlockSpec((1, 1, d), lambda i: (i // tiles_per_mod, 0, 0)),
                pl.BlockSpec(w.shape, lambda i: (0, 0), pipeline_mode=pl.Buffered(1))]
    args = [x2d, sh, sc, w]
    for tab in ropes:
        tiles_per_seq = seq // tm
        in_specs.append(pl.BlockSpec((3, tm, HEAD), lambda i: (0, i % tiles_per_seq, 0)))
        args.append(tab)
    if lbs is not None:
        first_logf = kinds.index("logf")
        assert all(k == "logf" for k in kinds[first_logf:]) and lbs.shape[0] == ng - first_logf
        in_specs.append(pl.BlockSpec(lbs.shape, lambda i: (0, 0, 0)))
        args.append(lbs)
    out_specs, out_shape, places = [], [], []
    for o, (dtype, n) in enumerate(outs):
        out_specs.append(pl.BlockSpec((tm, n * GROUP_W), lambda i: (i, 0)))
        out_shape.append(jax.ShapeDtypeStruct((m, n * GROUP_W), dtype))
        places += [(o, p) for p in range(n)]
    return pl.pallas_call(
        functools.partial(_inproj_kernel, groups=tuple(groups), kinds=tuple(kinds), places=tuple(places),
                          n_rope=len(ropes), n_out=len(outs)),
        grid=(m // tm,),
        in_specs=in_specs,
        out_specs=out_specs,
        out_shape=out_shape,
        scratch_shapes=[pltpu.VMEM((tm, d), BF16)],
        compiler_params=_cparams(("arbitrary",)),
        name=name,
    )(*args)


def _attn_kernel(q_ref, k_ref, v_ref, kc_ref, vc_ref, lq1_ref, lk1_ref, lq2_ref, lk2_ref, nw_ref, o_ref,
                 k_scr, v_scr):
    seq, ctx_len = k_ref.shape[0], kc_ref.shape[0]
    n_keys = seq + ctx_len
    k_split = max(256, (n_keys // 2) // 256 * 256)

    @pl.when(pl.program_id(2) == 0)
    def _():
        k_scr[0:seq, :] = k_ref[...]
        k_scr[seq:n_keys, :] = kc_ref[...]
        v_scr[0:HEAD, 0:seq] = v_ref[...].astype(F32).T.astype(BF16)
        v_scr[0:HEAD, seq:n_keys] = vc_ref[...].astype(F32).T.astype(BF16)
        v_scr[HEAD:, :] = jnp.ones((v_scr.shape[0] - HEAD, n_keys), BF16)

    lam = (jnp.exp(jnp.sum(lq1_ref[...] * lk1_ref[...], axis=1, keepdims=True))
           - jnp.exp(jnp.sum(lq2_ref[...] * lk2_ref[...], axis=1, keepdims=True)) + LAM_INIT)
    ts = min(ATTN_SUB, q_ref.shape[0])
    lane = lax.broadcasted_iota(jnp.int32, (ts, HEAD), 1)
    zero = jnp.zeros((ts, HEAD), BF16)
    n_sub = q_ref.shape[0] // ts

    def scores(r):
        q = q_ref[r * ts:(r + 1) * ts, :]
        q2 = jnp.concatenate([jnp.where(lane < D_QK, q, zero), jnp.where(lane >= D_QK, q, zero)], axis=0)
        sa = _dot_nt(k_scr[0:k_split, :], q2)
        sb = _dot_nt(k_scr[k_split:n_keys, :], q2)
        return sa, sb, jnp.maximum(jnp.max(sa, axis=0, keepdims=True), jnp.max(sb, axis=0, keepdims=True))

    def probs(sa, sb, m):
        return jnp.exp2(sa - m).astype(BF16), jnp.exp2(sb - m).astype(BF16)

    def attend(r, pa, pb):
        ox = (jnp.dot(v_scr[:, 0:k_split], pa, preferred_element_type=F32)
              + jnp.dot(v_scr[:, k_split:n_keys], pb, preferred_element_type=F32))
        on = ox[0:HEAD] / ox[HEAD:HEAD + 1]
        o = (on[:, :ts] - lam * on[:, ts:]).T
        ms = jnp.mean(o * o, axis=1, keepdims=True)
        o_ref[r * ts:(r + 1) * ts, :] = (
            o * lax.rsqrt(ms + EPS) * nw_ref[...] * (1.0 - LAM_INIT)).astype(o_ref.dtype)

    st_a, st_b = {}, {}
    for t in range(n_sub + 2):
        if t < n_sub:
            st_a[t] = scores(t)
        if 0 <= t - 1 < n_sub:
            st_b[t - 1] = probs(*st_a.pop(t - 1))
        if 0 <= t - 2 < n_sub:
            attend(t - 2, *st_b.pop(t - 2))


def _attention(qkv, kv_ctx, lq1, lk1, lq2, lk2, norm_w, *, batch, seq, ctx_len, tq):
    nq = seq // tq
    vec = lambda n: pl.BlockSpec((1, n), lambda b, h, i: (0, 0))
    return pl.pallas_call(
        _attn_kernel,
        grid=(batch, N_HEADS, nq),
        in_specs=[pl.BlockSpec((tq, HEAD), lambda b, h, i: (b * nq + i, h)),
                  pl.BlockSpec((seq, HEAD), lambda b, h, i: (b, N_HEADS + h)),
                  pl.BlockSpec((seq, HEAD), lambda b, h, i: (b, 2 * N_HEADS + h)),
                  pl.BlockSpec((ctx_len, HEAD), lambda b, h, i: (b, h)),
                  pl.BlockSpec((ctx_len, HEAD), lambda b, h, i: (b, N_HEADS + h)),
                  vec(D_QK), vec(D_QK), vec(D_QK), vec(D_QK), vec(HEAD)],
        out_specs=pl.BlockSpec((tq, HEAD), lambda b, h, i: (b * nq + i, h)),
        out_shape=jax.ShapeDtypeStruct((batch * seq, GROUP_W), BF16),
        scratch_shapes=[pltpu.VMEM((seq + ctx_len, HEAD), BF16),
                        pltpu.VMEM((HEAD + ATTN_ONES_ROWS, seq + ctx_len), BF16)],
        compiler_params=_cparams(("arbitrary", "arbitrary", "arbitrary")),
        name="diff_attn",
    )(qkv, qkv, qkv, kv_ctx, kv_ctx, lq1, lk1, lq2, lk2, norm_w)


def _split3(x):
    hi = x.astype(BF16)
    r1 = x - hi.astype(F32)
    mid = r1.astype(BF16)
    lo = (r1 - mid.astype(F32)).astype(BF16)
    return hi, mid, lo


def _cumsum_rows(tri, x):
    hi, mid, lo = _split3(x)
    return (jnp.dot(tri, hi, preferred_element_type=F32)
            + jnp.dot(tri, mid, preferred_element_type=F32)
            + jnp.dot(tri, lo, preferred_element_type=F32))


def _tri(n, lower):
    r = lax.broadcasted_iota(jnp.int32, (n, n), 0)
    c = lax.broadcasted_iota(jnp.int32, (n, n), 1)
    return jnp.where((r >= c) if lower else (r <= c), 1.0, 0.0).astype(BF16)


def _hgrn_ctx_state(lg2f, v, forward):
    n = lg2f.shape[0]
    b = _cumsum_rows(_tri(n, forward), lg2f)
    b_end = b[n - 1:n] if forward else b[0:1]
    k_hat = ((1.0 - jnp.exp2(lg2f)) * jnp.exp2(b_end - b)).astype(BF16)
    return _dot_tn(v, k_hat)


def _hgrn_group(q, lg2f, v, s_ref, tri, forward):
    L = HG_CHUNK
    n_sub = L // HG_SUB
    G = q.shape[0] // L
    x3 = lg2f.reshape(G, L, HEAD)
    parts = jnp.concatenate(_split3(x3), axis=2)
    bs = [jnp.dot(tri, parts[c], preferred_element_type=F32) for c in range(G)]
    b = jnp.stack([y[:, 0:HEAD] + y[:, HEAD:2 * HEAD] + y[:, 2 * HEAD:3 * HEAD] for y in bs])
    b_end = b[:, L - 1:L] if forward else b[:, 0:1]
    k = 1.0 - jnp.exp2(x3)
    qf = q.reshape(G, L, HEAD).astype(F32)
    q_in = (qf * jnp.exp2(b)).astype(BF16)
    k_hat = (k * jnp.exp2(b_end - b)).astype(BF16)
    e_end = jnp.exp2(b_end)

    refs = [b[:, j * HG_SUB + HG_SUB // 2:j * HG_SUB + HG_SUB // 2 + 1] for j in range(n_sub)]
    ref_own = jnp.concatenate([jnp.broadcast_to(r, (G, HG_SUB, HEAD)) for r in refs], axis=1)
    k_own = (k * jnp.exp2(ref_own - b)).astype(BF16)

    def zero_rows(n):
        return [jnp.zeros((G, n, HEAD), BF16)] if n else []

    qs, ks = [], []
    for j in range(n_sub):
        lo_r, hi_r = j * HG_SUB, (j + 1) * HG_SUB
        live = slice(lo_r, L) if forward else slice(0, hi_r)
        piece = (qf[:, live] * jnp.exp2(b[:, live] - refs[j])).astype(BF16)
        qs.append(jnp.concatenate(zero_rows(lo_r) + [piece] if forward else [piece] + zero_rows(L - hi_r), axis=1))
        ks.append(jnp.concatenate(zero_rows(lo_r) + [k_own[:, lo_r:hi_r]] + zero_rows(L - hi_r), axis=1))
    q_cat = jnp.concatenate(qs, axis=2)
    k_cat = jnp.concatenate(ks, axis=2)
    ar = lax.broadcasted_iota(jnp.int32, (L, L), 0)
    ac = lax.broadcasted_iota(jnp.int32, (L, L), 1)
    causal = (ar >= ac) if forward else (ar <= ac)
    v3 = v.reshape(G, L, HEAD)

    o_intra, ds = [], []
    for c in range(G):
        a = jnp.where(causal, _dot_nt(q_cat[c], k_cat[c]), 0.0)
        o_intra.append(jnp.dot(a.astype(BF16), v3[c], preferred_element_type=F32))
        ds.append(_dot_tn(v3[c], k_hat[c]))

    p = s_ref[...]
    p_before = [None] * G
    for c in (range(G) if forward else reversed(range(G))):
        p_before[c] = p.astype(BF16)
        p = p * e_end[c] + ds[c]
    s_ref[...] = p
    return jnp.concatenate([o_intra[c] + _dot_nt(q_in[c], p_before[c]) for c in range(G)], axis=0)


def _hgrn_kernel(q_ref, lf_ref, lb_ref, v_ref, g_ref, lfc_ref, lbc_ref, vc_ref, nw_ref, o_ref,
                 of_scr, ob_scr, sf_scr, sb_scr):
    seq = q_ref.shape[0]
    rows = min(HG_GROUP * HG_CHUNK, seq)
    ng = seq // rows
    vc = vc_ref[...]
    sf_scr[...] = _hgrn_ctx_state(lfc_ref[...], vc, True)
    sb_scr[...] = _hgrn_ctx_state(lbc_ref[...], vc, False)
    tri_lo = _tri(HG_CHUNK, True)
    tri_up = _tri(HG_CHUNK, False)

    def body(g, carry):
        rf = pl.ds(pl.multiple_of(g * rows, rows), rows)
        rb = pl.ds(pl.multiple_of((ng - 1 - g) * rows, rows), rows)
        of_scr[rf, :] = _hgrn_group(q_ref[rf, :], lf_ref[rf, :], v_ref[rf, :], sf_scr, tri_lo, True)
        ob_scr[rb, :] = _hgrn_group(q_ref[rb, :], lb_ref[rb, :], v_ref[rb, :], sb_scr, tri_up, False)
        return carry

    lax.fori_loop(0, ng, body, 0)

    o = of_scr[...] + ob_scr[...]
    ms = jnp.mean(o * o, axis=1, keepdims=True)
    o_ref[...] = (o * lax.rsqrt(ms + EPS) * nw_ref[...] * g_ref[...].astype(F32)).astype(o_ref.dtype)


def _hgrn(hq, logf, hq_ctx, logf_ctx, norm_w, *, batch, seq, ctx_len):
    nh = N_HEADS
    return pl.pallas_call(
        _hgrn_kernel,
        grid=(batch, nh),
        in_specs=[pl.BlockSpec((seq, HEAD), lambda b, h: (b, h)),
                  pl.BlockSpec((seq, HEAD), lambda b, h: (b, h)),
                  pl.BlockSpec((seq, HEAD), lambda b, h: (b, nh + h)),
                  pl.BlockSpec((seq, HEAD), lambda b, h: (b, nh + h)),
                  pl.BlockSpec((seq, HEAD), lambda b, h: (b, 2 * nh + h)),
                  pl.BlockSpec((ctx_len, HEAD), lambda b, h: (b, h)),
                  pl.BlockSpec((ctx_len, HEAD), lambda b, h: (b, nh + h)),
                  pl.BlockSpec((ctx_len, HEAD), lambda b, h: (b, 2 * nh + h)),
                  pl.BlockSpec((1, HEAD), lambda b, h: (0, 0))],
        out_specs=pl.BlockSpec((seq, HEAD), lambda b, h: (b, h)),
        out_shape=jax.ShapeDtypeStruct((batch * seq, GROUP_W), BF16),
        scratch_shapes=[pltpu.VMEM((seq, HEAD), F32), pltpu.VMEM((seq, HEAD), F32),
                        pltpu.VMEM((HEAD, HEAD), F32), pltpu.VMEM((HEAD, HEAD), F32)],
        compiler_params=_cparams(("arbitrary", "arbitrary")),
        name="hgrn2",
    )(hq, logf, logf, hq, hq, logf_ctx, logf_ctx, hq_ctx, norm_w)


def _layer_norm(z, g, b):
    mu = jnp.mean(z, axis=1, keepdims=True)
    zc = z - mu
    var = jnp.mean(zc * zc, axis=1, keepdims=True)
    return zc * lax.rsqrt(var + EPS) * g + b


def _outproj_kernel(oa_ref, oh_ref, x_ref, w_ref, gt_ref, sh_ref, sc_ref, g_ref, b_ref, h_ref, u_ref):
    y = (jnp.dot(oa_ref[...], w_ref[:GROUP_W, :], preferred_element_type=F32)
         + jnp.dot(oh_ref[...], w_ref[GROUP_W:, :], preferred_element_type=F32))
    h = _layer_norm(ALPHA * x_ref[...] + gt_ref[0] * y, g_ref[...], b_ref[...])
    h_ref[...] = h
    u_ref[...] = (h * (1.0 + sc_ref[0]) + sh_ref[0]).astype(u_ref.dtype)


def _outproj(o_da, o_hg, x2d, w_out, gt1, sh2, sc2, ln_g, ln_b, *, seq, tm):
    m, d = x2d.shape
    tiles_per_seq = seq // tm
    mod = pl.BlockSpec((1, 1, d), lambda i: (i // tiles_per_seq, 0, 0))
    vec = pl.BlockSpec((1, d), lambda i: (0, 0))
    return pl.pallas_call(
        _outproj_kernel,
        grid=(m // tm,),
        in_specs=[pl.BlockSpec((tm, GROUP_W), lambda i: (i, 0)),
                  pl.BlockSpec((tm, GROUP_W), lambda i: (i, 0)),
                  pl.BlockSpec((tm, d), lambda i: (i, 0)),
                  pl.BlockSpec((d, d), lambda i: (0, 0), pipeline_mode=pl.Buffered(1)),
                  mod, mod, mod, vec, vec],
        out_specs=[pl.BlockSpec((tm, d), lambda i: (i, 0)), pl.BlockSpec((tm, d), lambda i: (i, 0))],
        out_shape=[jax.ShapeDtypeStruct((m, d), F32), jax.ShapeDtypeStruct((m, d), BF16)],
        compiler_params=_cparams(("arbitrary",)),
        name="outproj_ln1",
    )(o_da, o_hg, x2d, w_out, gt1, sh2, sc2, ln_g, ln_b)


def _ffn_kernel(u_ref, h_ref, w1_ref, w2_ref, gt_ref, g_ref, b_ref, o_ref, acc_scr):
    j = pl.program_id(1)

    @pl.when(j == 0)
    def _():
        acc_scr[...] = jnp.zeros_like(acc_scr)

    a = jnp.maximum(jnp.dot(u_ref[...], w1_ref[...], preferred_element_type=F32), 0.0)
    acc_scr[...] += jnp.dot((a * a).astype(BF16), w2_ref[...], preferred_element_type=F32)

    @pl.when(j == pl.num_programs(1) - 1)
    def _():
        o_ref[...] = _layer_norm(ALPHA * h_ref[...] + gt_ref[0] * acc_scr[...], g_ref[...], b_ref[...])


def _ffn(u, h, w1, w2, gt2, ln_g, ln_b, *, seq, tm, tf):
    m, d = h.shape
    dff = w1.shape[1]
    tiles_per_seq = seq // tm
    return pl.pallas_call(
        _ffn_kernel,
        grid=(m // tm, dff // tf),
        in_specs=[pl.BlockSpec((tm, d), lambda i, j: (i, 0)),
                  pl.BlockSpec((tm, d), lambda i, j: (i, 0)),
                  pl.BlockSpec((d, tf), lambda i, j: (0, j)),
                  pl.BlockSpec((tf, d), lambda i, j: (j, 0)),
                  pl.BlockSpec((1, 1, d), lambda i, j: (i // tiles_per_seq, 0, 0)),
                  pl.BlockSpec((1, d), lambda i, j: (0, 0)),
                  pl.BlockSpec((1, d), lambda i, j: (0, 0))],
        out_specs=pl.BlockSpec((tm, d), lambda i, j: (i, 0)),
        out_shape=jax.ShapeDtypeStruct((m, d), F32),
        scratch_shapes=[pltpu.VMEM((tm, d), F32)],
        compiler_params=_cparams(("arbitrary", "arbitrary")),
        name="ffn_ln2",
    )(u, h, w1, w2, gt2, ln_g, ln_b)


def kernel(x, c, ctx, c_ctx, w_ada, b_ada, w_in, lam_q1, lam_k1, lam_q2, lam_k2, da_norm_w,
           hg_lb_fwd, hg_lb_bwd, hg_norm_w, w_out, ln1_g, ln1_b, w_ff1, w_ff2, ln2_g, ln2_b):
    batch, seq, d = x.shape
    ctx_len = ctx.shape[1]
    assert d == D_MODEL and w_in.shape[0] == DEPTH and seq % GRID_W == 0
    l = 0

    n_cond = batch + 1
    pad = (-n_cond) % 8
    cond = jnp.concatenate([c, c_ctx[None], jnp.zeros((pad, d), F32)], axis=0)
    mod = _adaln(cond, w_ada[l], b_ada[l][None]).reshape(n_cond + pad, N_ADA, 1, d)
    sh1, sc1, gt1, sh2, sc2, gt2 = [mod[:batch, j] for j in range(N_ADA)]
    csh1, csc1 = mod[batch:batch + 1, 0], mod[batch:batch + 1, 1]

    w_in_b = w_in[l].astype(BF16)
    x2d = x.reshape(batch * seq, d)
    ctx2d = ctx.reshape(batch * ctx_len, d)
    lbs = jnp.stack([hg_lb_fwd.reshape(DEPTH + 1, GROUP_W), hg_lb_bwd.reshape(DEPTH + 1, GROUP_W)])

    tm = min(512, seq)
    tmi = min(256, seq)
    tmc = min(256, ctx_len)
    rope_q = _rope_tables(seq, D_QK ** -0.5 * LOG2E)
    rope_k = _rope_tables(seq, 1.0)
    qkv, hq, logf = _inproj(
        x2d, sh1, sc1, w_in_b, (0, 1, 2, 3, 6, 7, 4, 5),
        ("rope", "rope", "copy", "silu", "copy", "silu", "logf", "logf"),
        ((BF16, 3), (BF16, 3), (F32, 2)),
        rows_per_mod=seq, tm=tmi, ropes=(rope_q, rope_k), seq=seq, lbs=lbs, name="inproj")
    m_ctx = batch * ctx_len
    kvh_ctx, logf_ctx = _inproj(
        ctx2d, csh1, csc1, w_in_b, (1, 2, 6, 4, 5), ("copy", "copy", "copy", "logf", "logf"),
        ((BF16, 3), (F32, 2)), rows_per_mod=m_ctx, tm=tmc, lbs=lbs, name="inproj_ctx")

    o_da = _attention(qkv, kvh_ctx, lam_q1[l][None], lam_k1[l][None], lam_q2[l][None], lam_k2[l][None],
                      da_norm_w[l][None], batch=batch, seq=seq, ctx_len=ctx_len, tq=min(1024, seq))
    o_hg = _hgrn(hq, logf, kvh_ctx, logf_ctx, hg_norm_w[l][None], batch=batch, seq=seq, ctx_len=ctx_len)

    h1, u2 = _outproj(o_da, o_hg, x2d, w_out[l].astype(BF16), gt1, sh2, sc2, ln1_g[l][None], ln1_b[l][None],
                      seq=seq, tm=min(256, seq))
    out = _ffn(u2, h1, w_ff1[l].astype(BF16), w_ff2[l].astype(BF16), gt2, ln2_g[l][None], ln2_b[l][None],
               seq=seq, tm=tm, tf=1024)
    return out.reshape(batch, seq, d)
```

```python
import functools
import math

import jax
import jax.numpy as jnp
import numpy as np
from jax import lax
from jax.experimental import pallas as pl
from jax.experimental.pallas import tpu as pltpu

F32 = jnp.float32
BF16 = jnp.bfloat16

D_MODEL = 2048
GRID_W = 64
HEAD = 128
N_HEADS = 8
GROUP_W = N_HEADS * HEAD
N_GROUPS = 8
D_QK = 64
ROT_AXIS = 32
ROPE_BASE = 10000.0
D_FF = 4 * D_MODEL
N_ADA = 6
EPS = 1e-5
DEPTH = 1
ALPHA = (2.0 * DEPTH) ** 0.25
LAM_INIT = 0.8 - 0.6 * math.exp(-0.3 * 0)
HG_CHUNK = 64
HG_SUB = 16
HG_GROUP = 32
ATTN_SUB = 128
ATTN_ONES_ROWS = 16
INPROJ_SUB = 256
LOG2E = 1.4426950408889634
BF16_SUBLANES = 16

VMEM_LIMIT = 56 * 1024 * 1024


def _cparams(sem):
    return pltpu.CompilerParams(dimension_semantics=sem, vmem_limit_bytes=VMEM_LIMIT)


def _sigmoid(z):
    return 1.0 / (1.0 + jnp.exp(-z))


def _dot_nt(a, b):
    return lax.dot_general(a, b, (((1,), (1,)), ((), ())), preferred_element_type=F32)


def _dot_tn(a, b):
    return lax.dot_general(a, b, (((0,), (0,)), ((), ())), preferred_element_type=F32)


def _adaln_kernel(cond_ref, w_ref, b_ref, o_ref):
    c = cond_ref[...]
    s = (c * _sigmoid(c)).astype(BF16)
    o_ref[...] = jnp.dot(s, w_ref[...].astype(BF16), preferred_element_type=F32) + b_ref[...]


def _adaln(cond, w, b):
    rows, d = cond.shape
    n = w.shape[1]
    tn = 1024
    return pl.pallas_call(
        _adaln_kernel,
        grid=(n // tn,),
        in_specs=[pl.BlockSpec((rows, d), lambda j: (0, 0)),
                  pl.BlockSpec((d, tn), lambda j: (0, j)),
                  pl.BlockSpec((1, tn), lambda j: (0, j))],
        out_specs=pl.BlockSpec((rows, tn), lambda j: (0, j)),
        out_shape=jax.ShapeDtypeStruct((rows, n), F32),
        compiler_params=_cparams(("arbitrary",)),
        name="adaln",
    )(cond, w, b)


def _rope_tables(seq, scale):
    t = np.arange(seq)
    row = (t // GRID_W).astype(np.float64)
    col = (t % GRID_W).astype(np.float64)
    lane = np.arange(HEAD)
    inv_freq = ROPE_BASE ** (-np.arange(0, ROT_AXIS, 2, dtype=np.float64) / ROT_AXIS)
    freq = inv_freq[lane % (ROT_AXIS // 2)]
    use_col = ((lane % D_QK) // ROT_AXIS) == 1
    pos = np.where(use_col[None, :], col[:, None], row[:, None])
    ang = pos * freq[None, :]
    first_half = (lane % ROT_AXIS) < (ROT_AXIS // 2)
    cos, sin = np.cos(ang), np.sin(ang)
    t1 = np.where(first_half[None, :], -sin, 0.0)
    t2 = np.where(first_half[None, :], 0.0, sin)
    return jnp.asarray((np.stack([cos, t1, t2]) * scale).astype(np.float32))


def _inproj_kernel(*refs, groups, kinds, places, n_rope, n_out, n_cast):
    x_ref, sh_ref, sc_ref, w_ref = refs[:4]
    rope_refs = refs[4:4 + n_rope]
    n_lb = 1 if "logf" in kinds else 0
    lb_ref = refs[4 + n_rope] if n_lb else None
    cast_in = refs[4 + n_rope + n_lb:4 + n_rope + n_lb + n_cast]
    out_refs = refs[-1 - n_out - n_cast:-1 - n_cast]
    cast_out = refs[-1 - n_cast:-1]
    u_scr = refs[-1]
    n_sub = GROUP_W // INPROJ_SUB
    heads_per_sub = INPROJ_SUB // HEAD
    first_logf = kinds.index("logf") if "logf" in kinds else None

    u_scr[...] = (x_ref[...] * (1.0 + sc_ref[0]) + sh_ref[0]).astype(BF16)

    def sub_dot(g, k):
        c0 = groups[g] * GROUP_W + k * INPROJ_SUB
        return jnp.dot(u_scr[...], w_ref[:, c0:c0 + INPROJ_SUB], preferred_element_type=F32)

    def epilogue(g, k, acc, rope_ref):
        kind = kinds[g]
        o_ref = out_refs[places[g][0]]
        c0 = places[g][1] * GROUP_W + k * INPROJ_SUB
        if kind == "rope":
            for h in range(heads_per_sub):
                xh = acc[:, h * HEAD:(h + 1) * HEAD]
                y = (xh * rope_ref[0]
                     + pltpu.roll(xh, HEAD - ROT_AXIS // 2, 1) * rope_ref[1]
                     + pltpu.roll(xh, ROT_AXIS // 2, 1) * rope_ref[2])
                o_ref[:, c0 + h * HEAD:c0 + (h + 1) * HEAD] = y.astype(o_ref.dtype)
        elif kind == "copy":
            o_ref[:, c0:c0 + INPROJ_SUB] = acc.astype(o_ref.dtype)
        elif kind == "silu":
            o_ref[:, c0:c0 + INPROJ_SUB] = (acc * _sigmoid(acc)).astype(o_ref.dtype)
        elif kind == "logf":
            a = lb_ref[g - first_logf][:, k * INPROJ_SUB:(k + 1) * INPROJ_SUB]
            e = jnp.exp(a - jnp.max(a, axis=0, keepdims=True))
            lb = e[0:1] / jnp.sum(e, axis=0, keepdims=True)
            o_ref[:, c0:c0 + INPROJ_SUB] = jnp.log2(lb + (1.0 - lb) * _sigmoid(acc)).astype(o_ref.dtype)
        else:
            raise ValueError(kind)

    rope_of, ri = {}, 0
    for g, kind in enumerate(kinds):
        if kind == "rope":
            rope_of[g] = rope_refs[ri]
            ri += 1
    jobs = [(g, k) for g in range(len(kinds)) for k in range(n_sub)]
    cast_at = {(c * len(jobs)) // n_cast: c for c in range(n_cast)}
    nxt = sub_dot(*jobs[0])
    for idx, (g, k) in enumerate(jobs):
        acc = nxt
        if idx + 1 < len(jobs):
            nxt = sub_dot(*jobs[idx + 1])
        epilogue(g, k, acc, rope_of.get(g))
        if idx in cast_at:
            c = cast_at[idx]
            cast_out[c][...] = cast_in[c][...].astype(BF16)


def _inproj(x2d, sh, sc, w, groups, kinds, outs, *, rows_per_mod, tm, ropes=(), seq=None, lbs=None, casts=(),
            name):
    m, d = x2d.shape
    ng = len(groups)
    assert m % tm == 0 and rows_per_mod % tm == 0 and sum(n for _, n in outs) == ng
    tiles_per_mod = rows_per_mod // tm
    in_specs = [pl.BlockSpec((tm, d), lambda i: (i, 0)),
                pl.BlockSpec((1, 1, d), lambda i: (i // tiles_per_mod, 0, 0)),
                pl.BlockSpec((1, 1, d), lambda i: (i // tiles_per_mod, 0, 0)),
                pl.BlockSpec(w.shape, lambda i: (0, 0), pipeline_mode=pl.Buffered(1))]
    args = [x2d, sh, sc, w]
    for tab in ropes:
        tiles_per_seq = seq // tm
        in_specs.append(pl.BlockSpec((3, tm, HEAD), lambda i: (0, i % tiles_per_seq, 0)))
        args.append(tab)
    if lbs is not None:
        first_logf = kinds.index("logf")
        assert all(k == "logf" for k in kinds[first_logf:]) and lbs.shape[0] == ng - first_logf
        in_specs.append(pl.BlockSpec(lbs.shape, lambda i: (0, 0, 0)))
        args.append(lbs)
    out_specs, out_shape, places = [], [], []
    for o, (dtype, n) in enumerate(outs):
        out_specs.append(pl.BlockSpec((tm, n * GROUP_W), lambda i: (i, 0)))
        out_shape.append(jax.ShapeDtypeStruct((m, n * GROUP_W), dtype))
        places += [(o, p) for p in range(n)]
    steps = m // tm
    for cw in casts:
        slab = cw.shape[0] // steps
        assert cw.shape[0] % steps == 0 and slab % BF16_SUBLANES == 0
        in_specs.append(pl.BlockSpec((slab, cw.shape[1]), lambda i: (i, 0)))
        args.append(cw)
        out_specs.append(pl.BlockSpec((slab, cw.shape[1]), lambda i: (i, 0)))
        out_shape.append(jax.ShapeDtypeStruct(cw.shape, BF16))
    return pl.pallas_call(
        functools.partial(_inproj_kernel, groups=tuple(groups), kinds=tuple(kinds), places=tuple(places),
                          n_rope=len(ropes), n_out=len(outs), n_cast=len(casts)),
        grid=(steps,),
        in_specs=in_specs,
        out_specs=out_specs,
        out_shape=out_shape,
        scratch_shapes=[pltpu.VMEM((tm, d), BF16)],
        compiler_params=_cparams(("arbitrary",)),
        name=name,
    )(*args)


def _attn_kernel(q_ref, k_ref, v_ref, kc_ref, vc_ref, lq1_ref, lk1_ref, lq2_ref, lk2_ref, nw_ref, o_ref,
                 k_scr, v_scr):
    seq, ctx_len = k_ref.shape[0], kc_ref.shape[0]
    n_keys = seq + ctx_len
    k_split = max(256, (n_keys // 2) // 256 * 256)

    @pl.when(pl.program_id(2) == 0)
    def _():
        k_scr[0:seq, :] = k_ref[...]
        k_scr[seq:n_keys, :] = kc_ref[...]
        v_scr[0:HEAD, 0:seq] = v_ref[...].astype(F32).T.astype(BF16)
        v_scr[0:HEAD, seq:n_keys] = vc_ref[...].astype(F32).T.astype(BF16)
        v_scr[HEAD:, :] = jnp.ones((v_scr.shape[0] - HEAD, n_keys), BF16)

    lam = (jnp.exp(jnp.sum(lq1_ref[...] * lk1_ref[...], axis=1, keepdims=True))
           - jnp.exp(jnp.sum(lq2_ref[...] * lk2_ref[...], axis=1, keepdims=True)) + LAM_INIT)
    ts = min(ATTN_SUB, q_ref.shape[0])
    lane = lax.broadcasted_iota(jnp.int32, (ts, HEAD), 1)
    zero = jnp.zeros((ts, HEAD), BF16)
    n_sub = q_ref.shape[0] // ts

    def scores(r):
        q = q_ref[r * ts:(r + 1) * ts, :]
        q2 = jnp.concatenate([jnp.where(lane < D_QK, q, zero), jnp.where(lane >= D_QK, q, zero)], axis=0)
        sa = _dot_nt(k_scr[0:k_split, :], q2)
        sb = _dot_nt(k_scr[k_split:n_keys, :], q2)
        return sa, sb, jnp.maximum(jnp.max(sa, axis=0, keepdims=True), jnp.max(sb, axis=0, keepdims=True))

    def probs(sa, sb, m):
        return jnp.exp2(sa - m).astype(BF16), jnp.exp2(sb - m).astype(BF16)

    def attend(r, pa, pb):
        ox = (jnp.dot(v_scr[:, 0:k_split], pa, preferred_element_type=F32)
              + jnp.dot(v_scr[:, k_split:n_keys], pb, preferred_element_type=F32))
        on = ox[0:HEAD] / ox[HEAD:HEAD + 1]
        o = (on[:, :ts] - lam * on[:, ts:]).T
        ms = jnp.mean(o * o, axis=1, keepdims=True)
        o_ref[r * ts:(r + 1) * ts, :] = (
            o * lax.rsqrt(ms + EPS) * nw_ref[...] * (1.0 - LAM_INIT)).astype(o_ref.dtype)

    st_a, st_b = {}, {}
    for t in range(n_sub + 2):
        if t < n_sub:
            st_a[t] = scores(t)
        if 0 <= t - 1 < n_sub:
            st_b[t - 1] = probs(*st_a.pop(t - 1))
        if 0 <= t - 2 < n_sub:
            attend(t - 2, *st_b.pop(t - 2))


def _attention(qkv, kv_ctx, lq1, lk1, lq2, lk2, norm_w, *, batch, seq, ctx_len, tq):
    nq = seq // tq
    vec = lambda n: pl.BlockSpec((1, n), lambda b, h, i: (0, 0))
    return pl.pallas_call(
        _attn_kernel,
        grid=(batch, N_HEADS, nq),
        in_specs=[pl.BlockSpec((tq, HEAD), lambda b, h, i: (b * nq + i, h)),
                  pl.BlockSpec((seq, HEAD), lambda b, h, i: (b, N_HEADS + h)),
                  pl.BlockSpec((seq, HEAD), lambda b, h, i: (b, 2 * N_HEADS + h)),
                  pl.BlockSpec((ctx_len, HEAD), lambda b, h, i: (b, h)),
                  pl.BlockSpec((ctx_len, HEAD), lambda b, h, i: (b, N_HEADS + h)),
                  vec(D_QK), vec(D_QK), vec(D_QK), vec(D_QK), vec(HEAD)],
        out_specs=pl.BlockSpec((tq, HEAD), lambda b, h, i: (b * nq + i, h)),
        out_shape=jax.ShapeDtypeStruct((batch * seq, GROUP_W), BF16),
        scratch_shapes=[pltpu.VMEM((seq + ctx_len, HEAD), BF16),
                        pltpu.VMEM((HEAD + ATTN_ONES_ROWS, seq + ctx_len), BF16)],
        compiler_params=_cparams(("arbitrary", "arbitrary", "arbitrary")),
        name="diff_attn",
    )(qkv, qkv, qkv, kv_ctx, kv_ctx, lq1, lk1, lq2, lk2, norm_w)


def _split3(x):
    hi = x.astype(BF16)
    r1 = x - hi.astype(F32)
    mid = r1.astype(BF16)
    lo = (r1 - mid.astype(F32)).astype(BF16)
    return hi, mid, lo


def _cumsum_rows(tri, x):
    hi, mid, lo = _split3(x)
    return (jnp.dot(tri, hi, preferred_element_type=F32)
            + jnp.dot(tri, mid, preferred_element_type=F32)
            + jnp.dot(tri, lo, preferred_element_type=F32))


def _tri(n, lower):
    r = lax.broadcasted_iota(jnp.int32, (n, n), 0)
    c = lax.broadcasted_iota(jnp.int32, (n, n), 1)
    return jnp.where((r >= c) if lower else (r <= c), 1.0, 0.0).astype(BF16)


def _hgrn_ctx_state(lg2f, v, forward):
    n = lg2f.shape[0]
    b = _cumsum_rows(_tri(n, forward), lg2f)
    b_end = b[n - 1:n] if forward else b[0:1]
    k_hat = ((1.0 - jnp.exp2(lg2f)) * jnp.exp2(b_end - b)).astype(BF16)
    return _dot_tn(v, k_hat)


def _hgrn_group(q, lg2f, v, s_ref, tri, forward):
    L = HG_CHUNK
    n_sub = L // HG_SUB
    G = q.shape[0] // L
    x3 = lg2f.reshape(G, L, HEAD)
    parts = jnp.concatenate(_split3(x3), axis=2)
    bs = [jnp.dot(tri, parts[c], preferred_element_type=F32) for c in range(G)]
    b = jnp.stack([y[:, 0:HEAD] + y[:, HEAD:2 * HEAD] + y[:, 2 * HEAD:3 * HEAD] for y in bs])
    b_end = b[:, L - 1:L] if forward else b[:, 0:1]
    k = 1.0 - jnp.exp2(x3)
    qf = q.reshape(G, L, HEAD).astype(F32)
    q_in = (qf * jnp.exp2(b)).astype(BF16)
    k_hat = (k * jnp.exp2(b_end - b)).astype(BF16)
    e_end = jnp.exp2(b_end)

    refs = [b[:, j * HG_SUB + HG_SUB // 2:j * HG_SUB + HG_SUB // 2 + 1] for j in range(n_sub)]
    ref_own = jnp.concatenate([jnp.broadcast_to(r, (G, HG_SUB, HEAD)) for r in refs], axis=1)
    k_own = (k * jnp.exp2(ref_own - b)).astype(BF16)

    def zero_rows(n):
        return [jnp.zeros((G, n, HEAD), BF16)] if n else []

    qs, ks = [], []
    for j in range(n_sub):
        lo_r, hi_r = j * HG_SUB, (j + 1) * HG_SUB
        live = slice(lo_r, L) if forward else slice(0, hi_r)
        piece = (qf[:, live] * jnp.exp2(b[:, live] - refs[j])).astype(BF16)
        qs.append(jnp.concatenate(zero_rows(lo_r) + [piece] if forward else [piece] + zero_rows(L - hi_r), axis=1))
        ks.append(jnp.concatenate(zero_rows(lo_r) + [k_own[:, lo_r:hi_r]] + zero_rows(L - hi_r), axis=1))
    q_cat = jnp.concatenate(qs, axis=2)
    k_cat = jnp.concatenate(ks, axis=2)
    ar = lax.broadcasted_iota(jnp.int32, (L, L), 0)
    ac = lax.broadcasted_iota(jnp.int32, (L, L), 1)
    causal = (ar >= ac) if forward else (ar <= ac)
    v3 = v.reshape(G, L, HEAD)

    o_intra, ds = [], []
    for c in range(G):
        a = jnp.where(causal, _dot_nt(q_cat[c], k_cat[c]), 0.0)
        o_intra.append(jnp.dot(a.astype(BF16), v3[c], preferred_element_type=F32))
        ds.append(_dot_tn(v3[c], k_hat[c]))

    p = s_ref[...]
    p_before = [None] * G
    for c in (range(G) if forward else reversed(range(G))):
        p_before[c] = p.astype(BF16)
        p = p * e_end[c] + ds[c]
    s_ref[...] = p
    return jnp.concatenate([o_intra[c] + _dot_nt(q_in[c], p_before[c]) for c in range(G)], axis=0)


def _hgrn_kernel(q_ref, lf_ref, lb_ref, v_ref, g_ref, lfc_ref, lbc_ref, vc_ref, nw_ref, o_ref,
                 of_scr, ob_scr, sf_scr, sb_scr):
    seq = q_ref.shape[0]
    rows = min(HG_GROUP * HG_CHUNK, seq)
    ng = seq // rows
    vc = vc_ref[...]
    sf_scr[...] = _hgrn_ctx_state(lfc_ref[...], vc, True)
    sb_scr[...] = _hgrn_ctx_state(lbc_ref[...], vc, False)
    tri_lo = _tri(HG_CHUNK, True)
    tri_up = _tri(HG_CHUNK, False)

    def body(g, carry):
        rf = pl.ds(pl.multiple_of(g * rows, rows), rows)
        rb = pl.ds(pl.multiple_of((ng - 1 - g) * rows, rows), rows)
        of_scr[rf, :] = _hgrn_group(q_ref[rf, :], lf_ref[rf, :], v_ref[rf, :], sf_scr, tri_lo, True)
        ob_scr[rb, :] = _hgrn_group(q_ref[rb, :], lb_ref[rb, :], v_ref[rb, :], sb_scr, tri_up, False)
        return carry

    lax.fori_loop(0, ng, body, 0)

    o = of_scr[...] + ob_scr[...]
    ms = jnp.mean(o * o, axis=1, keepdims=True)
    o_ref[...] = (o * lax.rsqrt(ms + EPS) * nw_ref[...] * g_ref[...].astype(F32)).astype(o_ref.dtype)


def _hgrn(hq, logf, hq_ctx, logf_ctx, norm_w, *, batch, seq, ctx_len):
    nh = N_HEADS
    return pl.pallas_call(
        _hgrn_kernel,
        grid=(batch, nh),
        in_specs=[pl.BlockSpec((seq, HEAD), lambda b, h: (b, h)),
                  pl.BlockSpec((seq, HEAD), lambda b, h: (b, h)),
                  pl.BlockSpec((seq, HEAD), lambda b, h: (b, nh + h)),
                  pl.BlockSpec((seq, HEAD), lambda b, h: (b, nh + h)),
                  pl.BlockSpec((seq, HEAD), lambda b, h: (b, 2 * nh + h)),
                  pl.BlockSpec((ctx_len, HEAD), lambda b, h: (b, h)),
                  pl.BlockSpec((ctx_len, HEAD), lambda b, h: (b, nh + h)),
                  pl.BlockSpec((ctx_len, HEAD), lambda b, h: (b, 2 * nh + h)),
                  pl.BlockSpec((1, HEAD), lambda b, h: (0, 0))],
        out_specs=pl.BlockSpec((seq, HEAD), lambda b, h: (b, h)),
        out_shape=jax.ShapeDtypeStruct((batch * seq, GROUP_W), BF16),
        scratch_shapes=[pltpu.VMEM((seq, HEAD), F32), pltpu.VMEM((seq, HEAD), F32),
                        pltpu.VMEM((HEAD, HEAD), F32), pltpu.VMEM((HEAD, HEAD), F32)],
        compiler_params=_cparams(("arbitrary", "arbitrary")),
        name="hgrn2",
    )(hq, logf, logf, hq, hq, logf_ctx, logf_ctx, hq_ctx, norm_w)


def _layer_norm(z, g, b):
    mu = jnp.mean(z, axis=1, keepdims=True)
    zc = z - mu
    var = jnp.mean(zc * zc, axis=1, keepdims=True)
    return zc * lax.rsqrt(var + EPS) * g + b


def _outproj_kernel(oa_ref, oh_ref, x_ref, w_ref, gt_ref, sh_ref, sc_ref, g_ref, b_ref, h_ref, u_ref):
    y = (jnp.dot(oa_ref[...], w_ref[:GROUP_W, :], preferred_element_type=F32)
         + jnp.dot(oh_ref[...], w_ref[GROUP_W:, :], preferred_element_type=F32))
    h = _layer_norm(ALPHA * x_ref[...] + gt_ref[0] * y, g_ref[...], b_ref[...])
    h_ref[...] = h
    u_ref[...] = (h * (1.0 + sc_ref[0]) + sh_ref[0]).astype(u_ref.dtype)


def _outproj(o_da, o_hg, x2d, w_out, gt1, sh2, sc2, ln_g, ln_b, *, seq, tm):
    m, d = x2d.shape
    tiles_per_seq = seq // tm
    mod = pl.BlockSpec((1, 1, d), lambda i: (i // tiles_per_seq, 0, 0))
    vec = pl.BlockSpec((1, d), lambda i: (0, 0))
    return pl.pallas_call(
        _outproj_kernel,
        grid=(m // tm,),
        in_specs=[pl.BlockSpec((tm, GROUP_W), lambda i: (i, 0)),
                  pl.BlockSpec((tm, GROUP_W), lambda i: (i, 0)),
                  pl.BlockSpec((tm, d), lambda i: (i, 0)),
                  pl.BlockSpec((d, d), lambda i: (0, 0), pipeline_mode=pl.Buffered(1)),
                  mod, mod, mod, vec, vec],
        out_specs=[pl.BlockSpec((tm, d), lambda i: (i, 0)), pl.BlockSpec((tm, d), lambda i: (i, 0))],
        out_shape=[jax.ShapeDtypeStruct((m, d), F32), jax.ShapeDtypeStruct((m, d), BF16)],
        compiler_params=_cparams(("arbitrary",)),
        name="outproj_ln1",
    )(o_da, o_hg, x2d, w_out, gt1, sh2, sc2, ln_g, ln_b)


def _ffn_kernel(u_ref, h_ref, w1_ref, w2_ref, gt_ref, g_ref, b_ref, o_ref, acc_scr):
    j = pl.program_id(1)

    @pl.when(j == 0)
    def _():
        acc_scr[...] = jnp.zeros_like(acc_scr)

    a = jnp.maximum(jnp.dot(u_ref[...], w1_ref[...], preferred_element_type=F32), 0.0)
    acc_scr[...] += jnp.dot((a * a).astype(BF16), w2_ref[...], preferred_element_type=F32)

    @pl.when(j == pl.num_programs(1) - 1)
    def _():
        o_ref[...] = _layer_norm(ALPHA * h_ref[...] + gt_ref[0] * acc_scr[...], g_ref[...], b_ref[...])


def _ffn(u, h, w1, w2, gt2, ln_g, ln_b, *, seq, tm, tf):
    m, d = h.shape
    dff = w1.shape[1]
    tiles_per_seq = seq // tm
    return pl.pallas_call(
        _ffn_kernel,
        grid=(m // tm, dff // tf),
        in_specs=[pl.BlockSpec((tm, d), lambda i, j: (i, 0)),
                  pl.BlockSpec((tm, d), lambda i, j: (i, 0)),
                  pl.BlockSpec((d, tf), lambda i, j: (0, j)),
                  pl.BlockSpec((tf, d), lambda i, j: (j, 0)),
                  pl.BlockSpec((1, 1, d), lambda i, j: (i // tiles_per_seq, 0, 0)),
                  pl.BlockSpec((1, d), lambda i, j: (0, 0)),
                  pl.BlockSpec((1, d), lambda i, j: (0, 0))],
        out_specs=pl.BlockSpec((tm, d), lambda i, j: (i, 0)),
        out_shape=jax.ShapeDtypeStruct((m, d), F32),
        scratch_shapes=[pltpu.VMEM((tm, d), F32)],
        compiler_params=_cparams(("arbitrary", "arbitrary")),
        name="ffn_ln2",
    )(u, h, w1, w2, gt2, ln_g, ln_b)


def kernel(x, c, ctx, c_ctx, w_ada, b_ada, w_in, lam_q1, lam_k1, lam_q2, lam_k2, da_norm_w,
           hg_lb_fwd, hg_lb_bwd, hg_norm_w, w_out, ln1_g, ln1_b, w_ff1, w_ff2, ln2_g, ln2_b):
    batch, seq, d = x.shape
    ctx_len = ctx.shape[1]
    assert d == D_MODEL and w_in.shape[0] == DEPTH and seq % GRID_W == 0
    l = 0

    n_cond = batch + 1
    pad = (-n_cond) % 8
    cond = jnp.concatenate([c, c_ctx[None], jnp.zeros((pad, d), F32)], axis=0)
    mod = _adaln(cond, w_ada[l], b_ada[l][None]).reshape(n_cond + pad, N_ADA, 1, d)
    sh1, sc1, gt1, sh2, sc2, gt2 = [mod[:batch, j] for j in range(N_ADA)]
    csh1, csc1 = mod[batch:batch + 1, 0], mod[batch:batch + 1, 1]

    w_in_b = w_in[l].astype(BF16)
    x2d = x.reshape(batch * seq, d)
    ctx2d = ctx.reshape(batch * ctx_len, d)
    lbs = jnp.stack([hg_lb_fwd.reshape(DEPTH + 1, GROUP_W), hg_lb_bwd.reshape(DEPTH + 1, GROUP_W)])

    tm = min(512, seq)
    tmi = min(256, seq)
    tmc = min(256, ctx_len)
    rope_q = _rope_tables(seq, D_QK ** -0.5 * LOG2E)
    rope_k = _rope_tables(seq, 1.0)
    qkv, hq, logf, w_out_b, w_ff1_b, w_ff2_b = _inproj(
        x2d, sh1, sc1, w_in_b, (0, 1, 2, 3, 6, 7, 4, 5),
        ("rope", "rope", "copy", "silu", "copy", "silu", "logf", "logf"),
        ((BF16, 3), (BF16, 3), (F32, 2)),
        rows_per_mod=seq, tm=tmi, ropes=(rope_q, rope_k), seq=seq, lbs=lbs,
        casts=(w_out[l], w_ff1[l], w_ff2[l]), name="inproj")
    m_ctx = batch * ctx_len
    kvh_ctx, logf_ctx = _inproj(
        ctx2d, csh1, csc1, w_in_b, (1, 2, 6, 4, 5), ("copy", "copy", "copy", "logf", "logf"),
        ((BF16, 3), (F32, 2)), rows_per_mod=m_ctx, tm=tmc, lbs=lbs, name="inproj_ctx")

    o_da = _attention(qkv, kvh_ctx, lam_q1[l][None], lam_k1[l][None], lam_q2[l][None], lam_k2[l][None],
                      da_norm_w[l][None], batch=batch, seq=seq, ctx_len=ctx_len, tq=min(1024, seq))
    o_hg = _hgrn(hq, logf, kvh_ctx, logf_ctx, hg_norm_w[l][None], batch=batch, seq=seq, ctx_len=ctx_len)

    h1, u2 = _outproj(o_da, o_hg, x2d, w_out_b, gt1, sh2, sc2, ln1_g[l][None], ln1_b[l][None],
                      seq=seq, tm=min(256, seq))
    out = _ffn(u2, h1, w_ff1_b, w_ff2_b, gt2, ln2_g[l][None], ln2_b[l][None], seq=seq, tm=tm, tf=1024)
    return out.reshape(batch, seq, d)
```

```python
import functools
import math

import jax
import jax.numpy as jnp
import numpy as np
from jax import lax
from jax.experimental import pallas as pl
from jax.experimental.pallas import tpu as pltpu

F32 = jnp.float32
BF16 = jnp.bfloat16

D_MODEL = 2048
GRID_W = 64
HEAD = 128
N_HEADS = 8
GROUP_W = N_HEADS * HEAD
N_GROUPS = 8
D_QK = 64
ROT_AXIS = 32
ROPE_BASE = 10000.0
D_FF = 4 * D_MODEL
N_ADA = 6
EPS = 1e-5
DEPTH = 1
ALPHA = (2.0 * DEPTH) ** 0.25
LAM_INIT = 0.8 - 0.6 * math.exp(-0.3 * 0)
HG_CHUNK = 64
HG_SUB = 16
HG_GROUP = 8
HG_HEADS_PER_STEP = 2
HG_HEAD_LAG = 7
ATTN_SUB = 128
ATTN_ONES_ROWS = 16
INPROJ_SUB = 256
LOG2E = 1.4426950408889634
BF16_SUBLANES = 16

VMEM_LIMIT = 56 * 1024 * 1024


def _cparams(sem):
    return pltpu.CompilerParams(dimension_semantics=sem, vmem_limit_bytes=VMEM_LIMIT)


def _sigmoid(z):
    return 1.0 / (1.0 + jnp.exp(-z))


def _dot_nt(a, b):
    return lax.dot_general(a, b, (((1,), (1,)), ((), ())), preferred_element_type=F32)


def _dot_tn(a, b):
    return lax.dot_general(a, b, (((0,), (0,)), ((), ())), preferred_element_type=F32)


def _adaln_kernel(cond_ref, w_ref, b_ref, o_ref):
    c = cond_ref[...]
    s = (c * _sigmoid(c)).astype(BF16)
    o_ref[...] = jnp.dot(s, w_ref[...].astype(BF16), preferred_element_type=F32) + b_ref[...]


def _adaln(cond, w, b):
    rows, d = cond.shape
    n = w.shape[1]
    tn = 1024
    return pl.pallas_call(
        _adaln_kernel,
        grid=(n // tn,),
        in_specs=[pl.BlockSpec((rows, d), lambda j: (0, 0)),
                  pl.BlockSpec((d, tn), lambda j: (0, j)),
                  pl.BlockSpec((1, tn), lambda j: (0, j))],
        out_specs=pl.BlockSpec((rows, tn), lambda j: (0, j)),
        out_shape=jax.ShapeDtypeStruct((rows, n), F32),
        compiler_params=_cparams(("arbitrary",)),
        name="adaln",
    )(cond, w, b)


def _rope_tables(seq, scale):
    t = np.arange(seq)
    row = (t // GRID_W).astype(np.float64)
    col = (t % GRID_W).astype(np.float64)
    lane = np.arange(HEAD)
    inv_freq = ROPE_BASE ** (-np.arange(0, ROT_AXIS, 2, dtype=np.float64) / ROT_AXIS)
    freq = inv_freq[lane % (ROT_AXIS // 2)]
    use_col = ((lane % D_QK) // ROT_AXIS) == 1
    pos = np.where(use_col[None, :], col[:, None], row[:, None])
    ang = pos * freq[None, :]
    first_half = (lane % ROT_AXIS) < (ROT_AXIS // 2)
    cos, sin = np.cos(ang), np.sin(ang)
    t1 = np.where(first_half[None, :], -sin, 0.0)
    t2 = np.where(first_half[None, :], 0.0, sin)
    return jnp.asarray((np.stack([cos, t1, t2]) * scale).astype(np.float32))


def _inproj_kernel(*refs, groups, kinds, places, n_rope, n_out, n_cast):
    x_ref, sh_ref, sc_ref, w_ref = refs[:4]
    rope_refs = refs[4:4 + n_rope]
    n_lb = 1 if "logf" in kinds else 0
    lb_ref = refs[4 + n_rope] if n_lb else None
    cast_in = refs[4 + n_rope + n_lb:4 + n_rope + n_lb + n_cast]
    out_refs = refs[-1 - n_out - n_cast:-1 - n_cast]
    cast_out = refs[-1 - n_cast:-1]
    u_scr = refs[-1]
    n_sub = GROUP_W // INPROJ_SUB
    heads_per_sub = INPROJ_SUB // HEAD
    first_logf = kinds.index("logf") if "logf" in kinds else None

    u_scr[...] = (x_ref[...] * (1.0 + sc_ref[0]) + sh_ref[0]).astype(BF16)

    def sub_dot(g, k):
        c0 = groups[g] * GROUP_W + k * INPROJ_SUB
        return jnp.dot(u_scr[...], w_ref[:, c0:c0 + INPROJ_SUB], preferred_element_type=F32)

    def epilogue(g, k, acc, rope_ref):
        kind = kinds[g]
        o_ref = out_refs[places[g][0]]
        c0 = places[g][1] * GROUP_W + k * INPROJ_SUB
        if kind == "rope":
            for h in range(heads_per_sub):
                xh = acc[:, h * HEAD:(h + 1) * HEAD]
                y = (xh * rope_ref[0]
                     + pltpu.roll(xh, HEAD - ROT_AXIS // 2, 1) * rope_ref[1]
                     + pltpu.roll(xh, ROT_AXIS // 2, 1) * rope_ref[2])
                o_ref[:, c0 + h * HEAD:c0 + (h + 1) * HEAD] = y.astype(o_ref.dtype)
        elif kind == "copy":
            o_ref[:, c0:c0 + INPROJ_SUB] = acc.astype(o_ref.dtype)
        elif kind == "silu":
            o_ref[:, c0:c0 + INPROJ_SUB] = (acc * _sigmoid(acc)).astype(o_ref.dtype)
        elif kind == "logf":
            a = lb_ref[g - first_logf][:, k * INPROJ_SUB:(k + 1) * INPROJ_SUB]
            e = jnp.exp(a - jnp.max(a, axis=0, keepdims=True))
            lb = e[0:1] / jnp.sum(e, axis=0, keepdims=True)
            o_ref[:, c0:c0 + INPROJ_SUB] = jnp.log2(lb + (1.0 - lb) * _sigmoid(acc)).astype(o_ref.dtype)
        else:
            raise ValueError(kind)

    rope_of, ri = {}, 0
    for g, kind in enumerate(kinds):
        if kind == "rope":
            rope_of[g] = rope_refs[ri]
            ri += 1
    jobs = [(g, k) for g in range(len(kinds)) for k in range(n_sub)]
    cast_at = {(c * len(jobs)) // n_cast: c for c in range(n_cast)}
    nxt = sub_dot(*jobs[0])
    for idx, (g, k) in enumerate(jobs):
        acc = nxt
        if idx + 1 < len(jobs):
            nxt = sub_dot(*jobs[idx + 1])
        epilogue(g, k, acc, rope_of.get(g))
        if idx in cast_at:
            c = cast_at[idx]
            cast_out[c][...] = cast_in[c][...].astype(BF16)


def _inproj(x2d, sh, sc, w, groups, kinds, outs, *, rows_per_mod, tm, ropes=(), seq=None, lbs=None, casts=(),
            name):
    m, d = x2d.shape
    ng = len(groups)
    assert m % tm == 0 and rows_per_mod % tm == 0 and sum(n for _, n in outs) == ng
    tiles_per_mod = rows_per_mod // tm
    in_specs = [pl.BlockSpec((tm, d), lambda i: (i, 0)),
                pl.BlockSpec((1, 1, d), lambda i: (i // tiles_per_mod, 0, 0)),
                pl.BlockSpec((1, 1, d), lambda i: (i // tiles_per_mod, 0, 0)),
                pl.BlockSpec(w.shape, lambda i: (0, 0), pipeline_mode=pl.Buffered(1))]
    args = [x2d, sh, sc, w]
    for tab in ropes:
        tiles_per_seq = seq // tm
        in_specs.append(pl.BlockSpec((3, tm, HEAD), lambda i: (0, i % tiles_per_seq, 0)))
        args.append(tab)
    if lbs is not None:
        first_logf = kinds.index("logf")
        assert all(k == "logf" for k in kinds[first_logf:]) and lbs.shape[0] == ng - first_logf
        in_specs.append(pl.BlockSpec(lbs.shape, lambda i: (0, 0, 0)))
        args.append(lbs)
    out_specs, out_shape, places = [], [], []
    for o, (dtype, n) in enumerate(outs):
        out_specs.append(pl.BlockSpec((tm, n * GROUP_W), lambda i: (i, 0)))
        out_shape.append(jax.ShapeDtypeStruct((m, n * GROUP_W), dtype))
        places += [(o, p) for p in range(n)]
    steps = m // tm
    for cw in casts:
        slab = cw.shape[0] // steps
        assert cw.shape[0] % steps == 0 and slab % BF16_SUBLANES == 0
        in_specs.append(pl.BlockSpec((slab, cw.shape[1]), lambda i: (i, 0)))
        args.append(cw)
        out_specs.append(pl.BlockSpec((slab, cw.shape[1]), lambda i: (i, 0)))
        out_shape.append(jax.ShapeDtypeStruct(cw.shape, BF16))
    return pl.pallas_call(
        functools.partial(_inproj_kernel, groups=tuple(groups), kinds=tuple(kinds), places=tuple(places),
                          n_rope=len(ropes), n_out=len(outs), n_cast=len(casts)),
        grid=(steps,),
        in_specs=in_specs,
        out_specs=out_specs,
        out_shape=out_shape,
        scratch_shapes=[pltpu.VMEM((tm, d), BF16)],
        compiler_params=_cparams(("arbitrary",)),
        name=name,
    )(*args)


def _attn_kernel(q_ref, k_ref, v_ref, kc_ref, vc_ref, lq1_ref, lk1_ref, lq2_ref, lk2_ref, nw_ref, o_ref,
                 k_scr, v_scr):
    seq, ctx_len = k_ref.shape[0], kc_ref.shape[0]
    n_keys = seq + ctx_len
    k_split = max(256, (n_keys // 2) // 256 * 256)

    @pl.when(pl.program_id(2) == 0)
    def _():
        k_scr[0:seq, :] = k_ref[...]
        k_scr[seq:n_keys, :] = kc_ref[...]
        v_scr[0:HEAD, 0:seq] = v_ref[...].astype(F32).T.astype(BF16)
        v_scr[0:HEAD, seq:n_keys] = vc_ref[...].astype(F32).T.astype(BF16)
        v_scr[HEAD:, :] = jnp.ones((v_scr.shape[0] - HEAD, n_keys), BF16)

    lam = (jnp.exp(jnp.sum(lq1_ref[...] * lk1_ref[...], axis=1, keepdims=True))
           - jnp.exp(jnp.sum(lq2_ref[...] * lk2_ref[...], axis=1, keepdims=True)) + LAM_INIT)
    ts = min(ATTN_SUB, q_ref.shape[0])
    lane = lax.broadcasted_iota(jnp.int32, (ts, HEAD), 1)
    zero = jnp.zeros((ts, HEAD), BF16)
    n_sub = q_ref.shape[0] // ts

    def scores(r):
        q = q_ref[r * ts:(r + 1) * ts, :]
        q2 = jnp.concatenate([jnp.where(lane < D_QK, q, zero), jnp.where(lane >= D_QK, q, zero)], axis=0)
        sa = _dot_nt(k_scr[0:k_split, :], q2)
        sb = _dot_nt(k_scr[k_split:n_keys, :], q2)
        return sa, sb, jnp.maximum(jnp.max(sa, axis=0, keepdims=True), jnp.max(sb, axis=0, keepdims=True))

    def probs(sa, sb, m):
        return jnp.exp2(sa - m).astype(BF16), jnp.exp2(sb - m).astype(BF16)

    def attend(r, pa, pb):
        ox = (jnp.dot(v_scr[:, 0:k_split], pa, preferred_element_type=F32)
              + jnp.dot(v_scr[:, k_split:n_keys], pb, preferred_element_type=F32))
        on = ox[0:HEAD] / ox[HEAD:HEAD + 1]
        o = (on[:, :ts] - lam * on[:, ts:]).T
        ms = jnp.mean(o * o, axis=1, keepdims=True)
        o_ref[r * ts:(r + 1) * ts, :] = (
            o * lax.rsqrt(ms + EPS) * nw_ref[...] * (1.0 - LAM_INIT)).astype(o_ref.dtype)

    st_a, st_b = {}, {}
    for t in range(n_sub + 2):
        if t < n_sub:
            st_a[t] = scores(t)
        if 0 <= t - 1 < n_sub:
            st_b[t - 1] = probs(*st_a.pop(t - 1))
        if 0 <= t - 2 < n_sub:
            attend(t - 2, *st_b.pop(t - 2))


def _attention(qkv, kv_ctx, lq1, lk1, lq2, lk2, norm_w, *, batch, seq, ctx_len, tq):
    nq = seq // tq
    vec = lambda n: pl.BlockSpec((1, n), lambda b, h, i: (0, 0))
    return pl.pallas_call(
        _attn_kernel,
        grid=(batch, N_HEADS, nq),
        in_specs=[pl.BlockSpec((tq, HEAD), lambda b, h, i: (b * nq + i, h)),
                  pl.BlockSpec((seq, HEAD), lambda b, h, i: (b, N_HEADS + h)),
                  pl.BlockSpec((seq, HEAD), lambda b, h, i: (b, 2 * N_HEADS + h)),
                  pl.BlockSpec((ctx_len, HEAD), lambda b, h, i: (b, h)),
                  pl.BlockSpec((ctx_len, HEAD), lambda b, h, i: (b, N_HEADS + h)),
                  vec(D_QK), vec(D_QK), vec(D_QK), vec(D_QK), vec(HEAD)],
        out_specs=pl.BlockSpec((tq, HEAD), lambda b, h, i: (b * nq + i, h)),
        out_shape=jax.ShapeDtypeStruct((batch * seq, GROUP_W), BF16),
        scratch_shapes=[pltpu.VMEM((seq + ctx_len, HEAD), BF16),
                        pltpu.VMEM((HEAD + ATTN_ONES_ROWS, seq + ctx_len), BF16)],
        compiler_params=_cparams(("arbitrary", "arbitrary", "arbitrary")),
        name="diff_attn",
    )(qkv, qkv, qkv, kv_ctx, kv_ctx, lq1, lk1, lq2, lk2, norm_w)


def _split3(x):
    hi = x.astype(BF16)
    r1 = x - hi.astype(F32)
    mid = r1.astype(BF16)
    lo = (r1 - mid.astype(F32)).astype(BF16)
    return hi, mid, lo


def _cumsum_rows(tri, x):
    hi, mid, lo = _split3(x)
    return (jnp.dot(tri, hi, preferred_element_type=F32)
            + jnp.dot(tri, mid, preferred_element_type=F32)
            + jnp.dot(tri, lo, preferred_element_type=F32))


def _tri(n, lower):
    r = lax.broadcasted_iota(jnp.int32, (n, n), 0)
    c = lax.broadcasted_iota(jnp.int32, (n, n), 1)
    return jnp.where((r >= c) if lower else (r <= c), 1.0, 0.0).astype(BF16)


def _hgrn_ctx_state(lg2f, v, forward):
    n = lg2f.shape[0]
    b = _cumsum_rows(_tri(n, forward), lg2f)
    b_end = b[n - 1:n] if forward else b[0:1]
    k_hat = ((1.0 - jnp.exp2(lg2f)) * jnp.exp2(b_end - b)).astype(BF16)
    return _dot_tn(v, k_hat)


def _hgrn_stream(q_ref, lg_ref, v_ref, rows, state, tri, forward, out_ref):
    L = HG_CHUNK
    n_sub = L // HG_SUB
    G = (rows.stop - rows.start) // L
    x3 = lg_ref[rows, :].reshape(G, L, HEAD)
    parts = jnp.concatenate(_split3(x3), axis=2)
    bs = [jnp.dot(tri, parts[c], preferred_element_type=F32) for c in range(G)]
    yield
    b = jnp.stack([y[:, 0:HEAD] + y[:, HEAD:2 * HEAD] + y[:, 2 * HEAD:3 * HEAD] for y in bs])
    b_end = b[:, L - 1:L] if forward else b[:, 0:1]
    k = 1.0 - jnp.exp2(x3)
    qf = q_ref[rows, :].reshape(G, L, HEAD).astype(F32)
    q_in = (qf * jnp.exp2(b)).astype(BF16)
    k_hat = (k * jnp.exp2(b_end - b)).astype(BF16)
    e_end = jnp.exp2(b_end)

    refs = [b[:, j * HG_SUB + HG_SUB // 2:j * HG_SUB + HG_SUB // 2 + 1] for j in range(n_sub)]
    ref_own = jnp.concatenate([jnp.broadcast_to(r, (G, HG_SUB, HEAD)) for r in refs], axis=1)
    k_own = (k * jnp.exp2(ref_own - b)).astype(BF16)

    def zero_rows(n):
        return [jnp.zeros((G, n, HEAD), BF16)] if n else []

    qs, ks = [], []
    for j in range(n_sub):
        lo_r, hi_r = j * HG_SUB, (j + 1) * HG_SUB
        live = slice(lo_r, L) if forward else slice(0, hi_r)
        piece = (qf[:, live] * jnp.exp2(b[:, live] - refs[j])).astype(BF16)
        qs.append(jnp.concatenate(zero_rows(lo_r) + [piece] if forward else [piece] + zero_rows(L - hi_r), axis=1))
        ks.append(jnp.concatenate(zero_rows(lo_r) + [k_own[:, lo_r:hi_r]] + zero_rows(L - hi_r), axis=1))
    q_cat = jnp.concatenate(qs, axis=2)
    k_cat = jnp.concatenate(ks, axis=2)
    v3 = v_ref[rows, :].reshape(G, L, HEAD)
    yield
    a_raw = [_dot_nt(q_cat[c], k_cat[c]) for c in range(G)]
    ds = [_dot_tn(v3[c], k_hat[c]) for c in range(G)]
    yield
    ar = lax.broadcasted_iota(jnp.int32, (L, L), 0)
    ac = lax.broadcasted_iota(jnp.int32, (L, L), 1)
    causal = (ar >= ac) if forward else (ar <= ac)
    o_intra = [jnp.dot(jnp.where(causal, a_raw[c], 0.0).astype(BF16), v3[c], preferred_element_type=F32)
               for c in range(G)]
    yield
    p = state[forward]
    p_before = [None] * G
    for c in (range(G) if forward else reversed(range(G))):
        p_before[c] = p.astype(BF16)
        p = p * e_end[c] + ds[c]
    state[forward] = p
    yield
    out_ref[rows, :] = jnp.concatenate([o_intra[c] + _dot_nt(q_in[c], p_before[c]) for c in range(G)], axis=0)


def _advance(gens):
    for gen in list(gens):
        if next(gen, "done") == "done":
            gens.remove(gen)


def _hgrn_head(q_ref, lf_ref, lb_ref, v_ref, g_ref, lfc_ref, lbc_ref, vc_ref, nw_ref, o_ref, of_scr, ob_scr,
               lanes, tri_lo, tri_up):
    seq = q_ref.shape[0]
    rows = min(HG_GROUP * HG_CHUNK, seq)
    ng = seq // rows
    vc = vc_ref[:, lanes]
    state = {True: _hgrn_ctx_state(lfc_ref[:, lanes], vc, True),
             False: _hgrn_ctx_state(lbc_ref[:, lanes], vc, False)}
    yield
    q_h, lf_h, lb_h, v_h = q_ref.at[:, lanes], lf_ref.at[:, lanes], lb_ref.at[:, lanes], v_ref.at[:, lanes]
    of_h, ob_h = of_scr.at[:, lanes], ob_scr.at[:, lanes]
    streams = []
    for g in range(ng):
        rf = slice(g * rows, (g + 1) * rows)
        rb = slice((ng - 1 - g) * rows, (ng - g) * rows)
        streams.append(_hgrn_stream(q_h, lf_h, v_h, rf, state, tri_lo, True, of_h))
        streams.append(_hgrn_stream(q_h, lb_h, v_h, rb, state, tri_up, False, ob_h))
    live = []
    while streams or live:
        if streams:
            live.append(streams.pop(0))
        _advance(live)
        yield
    o = of_h[...] + ob_h[...]
    ms = jnp.mean(o * o, axis=1, keepdims=True)
    o_ref[:, lanes] = (o * lax.rsqrt(ms + EPS) * nw_ref[...] * g_ref[:, lanes].astype(F32)).astype(o_ref.dtype)


def _hgrn_kernel(q_ref, lf_ref, lb_ref, v_ref, g_ref, lfc_ref, lbc_ref, vc_ref, nw_ref, o_ref, of_scr, ob_scr):
    tri_lo = _tri(HG_CHUNK, True)
    tri_up = _tri(HG_CHUNK, False)
    n_heads = q_ref.shape[1] // HEAD
    heads = [_hgrn_head(q_ref, lf_ref, lb_ref, v_ref, g_ref, lfc_ref, lbc_ref, vc_ref, nw_ref, o_ref,
                        of_scr, ob_scr, slice(h * HEAD, (h + 1) * HEAD), tri_lo, tri_up) for h in range(n_heads)]
    live, tick = [], 0
    while heads or live:
        if heads and tick % HG_HEAD_LAG == 0:
            live.append(heads.pop(0))
        _advance(live)
        tick += 1


def _hgrn(hq, logf, hq_ctx, logf_ctx, norm_w, *, batch, seq, ctx_len):
    w = HG_HEADS_PER_STEP * HEAD
    nb = GROUP_W // w
    return pl.pallas_call(
        _hgrn_kernel,
        grid=(batch, nb),
        in_specs=[pl.BlockSpec((seq, w), lambda b, h: (b, h)),
                  pl.BlockSpec((seq, w), lambda b, h: (b, h)),
                  pl.BlockSpec((seq, w), lambda b, h: (b, nb + h)),
                  pl.BlockSpec((seq, w), lambda b, h: (b, nb + h)),
                  pl.BlockSpec((seq, w), lambda b, h: (b, 2 * nb + h)),
                  pl.BlockSpec((ctx_len, w), lambda b, h: (b, h)),
                  pl.BlockSpec((ctx_len, w), lambda b, h: (b, nb + h)),
                  pl.BlockSpec((ctx_len, w), lambda b, h: (b, 2 * nb + h)),
                  pl.BlockSpec((1, HEAD), lambda b, h: (0, 0))],
        out_specs=pl.BlockSpec((seq, w), lambda b, h: (b, h)),
        out_shape=jax.ShapeDtypeStruct((batch * seq, GROUP_W), BF16),
        scratch_shapes=[pltpu.VMEM((seq, w), F32), pltpu.VMEM((seq, w), F32)],
        compiler_params=_cparams(("arbitrary", "arbitrary")),
        name="hgrn2",
    )(hq, logf, logf, hq, hq, logf_ctx, logf_ctx, hq_ctx, norm_w)


def _layer_norm(z, g, b):
    mu = jnp.mean(z, axis=1, keepdims=True)
    zc = z - mu
    var = jnp.mean(zc * zc, axis=1, keepdims=True)
    return zc * lax.rsqrt(var + EPS) * g + b


def _outproj_kernel(oa_ref, oh_ref, x_ref, w_ref, gt_ref, sh_ref, sc_ref, g_ref, b_ref, h_ref, u_ref):
    y = (jnp.dot(oa_ref[...], w_ref[:GROUP_W, :], preferred_element_type=F32)
         + jnp.dot(oh_ref[...], w_ref[GROUP_W:, :], preferred_element_type=F32))
    h = _layer_norm(ALPHA * x_ref[...] + gt_ref[0] * y, g_ref[...], b_ref[...])
    h_ref[...] = h
    u_ref[...] = (h * (1.0 + sc_ref[0]) + sh_ref[0]).astype(u_ref.dtype)


def _outproj(o_da, o_hg, x2d, w_out, gt1, sh2, sc2, ln_g, ln_b, *, seq, tm):
    m, d = x2d.shape
    tiles_per_seq = seq // tm
    mod = pl.BlockSpec((1, 1, d), lambda i: (i // tiles_per_seq, 0, 0))
    vec = pl.BlockSpec((1, d), lambda i: (0, 0))
    return pl.pallas_call(
        _outproj_kernel,
        grid=(m // tm,),
        in_specs=[pl.BlockSpec((tm, GROUP_W), lambda i: (i, 0)),
                  pl.BlockSpec((tm, GROUP_W), lambda i: (i, 0)),
                  pl.BlockSpec((tm, d), lambda i: (i, 0)),
                  pl.BlockSpec((d, d), lambda i: (0, 0), pipeline_mode=pl.Buffered(1)),
                  mod, mod, mod, vec, vec],
        out_specs=[pl.BlockSpec((tm, d), lambda i: (i, 0)), pl.BlockSpec((tm, d), lambda i: (i, 0))],
        out_shape=[jax.ShapeDtypeStruct((m, d), F32), jax.ShapeDtypeStruct((m, d), BF16)],
        compiler_params=_cparams(("arbitrary",)),
        name="outproj_ln1",
    )(o_da, o_hg, x2d, w_out, gt1, sh2, sc2, ln_g, ln_b)


def _ffn_kernel(u_ref, h_ref, w1_ref, w2_ref, gt_ref, g_ref, b_ref, o_ref, acc_scr):
    j = pl.program_id(1)

    @pl.when(j == 0)
    def _():
        acc_scr[...] = jnp.zeros_like(acc_scr)

    a = jnp.maximum(jnp.dot(u_ref[...], w1_ref[...], preferred_element_type=F32), 0.0)
    acc_scr[...] += jnp.dot((a * a).astype(BF16), w2_ref[...], preferred_element_type=F32)

    @pl.when(j == pl.num_programs(1) - 1)
    def _():
        o_ref[...] = _layer_norm(ALPHA * h_ref[...] + gt_ref[0] * acc_scr[...], g_ref[...], b_ref[...])


def _ffn(u, h, w1, w2, gt2, ln_g, ln_b, *, seq, tm, tf):
    m, d = h.shape
    dff = w1.shape[1]
    tiles_per_seq = seq // tm
    return pl.pallas_call(
        _ffn_kernel,
        grid=(m // tm, dff // tf),
        in_specs=[pl.BlockSpec((tm, d), lambda i, j: (i, 0)),
                  pl.BlockSpec((tm, d), lambda i, j: (i, 0)),
                  pl.BlockSpec((d, tf), lambda i, j: (0, j)),
                  pl.BlockSpec((tf, d), lambda i, j: (j, 0)),
                  pl.BlockSpec((1, 1, d), lambda i, j: (i // tiles_per_seq, 0, 0)),
                  pl.BlockSpec((1, d), lambda i, j: (0, 0)),
                  pl.BlockSpec((1, d), lambda i, j: (0, 0))],
        out_specs=pl.BlockSpec((tm, d), lambda i, j: (i, 0)),
        out_shape=jax.ShapeDtypeStruct((m, d), F32),
        scratch_shapes=[pltpu.VMEM((tm, d), F32)],
        compiler_params=_cparams(("arbitrary", "arbitrary")),
        name="ffn_ln2",
    )(u, h, w1, w2, gt2, ln_g, ln_b)


def kernel(x, c, ctx, c_ctx, w_ada, b_ada, w_in, lam_q1, lam_k1, lam_q2, lam_k2, da_norm_w,
           hg_lb_fwd, hg_lb_bwd, hg_norm_w, w_out, ln1_g, ln1_b, w_ff1, w_ff2, ln2_g, ln2_b):
    batch, seq, d = x.shape
    ctx_len = ctx.shape[1]
    assert d == D_MODEL and w_in.shape[0] == DEPTH and seq % GRID_W == 0
    l = 0

    n_cond = batch + 1
    pad = (-n_cond) % 8
    cond = jnp.concatenate([c, c_ctx[None], jnp.zeros((pad, d), F32)], axis=0)
    mod = _adaln(cond, w_ada[l], b_ada[l][None]).reshape(n_cond + pad, N_ADA, 1, d)
    sh1, sc1, gt1, sh2, sc2, gt2 = [mod[:batch, j] for j in range(N_ADA)]
    csh1, csc1 = mod[batch:batch + 1, 0], mod[batch:batch + 1, 1]

    w_in_b = w_in[l].astype(BF16)
    x2d = x.reshape(batch * seq, d)
    ctx2d = ctx.reshape(batch * ctx_len, d)
    lbs = jnp.stack([hg_lb_fwd.reshape(DEPTH + 1, GROUP_W), hg_lb_bwd.reshape(DEPTH + 1, GROUP_W)])

    tm = min(512, seq)
    tmi = min(256, seq)
    tmc = min(256, ctx_len)
    rope_q = _rope_tables(seq, D_QK ** -0.5 * LOG2E)
    rope_k = _rope_tables(seq, 1.0)
    qkv, hq, logf, w_out_b, w_ff1_b, w_ff2_b = _inproj(
        x2d, sh1, sc1, w_in_b, (0, 1, 2, 3, 6, 7, 4, 5),
        ("rope", "rope", "copy", "silu", "copy", "silu", "logf", "logf"),
        ((BF16, 3), (BF16, 3), (F32, 2)),
        rows_per_mod=seq, tm=tmi, ropes=(rope_q, rope_k), seq=seq, lbs=lbs,
        casts=(w_out[l], w_ff1[l], w_ff2[l]), name="inproj")
    m_ctx = batch * ctx_len
    kvh_ctx, logf_ctx = _inproj(
        ctx2d, csh1, csc1, w_in_b, (1, 2, 6, 4, 5), ("copy", "copy", "copy", "logf", "logf"),
        ((BF16, 3), (F32, 2)), rows_per_mod=m_ctx, tm=tmc, lbs=lbs, name="inproj_ctx")

    o_da = _attention(qkv, kvh_ctx, lam_q1[l][None], lam_k1[l][None], lam_q2[l][None], lam_k2[l][None],
                      da_norm_w[l][None], batch=batch, seq=seq, ctx_len=ctx_len, tq=min(2048, seq))
    o_hg = _hgrn(hq, logf, kvh_ctx, logf_ctx, hg_norm_w[l][None], batch=batch, seq=seq, ctx_len=ctx_len)

    h1, u2 = _outproj(o_da, o_hg, x2d, w_out_b, gt1, sh2, sc2, ln1_g[l][None], ln1_b[l][None],
                      seq=seq, tm=min(256, seq))
    out = _ffn(u2, h1, w_ff1_b, w_ff2_b, gt2, ln2_g[l][None], ln2_b[l][None], seq=seq, tm=tm, tf=1024)
    return out.reshape(batch, seq, d)
```

```python
import functools
import math

import jax
import jax.numpy as jnp
import numpy as np
from jax import lax
from jax.experimental import pallas as pl
from jax.experimental.pallas import tpu as pltpu

F32 = jnp.float32
BF16 = jnp.bfloat16

D_MODEL = 2048
GRID_W = 64
HEAD = 128
N_HEADS = 8
GROUP_W = N_HEADS * HEAD
N_GROUPS = 8
D_QK = 64
ROT_AXIS = 32
ROPE_BASE = 10000.0
D_FF = 4 * D_MODEL
N_ADA = 6
EPS = 1e-5
DEPTH = 1
ALPHA = (2.0 * DEPTH) ** 0.25
LAM_INIT = 0.8 - 0.6 * math.exp(-0.3 * 0)
HG_CHUNK = 64
HG_SUB = 16
HG_GROUP = 8
HG_HEADS_PER_STEP = 2
HG_HEAD_LAG = 7
ATTN_SUB = 128
ATTN_ONES_ROWS = 16
FFN_NORM_SUB = 256
INPROJ_SUB = 256
LOG2E = 1.4426950408889634
BF16_SUBLANES = 16

VMEM_LIMIT = 56 * 1024 * 1024


def _cparams(sem):
    return pltpu.CompilerParams(dimension_semantics=sem, vmem_limit_bytes=VMEM_LIMIT)


def _sigmoid(z):
    return 1.0 / (1.0 + jnp.exp(-z))


def _dot_nt(a, b):
    return lax.dot_general(a, b, (((1,), (1,)), ((), ())), preferred_element_type=F32)


def _dot_tn(a, b):
    return lax.dot_general(a, b, (((0,), (0,)), ((), ())), preferred_element_type=F32)


def _adaln_kernel(cond_ref, w_ref, b_ref, o_ref):
    c = cond_ref[...]
    s = (c * _sigmoid(c)).astype(BF16)
    o_ref[...] = jnp.dot(s, w_ref[...].astype(BF16), preferred_element_type=F32) + b_ref[...]


def _adaln(cond, w, b):
    rows, d = cond.shape
    n = w.shape[1]
    tn = 1024
    return pl.pallas_call(
        _adaln_kernel,
        grid=(n // tn,),
        in_specs=[pl.BlockSpec((rows, d), lambda j: (0, 0)),
                  pl.BlockSpec((d, tn), lambda j: (0, j)),
                  pl.BlockSpec((1, tn), lambda j: (0, j))],
        out_specs=pl.BlockSpec((rows, tn), lambda j: (0, j)),
        out_shape=jax.ShapeDtypeStruct((rows, n), F32),
        compiler_params=_cparams(("arbitrary",)),
        name="adaln",
    )(cond, w, b)


def _rope_tables(seq, scale):
    t = np.arange(seq)
    row = (t // GRID_W).astype(np.float64)
    col = (t % GRID_W).astype(np.float64)
    lane = np.arange(HEAD)
    inv_freq = ROPE_BASE ** (-np.arange(0, ROT_AXIS, 2, dtype=np.float64) / ROT_AXIS)
    freq = inv_freq[lane % (ROT_AXIS // 2)]
    use_col = ((lane % D_QK) // ROT_AXIS) == 1
    pos = np.where(use_col[None, :], col[:, None], row[:, None])
    ang = pos * freq[None, :]
    first_half = (lane % ROT_AXIS) < (ROT_AXIS // 2)
    cos, sin = np.cos(ang), np.sin(ang)
    t1 = np.where(first_half[None, :], -sin, 0.0)
    t2 = np.where(first_half[None, :], 0.0, sin)
    return jnp.asarray((np.stack([cos, t1, t2]) * scale).astype(np.float32))


def _inproj_kernel(*refs, groups, kinds, places, n_rope, n_out, n_cast):
    x_ref, sh_ref, sc_ref, w_ref = refs[:4]
    rope_refs = refs[4:4 + n_rope]
    n_lb = 1 if "logf" in kinds else 0
    lb_ref = refs[4 + n_rope] if n_lb else None
    cast_in = refs[4 + n_rope + n_lb:4 + n_rope + n_lb + n_cast]
    out_refs = refs[-1 - n_out - n_cast:-1 - n_cast]
    cast_out = refs[-1 - n_cast:-1]
    u_scr = refs[-1]
    n_sub = GROUP_W // INPROJ_SUB
    heads_per_sub = INPROJ_SUB // HEAD
    first_logf = kinds.index("logf") if "logf" in kinds else None

    u_scr[...] = (x_ref[...] * (1.0 + sc_ref[0]) + sh_ref[0]).astype(BF16)

    def sub_dot(g, k):
        c0 = groups[g] * GROUP_W + k * INPROJ_SUB
        return jnp.dot(u_scr[...], w_ref[:, c0:c0 + INPROJ_SUB], preferred_element_type=F32)

    def epilogue(g, k, acc, rope_ref):
        kind = kinds[g]
        o_ref = out_refs[places[g][0]]
        c0 = places[g][1] * GROUP_W + k * INPROJ_SUB
        if kind == "rope":
            for h in range(heads_per_sub):
                xh = acc[:, h * HEAD:(h + 1) * HEAD]
                y = (xh * rope_ref[0]
                     + pltpu.roll(xh, HEAD - ROT_AXIS // 2, 1) * rope_ref[1]
                     + pltpu.roll(xh, ROT_AXIS // 2, 1) * rope_ref[2])
                o_ref[:, c0 + h * HEAD:c0 + (h + 1) * HEAD] = y.astype(o_ref.dtype)
        elif kind == "copy":
            o_ref[:, c0:c0 + INPROJ_SUB] = acc.astype(o_ref.dtype)
        elif kind == "silu":
            o_ref[:, c0:c0 + INPROJ_SUB] = (acc * _sigmoid(acc)).astype(o_ref.dtype)
        elif kind == "logf":
            a = lb_ref[g - first_logf][:, k * INPROJ_SUB:(k + 1) * INPROJ_SUB]
            e = jnp.exp(a - jnp.max(a, axis=0, keepdims=True))
            lb = e[0:1] / jnp.sum(e, axis=0, keepdims=True)
            o_ref[:, c0:c0 + INPROJ_SUB] = jnp.log2(lb + (1.0 - lb) * _sigmoid(acc)).astype(o_ref.dtype)
        else:
            raise ValueError(kind)

    rope_of, ri = {}, 0
    for g, kind in enumerate(kinds):
        if kind == "rope":
            rope_of[g] = rope_refs[ri]
            ri += 1
    jobs = [(g, k) for g in range(len(kinds)) for k in range(n_sub)]
    cast_at = {(c * len(jobs)) // n_cast: c for c in range(n_cast)}
    nxt = sub_dot(*jobs[0])
    for idx, (g, k) in enumerate(jobs):
        acc = nxt
        if idx + 1 < len(jobs):
            nxt = sub_dot(*jobs[idx + 1])
        epilogue(g, k, acc, rope_of.get(g))
        if idx in cast_at:
            c = cast_at[idx]
            cast_out[c][...] = cast_in[c][...].astype(BF16)


def _inproj(x2d, sh, sc, w, groups, kinds, outs, *, rows_per_mod, tm, ropes=(), seq=None, lbs=None, casts=(),
            name):
    m, d = x2d.shape
    ng = len(groups)
    assert m % tm == 0 and rows_per_mod % tm == 0 and sum(n for _, n in outs) == ng
    tiles_per_mod = rows_per_mod // tm
    in_specs = [pl.BlockSpec((tm, d), lambda i: (i, 0)),
                pl.BlockSpec((1, 1, d), lambda i: (i // tiles_per_mod, 0, 0)),
                pl.BlockSpec((1, 1, d), lambda i: (i // tiles_per_mod, 0, 0)),
                pl.BlockSpec(w.shape, lambda i: (0, 0), pipeline_mode=pl.Buffered(1))]
    args = [x2d, sh, sc, w]
    for tab in ropes:
        tiles_per_seq = seq // tm
        in_specs.append(pl.BlockSpec((3, tm, HEAD), lambda i: (0, i % tiles_per_seq, 0)))
        args.append(tab)
    if lbs is not None:
        first_logf = kinds.index("logf")
        assert all(k == "logf" for k in kinds[first_logf:]) and lbs.shape[0] == ng - first_logf
        in_specs.append(pl.BlockSpec(lbs.shape, lambda i: (0, 0, 0)))
        args.append(lbs)
    out_specs, out_shape, places = [], [], []
    for o, (dtype, n) in enumerate(outs):
        out_specs.append(pl.BlockSpec((tm, n * GROUP_W), lambda i: (i, 0)))
        out_shape.append(jax.ShapeDtypeStruct((m, n * GROUP_W), dtype))
        places += [(o, p) for p in range(n)]
    steps = m // tm
    for cw in casts:
        slab = cw.shape[0] // steps
        assert cw.shape[0] % steps == 0 and slab % BF16_SUBLANES == 0
        in_specs.append(pl.BlockSpec((slab, cw.shape[1]), lambda i: (i, 0)))
        args.append(cw)
        out_specs.append(pl.BlockSpec((slab, cw.shape[1]), lambda i: (i, 0)))
        out_shape.append(jax.ShapeDtypeStruct(cw.shape, BF16))
    return pl.pallas_call(
        functools.partial(_inproj_kernel, groups=tuple(groups), kinds=tuple(kinds), places=tuple(places),
                          n_rope=len(ropes), n_out=len(outs), n_cast=len(casts)),
        grid=(steps,),
        in_specs=in_specs,
        out_specs=out_specs,
        out_shape=out_shape,
        scratch_shapes=[pltpu.VMEM((tm, d), BF16)],
        compiler_params=_cparams(("arbitrary",)),
        name=name,
    )(*args)


def _attn_kernel(q_ref, k_ref, v_ref, kc_ref, vc_ref, lq1_ref, lk1_ref, lq2_ref, lk2_ref, nw_ref, o_ref,
                 k_scr, v_scr):
    seq, ctx_len = k_ref.shape[0], kc_ref.shape[0]
    n_keys = seq + ctx_len
    k_split = max(256, (n_keys // 2) // 256 * 256)

    @pl.when(pl.program_id(2) == 0)
    def _():
        k_scr[0:seq, :] = k_ref[...]
        k_scr[seq:n_keys, :] = kc_ref[...]
        v_scr[0:HEAD, 0:seq] = v_ref[...].astype(F32).T.astype(BF16)
        v_scr[0:HEAD, seq:n_keys] = vc_ref[...].astype(F32).T.astype(BF16)
        v_scr[HEAD:, :] = jnp.ones((v_scr.shape[0] - HEAD, n_keys), BF16)

    lam = (jnp.exp(jnp.sum(lq1_ref[...] * lk1_ref[...], axis=1, keepdims=True))
           - jnp.exp(jnp.sum(lq2_ref[...] * lk2_ref[...], axis=1, keepdims=True)) + LAM_INIT)
    ts = min(ATTN_SUB, q_ref.shape[0])
    lane = lax.broadcasted_iota(jnp.int32, (ts, HEAD), 1)
    zero = jnp.zeros((ts, HEAD), BF16)
    n_sub = q_ref.shape[0] // ts

    def scores(r):
        q = q_ref[r * ts:(r + 1) * ts, :]
        q2 = jnp.concatenate([jnp.where(lane < D_QK, q, zero), jnp.where(lane >= D_QK, q, zero)], axis=0)
        sa = _dot_nt(k_scr[0:k_split, :], q2)
        sb = _dot_nt(k_scr[k_split:n_keys, :], q2)
        return sa, sb, jnp.maximum(jnp.max(sa, axis=0, keepdims=True), jnp.max(sb, axis=0, keepdims=True))

    def probs(sa, sb, m):
        return jnp.exp2(sa - m).astype(BF16), jnp.exp2(sb - m).astype(BF16)

    def attend(r, pa, pb):
        ox = (jnp.dot(v_scr[:, 0:k_split], pa, preferred_element_type=F32)
              + jnp.dot(v_scr[:, k_split:n_keys], pb, preferred_element_type=F32))
        on = ox[0:HEAD] / ox[HEAD:HEAD + 1]
        o = (on[:, :ts] - lam * on[:, ts:]).T
        ms = jnp.mean(o * o, axis=1, keepdims=True)
        o_ref[r * ts:(r + 1) * ts, :] = (
            o * lax.rsqrt(ms + EPS) * nw_ref[...] * (1.0 - LAM_INIT)).astype(o_ref.dtype)

    st_a, st_b = {}, {}
    for t in range(n_sub + 2):
        if t < n_sub:
            st_a[t] = scores(t)
        if 0 <= t - 1 < n_sub:
            st_b[t - 1] = probs(*st_a.pop(t - 1))
        if 0 <= t - 2 < n_sub:
            attend(t - 2, *st_b.pop(t - 2))


def _attention(qkv, kv_ctx, lq1, lk1, lq2, lk2, norm_w, *, batch, seq, ctx_len, tq):
    nq = seq // tq
    vec = lambda n: pl.BlockSpec((1, n), lambda b, h, i: (0, 0))
    return pl.pallas_call(
        _attn_kernel,
        grid=(batch, N_HEADS, nq),
        in_specs=[pl.BlockSpec((tq, HEAD), lambda b, h, i: (b * nq + i, h)),
                  pl.BlockSpec((seq, HEAD), lambda b, h, i: (b, N_HEADS + h)),
                  pl.BlockSpec((seq, HEAD), lambda b, h, i: (b, 2 * N_HEADS + h)),
                  pl.BlockSpec((ctx_len, HEAD), lambda b, h, i: (b, h)),
                  pl.BlockSpec((ctx_len, HEAD), lambda b, h, i: (b, N_HEADS + h)),
                  vec(D_QK), vec(D_QK), vec(D_QK), vec(D_QK), vec(HEAD)],
        out_specs=pl.BlockSpec((tq, HEAD), lambda b, h, i: (b * nq + i, h)),
        out_shape=jax.ShapeDtypeStruct((batch * seq, GROUP_W), BF16),
        scratch_shapes=[pltpu.VMEM((seq + ctx_len, HEAD), BF16),
                        pltpu.VMEM((HEAD + ATTN_ONES_ROWS, seq + ctx_len), BF16)],
        compiler_params=_cparams(("arbitrary", "arbitrary", "arbitrary")),
        name="diff_attn",
    )(qkv, qkv, qkv, kv_ctx, kv_ctx, lq1, lk1, lq2, lk2, norm_w)


def _split3(x):
    hi = x.astype(BF16)
    r1 = x - hi.astype(F32)
    mid = r1.astype(BF16)
    lo = (r1 - mid.astype(F32)).astype(BF16)
    return hi, mid, lo


def _cumsum_rows(tri, x):
    hi, mid, lo = _split3(x)
    return (jnp.dot(tri, hi, preferred_element_type=F32)
            + jnp.dot(tri, mid, preferred_element_type=F32)
            + jnp.dot(tri, lo, preferred_element_type=F32))


def _tri(n, lower):
    r = lax.broadcasted_iota(jnp.int32, (n, n), 0)
    c = lax.broadcasted_iota(jnp.int32, (n, n), 1)
    return jnp.where((r >= c) if lower else (r <= c), 1.0, 0.0).astype(BF16)


def _hgrn_ctx_state(lg2f, v, forward):
    n = lg2f.shape[0]
    b = _cumsum_rows(_tri(n, forward), lg2f)
    b_end = b[n - 1:n] if forward else b[0:1]
    k_hat = ((1.0 - jnp.exp2(lg2f)) * jnp.exp2(b_end - b)).astype(BF16)
    return _dot_tn(v, k_hat)


def _hgrn_stream(q_ref, lg_ref, v_ref, rows, state, tri, forward, out_ref):
    L = HG_CHUNK
    n_sub = L // HG_SUB
    G = (rows.stop - rows.start) // L
    x3 = lg_ref[rows, :].reshape(G, L, HEAD)
    parts = jnp.concatenate(_split3(x3), axis=2)
    bs = [jnp.dot(tri, parts[c], preferred_element_type=F32) for c in range(G)]
    yield
    b = jnp.stack([y[:, 0:HEAD] + y[:, HEAD:2 * HEAD] + y[:, 2 * HEAD:3 * HEAD] for y in bs])
    b_end = b[:, L - 1:L] if forward else b[:, 0:1]
    k = 1.0 - jnp.exp2(x3)
    qf = q_ref[rows, :].reshape(G, L, HEAD).astype(F32)
    q_in = (qf * jnp.exp2(b)).astype(BF16)
    k_hat = (k * jnp.exp2(b_end - b)).astype(BF16)
    e_end = jnp.exp2(b_end)

    refs = [b[:, j * HG_SUB + HG_SUB // 2:j * HG_SUB + HG_SUB // 2 + 1] for j in range(n_sub)]
    ref_own = jnp.concatenate([jnp.broadcast_to(r, (G, HG_SUB, HEAD)) for r in refs], axis=1)
    k_own = (k * jnp.exp2(ref_own - b)).astype(BF16)

    def zero_rows(n):
        return [jnp.zeros((G, n, HEAD), BF16)] if n else []

    qs, ks = [], []
    for j in range(n_sub):
        lo_r, hi_r = j * HG_SUB, (j + 1) * HG_SUB
        live = slice(lo_r, L) if forward else slice(0, hi_r)
        piece = (qf[:, live] * jnp.exp2(b[:, live] - refs[j])).astype(BF16)
        qs.append(jnp.concatenate(zero_rows(lo_r) + [piece] if forward else [piece] + zero_rows(L - hi_r), axis=1))
        ks.append(jnp.concatenate(zero_rows(lo_r) + [k_own[:, lo_r:hi_r]] + zero_rows(L - hi_r), axis=1))
    q_cat = jnp.concatenate(qs, axis=2)
    k_cat = jnp.concatenate(ks, axis=2)
    v3 = v_ref[rows, :].reshape(G, L, HEAD)
    yield
    a_raw = [_dot_nt(q_cat[c], k_cat[c]) for c in range(G)]
    ds = [_dot_tn(v3[c], k_hat[c]) for c in range(G)]
    yield
    ar = lax.broadcasted_iota(jnp.int32, (L, L), 0)
    ac = lax.broadcasted_iota(jnp.int32, (L, L), 1)
    causal = (ar >= ac) if forward else (ar <= ac)
    o_intra = [jnp.dot(jnp.where(causal, a_raw[c], 0.0).astype(BF16), v3[c], preferred_element_type=F32)
               for c in range(G)]
    yield
    p = state[forward]
    p_before = [None] * G
    for c in (range(G) if forward else reversed(range(G))):
        p_before[c] = p.astype(BF16)
        p = p * e_end[c] + ds[c]
    state[forward] = p
    yield
    out_ref[rows, :] = jnp.concatenate([o_intra[c] + _dot_nt(q_in[c], p_before[c]) for c in range(G)], axis=0)


def _advance(gens):
    for gen in list(gens):
        if next(gen, "done") == "done":
            gens.remove(gen)


def _hgrn_head(q_ref, lf_ref, lb_ref, v_ref, g_ref, lfc_ref, lbc_ref, vc_ref, nw_ref, o_ref, of_scr, ob_scr,
               lanes, tri_lo, tri_up):
    seq = q_ref.shape[0]
    rows = min(HG_GROUP * HG_CHUNK, seq)
    ng = seq // rows
    vc = vc_ref[:, lanes]
    state = {True: _hgrn_ctx_state(lfc_ref[:, lanes], vc, True),
             False: _hgrn_ctx_state(lbc_ref[:, lanes], vc, False)}
    yield
    q_h, lf_h, lb_h, v_h = q_ref.at[:, lanes], lf_ref.at[:, lanes], lb_ref.at[:, lanes], v_ref.at[:, lanes]
    of_h, ob_h = of_scr.at[:, lanes], ob_scr.at[:, lanes]
    streams = []
    for g in range(ng):
        rf = slice(g * rows, (g + 1) * rows)
        rb = slice((ng - 1 - g) * rows, (ng - g) * rows)
        streams.append(_hgrn_stream(q_h, lf_h, v_h, rf, state, tri_lo, True, of_h))
        streams.append(_hgrn_stream(q_h, lb_h, v_h, rb, state, tri_up, False, ob_h))
    live = []
    while streams or live:
        if streams:
            live.append(streams.pop(0))
        _advance(live)
        yield
    o = of_h[...] + ob_h[...]
    ms = jnp.mean(o * o, axis=1, keepdims=True)
    o_ref[:, lanes] = (o * lax.rsqrt(ms + EPS) * nw_ref[...] * g_ref[:, lanes].astype(F32)).astype(o_ref.dtype)


def _hgrn_kernel(q_ref, lf_ref, lb_ref, v_ref, g_ref, lfc_ref, lbc_ref, vc_ref, nw_ref, o_ref, of_scr, ob_scr):
    tri_lo = _tri(HG_CHUNK, True)
    tri_up = _tri(HG_CHUNK, False)
    n_heads = q_ref.shape[1] // HEAD
    heads = [_hgrn_head(q_ref, lf_ref, lb_ref, v_ref, g_ref, lfc_ref, lbc_ref, vc_ref, nw_ref, o_ref,
                        of_scr, ob_scr, slice(h * HEAD, (h + 1) * HEAD), tri_lo, tri_up) for h in range(n_heads)]
    live, tick = [], 0
    while heads or live:
        if heads and tick % HG_HEAD_LAG == 0:
            live.append(heads.pop(0))
        _advance(live)
        tick += 1


def _hgrn(hq, logf, hq_ctx, logf_ctx, norm_w, *, batch, seq, ctx_len):
    w = HG_HEADS_PER_STEP * HEAD
    nb = GROUP_W // w
    return pl.pallas_call(
        _hgrn_kernel,
        grid=(batch, nb),
        in_specs=[pl.BlockSpec((seq, w), lambda b, h: (b, h)),
                  pl.BlockSpec((seq, w), lambda b, h: (b, h)),
                  pl.BlockSpec((seq, w), lambda b, h: (b, nb + h)),
                  pl.BlockSpec((seq, w), lambda b, h: (b, nb + h)),
                  pl.BlockSpec((seq, w), lambda b, h: (b, 2 * nb + h)),
                  pl.BlockSpec((ctx_len, w), lambda b, h: (b, h)),
                  pl.BlockSpec((ctx_len, w), lambda b, h: (b, nb + h)),
                  pl.BlockSpec((ctx_len, w), lambda b, h: (b, 2 * nb + h)),
                  pl.BlockSpec((1, HEAD), lambda b, h: (0, 0))],
        out_specs=pl.BlockSpec((seq, w), lambda b, h: (b, h)),
        out_shape=jax.ShapeDtypeStruct((batch * seq, GROUP_W), BF16),
        scratch_shapes=[pltpu.VMEM((seq, w), F32), pltpu.VMEM((seq, w), F32)],
        compiler_params=_cparams(("arbitrary", "arbitrary")),
        name="hgrn2",
    )(hq, logf, logf, hq, hq, logf_ctx, logf_ctx, hq_ctx, norm_w)


def _layer_norm(z, g, b):
    mu = jnp.mean(z, axis=1, keepdims=True)
    zc = z - mu
    var = jnp.mean(zc * zc, axis=1, keepdims=True)
    return zc * lax.rsqrt(var + EPS) * g + b


def _outproj_kernel(oa_ref, oh_ref, x_ref, w_ref, gt_ref, zc_ref, rstd_ref):
    y = (jnp.dot(oa_ref[...], w_ref[:GROUP_W, :], preferred_element_type=F32)
         + jnp.dot(oh_ref[...], w_ref[GROUP_W:, :], preferred_element_type=F32))
    z = ALPHA * x_ref[...] + gt_ref[0] * y
    zc = z - jnp.mean(z, axis=1, keepdims=True)
    zc_ref[...] = zc
    rstd_ref[...] = lax.rsqrt(jnp.mean(zc * zc, axis=1, keepdims=True) + EPS)


def _outproj(o_da, o_hg, x2d, w_out, gt1, *, seq, tm):
    m, d = x2d.shape
    tiles_per_seq = seq // tm
    return pl.pallas_call(
        _outproj_kernel,
        grid=(m // tm,),
        in_specs=[pl.BlockSpec((tm, GROUP_W), lambda i: (i, 0)),
                  pl.BlockSpec((tm, GROUP_W), lambda i: (i, 0)),
                  pl.BlockSpec((tm, d), lambda i: (i, 0)),
                  pl.BlockSpec((d, d), lambda i: (0, 0), pipeline_mode=pl.Buffered(1)),
                  pl.BlockSpec((1, 1, d), lambda i: (i // tiles_per_seq, 0, 0))],
        out_specs=[pl.BlockSpec((tm, d), lambda i: (i, 0)), pl.BlockSpec((tm, 1), lambda i: (i, 0))],
        out_shape=[jax.ShapeDtypeStruct((m, d), F32), jax.ShapeDtypeStruct((m, 1), F32)],
        compiler_params=_cparams(("arbitrary",)),
        name="outproj_ln1",
    )(o_da, o_hg, x2d, w_out, gt1)


def _ffn_kernel(zc_ref, rstd_ref, w1_ref, w2_ref, sh_ref, sc_ref, gt_ref, g1_ref, b1_ref, g2_ref, b2_ref,
                o_ref, acc_scr, h_scr, u_scr):
    j = pl.program_id(1)
    last = pl.num_programs(1) - 1
    d = zc_ref.shape[1]

    def partial_sum():
        a = jnp.maximum(jnp.dot(u_scr[...], w1_ref[...], preferred_element_type=F32), 0.0)
        return jnp.dot((a * a).astype(BF16), w2_ref[...], preferred_element_type=F32)

    @pl.when(j == 0)
    def _():
        rstd = rstd_ref[...]
        for k in range(d // FFN_NORM_SUB):
            cols = slice(k * FFN_NORM_SUB, (k + 1) * FFN_NORM_SUB)
            h = zc_ref[:, cols] * rstd * g1_ref[:, cols] + b1_ref[:, cols]
            h_scr[:, cols] = h
            u_scr[:, cols] = (h * (1.0 + sc_ref[0][:, cols]) + sh_ref[0][:, cols]).astype(BF16)
        acc_scr[...] = partial_sum()

    @pl.when((j > 0) & (j < last))
    def _():
        acc_scr[...] += partial_sum()

    @pl.when(j == last)
    def _():
        ff = acc_scr[...] + partial_sum()
        o_ref[...] = _layer_norm(ALPHA * h_scr[...] + gt_ref[0] * ff, g2_ref[...], b2_ref[...])


def _ffn(zc, rstd, w1, w2, sh2, sc2, gt2, ln1_g, ln1_b, ln2_g, ln2_b, *, seq, tm, tf):
    m, d = zc.shape
    dff = w1.shape[1]
    assert dff // tf >= 2
    tiles_per_seq = seq // tm
    mod = pl.BlockSpec((1, 1, d), lambda i, j: (i // tiles_per_seq, 0, 0))
    vec = pl.BlockSpec((1, d), lambda i, j: (0, 0))
    return pl.pallas_call(
        _ffn_kernel,
        grid=(m // tm, dff // tf),
        in_specs=[pl.BlockSpec((tm, d), lambda i, j: (i, 0)),
                  pl.BlockSpec((tm, 1), lambda i, j: (i, 0)),
                  pl.BlockSpec((d, tf), lambda i, j: (0, j)),
                  pl.BlockSpec((tf, d), lambda i, j: (j, 0)),
                  mod, mod, mod, vec, vec, vec, vec],
        out_specs=pl.BlockSpec((tm, d), lambda i, j: (i, 0)),
        out_shape=jax.ShapeDtypeStruct((m, d), F32),
        scratch_shapes=[pltpu.VMEM((tm, d), F32), pltpu.VMEM((tm, d), F32), pltpu.VMEM((tm, d), BF16)],
        compiler_params=_cparams(("arbitrary", "arbitrary")),
        name="ffn_ln2",
    )(zc, rstd, w1, w2, sh2, sc2, gt2, ln1_g, ln1_b, ln2_g, ln2_b)


def kernel(x, c, ctx, c_ctx, w_ada, b_ada, w_in, lam_q1, lam_k1, lam_q2, lam_k2, da_norm_w,
           hg_lb_fwd, hg_lb_bwd, hg_norm_w, w_out, ln1_g, ln1_b, w_ff1, w_ff2, ln2_g, ln2_b):
    batch, seq, d = x.shape
    ctx_len = ctx.shape[1]
    assert d == D_MODEL and w_in.shape[0] == DEPTH and seq % GRID_W == 0
    l = 0

    n_cond = batch + 1
    pad = (-n_cond) % 8
    cond = jnp.concatenate([c, c_ctx[None], jnp.zeros((pad, d), F32)], axis=0)
    mod = _adaln(cond, w_ada[l], b_ada[l][None]).reshape(n_cond + pad, N_ADA, 1, d)
    sh1, sc1, gt1, sh2, sc2, gt2 = [mod[:batch, j] for j in range(N_ADA)]
    csh1, csc1 = mod[batch:batch + 1, 0], mod[batch:batch + 1, 1]

    w_in_b = w_in[l].astype(BF16)
    x2d = x.reshape(batch * seq, d)
    ctx2d = ctx.reshape(batch * ctx_len, d)
    lbs = jnp.stack([hg_lb_fwd.reshape(DEPTH + 1, GROUP_W), hg_lb_bwd.reshape(DEPTH + 1, GROUP_W)])

    tm = min(512, seq)
    tmi = min(256, seq)
    tmc = min(256, ctx_len)
    rope_q = _rope_tables(seq, D_QK ** -0.5 * LOG2E)
    rope_k = _rope_tables(seq, 1.0)
    qkv, hq, logf, w_out_b, w_ff1_b, w_ff2_b = _inproj(
        x2d, sh1, sc1, w_in_b, (0, 1, 2, 3, 6, 7, 4, 5),
        ("rope", "rope", "copy", "silu", "copy", "silu", "logf", "logf"),
        ((BF16, 3), (BF16, 3), (F32, 2)),
        rows_per_mod=seq, tm=tmi, ropes=(rope_q, rope_k), seq=seq, lbs=lbs,
        casts=(w_out[l], w_ff1[l], w_ff2[l]), name="inproj")
    m_ctx = batch * ctx_len
    kvh_ctx, logf_ctx = _inproj(
        ctx2d, csh1, csc1, w_in_b, (1, 2, 6, 4, 5), ("copy", "copy", "copy", "logf", "logf"),
        ((BF16, 3), (F32, 2)), rows_per_mod=m_ctx, tm=tmc, lbs=lbs, name="inproj_ctx")

    o_da = _attention(qkv, kvh_ctx, lam_q1[l][None], lam_k1[l][None], lam_q2[l][None], lam_k2[l][None],
                      da_norm_w[l][None], batch=batch, seq=seq, ctx_len=ctx_len, tq=min(2048, seq))
    o_hg = _hgrn(hq, logf, kvh_ctx, logf_ctx, hg_norm_w[l][None], batch=batch, seq=seq, ctx_len=ctx_len)

    zc, rstd = _outproj(o_da, o_hg, x2d, w_out_b, gt1, seq=seq, tm=min(256, seq))
    out = _ffn(zc, rstd, w_ff1_b, w_ff2_b, sh2, sc2, gt2, ln1_g[l][None], ln1_b[l][None],
               ln2_g[l][None], ln2_b[l][None], seq=seq, tm=tm, tf=1024)
    return out.reshape(batch, seq, d)
```

```python
import functools
import math

import jax
import jax.numpy as jnp
import numpy as np
from jax import lax
from jax.experimental import pallas as pl
from jax.experimental.pallas import tpu as pltpu

F32 = jnp.float32
BF16 = jnp.bfloat16

D_MODEL = 2048
GRID_W = 64
HEAD = 128
N_HEADS = 8
GROUP_W = N_HEADS * HEAD
N_GROUPS = 8
D_QK = 64
ROT_AXIS = 32
ROPE_BASE = 10000.0
D_FF = 4 * D_MODEL
N_ADA = 6
EPS = 1e-5
DEPTH = 1
ALPHA = (2.0 * DEPTH) ** 0.25
LAM_INIT = 0.8 - 0.6 * math.exp(-0.3 * 0)
HG_CHUNK = 64
HG_SUB = 16
HG_GROUP = 8
HG_HEADS_PER_STEP = 2
HG_HEAD_LAG = 7
ATTN_SUB = 128
ATTN_ONES_ROWS = 16
FFN_NORM_SUB = 256
INPROJ_SUB = 256
LOG2E = 1.4426950408889634
BF16_SUBLANES = 16
F32_SUBLANES = 8

VMEM_LIMIT = 56 * 1024 * 1024


def _cparams(sem):
    return pltpu.CompilerParams(dimension_semantics=sem, vmem_limit_bytes=VMEM_LIMIT)


def _sigmoid(z):
    return 1.0 / (1.0 + jnp.exp(-z))


def _dot_nt(a, b):
    return lax.dot_general(a, b, (((1,), (1,)), ((), ())), preferred_element_type=F32)


def _dot_tn(a, b):
    return lax.dot_general(a, b, (((0,), (0,)), ((), ())), preferred_element_type=F32)


def _adaln_kernel(cond_ref, w_ref, b_ref, o_ref):
    c = cond_ref[...]
    s = (c * _sigmoid(c)).astype(BF16)
    o_ref[...] = jnp.dot(s, w_ref[...].astype(BF16), preferred_element_type=F32) + b_ref[...]


def _adaln(cond, w, b):
    rows, d = cond.shape
    n = w.shape[1]
    tn = 1024
    return pl.pallas_call(
        _adaln_kernel,
        grid=(n // tn,),
        in_specs=[pl.BlockSpec((rows, d), lambda j: (0, 0)),
                  pl.BlockSpec((d, tn), lambda j: (0, j)),
                  pl.BlockSpec((1, tn), lambda j: (0, j))],
        out_specs=pl.BlockSpec((rows, tn), lambda j: (0, j)),
        out_shape=jax.ShapeDtypeStruct((rows, n), F32),
        compiler_params=_cparams(("arbitrary",)),
        name="adaln",
    )(cond, w, b)


def _rope_tables(seq, scale):
    t = np.arange(seq)
    row = (t // GRID_W).astype(np.float64)
    col = (t % GRID_W).astype(np.float64)
    lane = np.arange(HEAD)
    inv_freq = ROPE_BASE ** (-np.arange(0, ROT_AXIS, 2, dtype=np.float64) / ROT_AXIS)
    freq = inv_freq[lane % (ROT_AXIS // 2)]
    use_col = ((lane % D_QK) // ROT_AXIS) == 1
    pos = np.where(use_col[None, :], col[:, None], row[:, None])
    ang = pos * freq[None, :]
    first_half = (lane % ROT_AXIS) < (ROT_AXIS // 2)
    cos, sin = np.cos(ang), np.sin(ang)
    t1 = np.where(first_half[None, :], -sin, 0.0)
    t2 = np.where(first_half[None, :], 0.0, sin)
    return jnp.asarray((np.stack([cos, t1, t2]) * scale).astype(np.float32))


def _inproj_kernel(*refs, groups, kinds, places, n_rope, n_out, n_cast):
    x_ref, sh_ref, sc_ref, w_ref = refs[:4]
    rope_refs = refs[4:4 + n_rope]
    n_lb = 1 if "logf" in kinds else 0
    lb_ref = refs[4 + n_rope] if n_lb else None
    cast_in = refs[4 + n_rope + n_lb:4 + n_rope + n_lb + n_cast]
    out_refs = refs[-1 - n_out - n_cast:-1 - n_cast]
    cast_out = refs[-1 - n_cast:-1]
    u_scr = refs[-1]
    n_sub = GROUP_W // INPROJ_SUB
    heads_per_sub = INPROJ_SUB // HEAD
    first_logf = kinds.index("logf") if "logf" in kinds else None

    u_scr[...] = (x_ref[...] * (1.0 + sc_ref[0]) + sh_ref[0]).astype(BF16)

    def sub_dot(g, k):
        c0 = groups[g] * GROUP_W + k * INPROJ_SUB
        return jnp.dot(u_scr[...], w_ref[:, c0:c0 + INPROJ_SUB], preferred_element_type=F32)

    def epilogue(g, k, acc, rope_ref):
        kind = kinds[g]
        o_ref = out_refs[places[g][0]]
        c0 = places[g][1] * GROUP_W + k * INPROJ_SUB
        if kind == "rope":
            for h in range(heads_per_sub):
                xh = acc[:, h * HEAD:(h + 1) * HEAD]
                y = (xh * rope_ref[0]
                     + pltpu.roll(xh, HEAD - ROT_AXIS // 2, 1) * rope_ref[1]
                     + pltpu.roll(xh, ROT_AXIS // 2, 1) * rope_ref[2])
                o_ref[:, c0 + h * HEAD:c0 + (h + 1) * HEAD] = y.astype(o_ref.dtype)
        elif kind == "copy":
            o_ref[:, c0:c0 + INPROJ_SUB] = acc.astype(o_ref.dtype)
        elif kind == "silu":
            o_ref[:, c0:c0 + INPROJ_SUB] = (acc * _sigmoid(acc)).astype(o_ref.dtype)
        elif kind == "logf":
            a = lb_ref[g - first_logf][:, k * INPROJ_SUB:(k + 1) * INPROJ_SUB]
            e = jnp.exp(a - jnp.max(a, axis=0, keepdims=True))
            lb = e[0:1] / jnp.sum(e, axis=0, keepdims=True)
            o_ref[:, c0:c0 + INPROJ_SUB] = jnp.log2(lb + (1.0 - lb) * _sigmoid(acc)).astype(o_ref.dtype)
        else:
            raise ValueError(kind)

    rope_of, ri = {}, 0
    for g, kind in enumerate(kinds):
        if kind == "rope":
            rope_of[g] = rope_refs[ri]
            ri += 1
    jobs = [(g, k) for g in range(len(kinds)) for k in range(n_sub)]
    cast_at = {(c * len(jobs)) // n_cast: c for c in range(n_cast)}
    nxt = sub_dot(*jobs[0])
    for idx, (g, k) in enumerate(jobs):
        acc = nxt
        if idx + 1 < len(jobs):
            nxt = sub_dot(*jobs[idx + 1])
        epilogue(g, k, acc, rope_of.get(g))
        if idx in cast_at:
            c = cast_at[idx]
            cast_out[c][...] = cast_in[c][...].astype(BF16)


def _inproj(x2d, sh, sc, w, groups, kinds, outs, *, rows_per_mod, tm, ropes=(), seq=None, lbs=None, casts=(),
            name):
    m, d = x2d.shape
    ng = len(groups)
    assert m % tm == 0 and rows_per_mod % tm == 0 and sum(n for _, n in outs) == ng
    tiles_per_mod = rows_per_mod // tm
    in_specs = [pl.BlockSpec((tm, d), lambda i: (i, 0)),
                pl.BlockSpec((1, 1, d), lambda i: (i // tiles_per_mod, 0, 0)),
                pl.BlockSpec((1, 1, d), lambda i: (i // tiles_per_mod, 0, 0)),
                pl.BlockSpec(w.shape, lambda i: (0, 0), pipeline_mode=pl.Buffered(1))]
    args = [x2d, sh, sc, w]
    for tab in ropes:
        tiles_per_seq = seq // tm
        in_specs.append(pl.BlockSpec((3, tm, HEAD), lambda i: (0, i % tiles_per_seq, 0)))
        args.append(tab)
    if lbs is not None:
        first_logf = kinds.index("logf")
        assert all(k == "logf" for k in kinds[first_logf:]) and lbs.shape[0] == ng - first_logf
        in_specs.append(pl.BlockSpec(lbs.shape, lambda i: (0, 0, 0)))
        args.append(lbs)
    out_specs, out_shape, places = [], [], []
    for o, (dtype, n) in enumerate(outs):
        out_specs.append(pl.BlockSpec((tm, n * GROUP_W), lambda i: (i, 0)))
        out_shape.append(jax.ShapeDtypeStruct((m, n * GROUP_W), dtype))
        places += [(o, p) for p in range(n)]
    steps = m // tm
    for cw in casts:
        slab = cw.shape[0] // steps
        assert cw.shape[0] % steps == 0 and slab % BF16_SUBLANES == 0
        in_specs.append(pl.BlockSpec((slab, cw.shape[1]), lambda i: (i, 0)))
        args.append(cw)
        out_specs.append(pl.BlockSpec((slab, cw.shape[1]), lambda i: (i, 0)))
        out_shape.append(jax.ShapeDtypeStruct(cw.shape, BF16))
    return pl.pallas_call(
        functools.partial(_inproj_kernel, groups=tuple(groups), kinds=tuple(kinds), places=tuple(places),
                          n_rope=len(ropes), n_out=len(outs), n_cast=len(casts)),
        grid=(steps,),
        in_specs=in_specs,
        out_specs=out_specs,
        out_shape=out_shape,
        scratch_shapes=[pltpu.VMEM((tm, d), BF16)],
        compiler_params=_cparams(("arbitrary",)),
        name=name,
    )(*args)


def _attn_kernel(q_ref, k_ref, v_ref, kc_ref, vc_ref, lq1_ref, lk1_ref, lq2_ref, lk2_ref, nw_ref, o_ref,
                 k_scr, v_scr):
    seq, ctx_len = k_ref.shape[0], kc_ref.shape[0]
    n_keys = seq + ctx_len
    k_split = max(256, (n_keys // 2) // 256 * 256)

    @pl.when(pl.program_id(2) == 0)
    def _():
        k_scr[0:seq, :] = k_ref[...]
        k_scr[seq:n_keys, :] = kc_ref[...]
        v_scr[0:HEAD, 0:seq] = v_ref[...].astype(F32).T.astype(BF16)
        v_scr[0:HEAD, seq:n_keys] = vc_ref[...].astype(F32).T.astype(BF16)
        v_scr[HEAD:, :] = jnp.ones((v_scr.shape[0] - HEAD, n_keys), BF16)

    lam = (jnp.exp(jnp.sum(lq1_ref[...] * lk1_ref[...], axis=1, keepdims=True))
           - jnp.exp(jnp.sum(lq2_ref[...] * lk2_ref[...], axis=1, keepdims=True)) + LAM_INIT)
    ts = min(ATTN_SUB, q_ref.shape[0])
    lane = lax.broadcasted_iota(jnp.int32, (ts, HEAD), 1)
    zero = jnp.zeros((ts, HEAD), BF16)
    n_sub = q_ref.shape[0] // ts

    def scores(r):
        q = q_ref[r * ts:(r + 1) * ts, :]
        q2 = jnp.concatenate([jnp.where(lane < D_QK, q, zero), jnp.where(lane >= D_QK, q, zero)], axis=0)
        sa = _dot_nt(k_scr[0:k_split, :], q2)
        sb = _dot_nt(k_scr[k_split:n_keys, :], q2)
        return sa, sb, jnp.maximum(jnp.max(sa, axis=0, keepdims=True), jnp.max(sb, axis=0, keepdims=True))

    def probs(sa, sb, m):
        return jnp.exp2(sa - m).astype(BF16), jnp.exp2(sb - m).astype(BF16)

    def attend(r, pa, pb):
        ox = (jnp.dot(v_scr[:, 0:k_split], pa, preferred_element_type=F32)
              + jnp.dot(v_scr[:, k_split:n_keys], pb, preferred_element_type=F32))
        on = ox[0:HEAD] / ox[HEAD:HEAD + 1]
        o = (on[:, :ts] - lam * on[:, ts:]).T
        ms = jnp.mean(o * o, axis=1, keepdims=True)
        o_ref[r * ts:(r + 1) * ts, :] = (
            o * lax.rsqrt(ms + EPS) * nw_ref[...] * (1.0 - LAM_INIT)).astype(o_ref.dtype)

    st_a, st_b = {}, {}
    for t in range(n_sub + 2):
        if t < n_sub:
            st_a[t] = scores(t)
        if 0 <= t - 1 < n_sub:
            st_b[t - 1] = probs(*st_a.pop(t - 1))
        if 0 <= t - 2 < n_sub:
            attend(t - 2, *st_b.pop(t - 2))


def _attention(qkv, kv_ctx, lq1, lk1, lq2, lk2, norm_w, *, batch, seq, ctx_len, tq):
    nq = seq // tq
    vec = lambda n: pl.BlockSpec((1, n), lambda b, h, i: (0, 0))
    return pl.pallas_call(
        _attn_kernel,
        grid=(batch, N_HEADS, nq),
        in_specs=[pl.BlockSpec((tq, HEAD), lambda b, h, i: (b * nq + i, h)),
                  pl.BlockSpec((seq, HEAD), lambda b, h, i: (b, N_HEADS + h)),
                  pl.BlockSpec((seq, HEAD), lambda b, h, i: (b, 2 * N_HEADS + h)),
                  pl.BlockSpec((ctx_len, HEAD), lambda b, h, i: (b, h)),
                  pl.BlockSpec((ctx_len, HEAD), lambda b, h, i: (b, N_HEADS + h)),
                  vec(D_QK), vec(D_QK), vec(D_QK), vec(D_QK), vec(HEAD)],
        out_specs=pl.BlockSpec((tq, HEAD), lambda b, h, i: (b * nq + i, h)),
        out_shape=jax.ShapeDtypeStruct((batch * seq, GROUP_W), BF16),
        scratch_shapes=[pltpu.VMEM((seq + ctx_len, HEAD), BF16),
                        pltpu.VMEM((HEAD + ATTN_ONES_ROWS, seq + ctx_len), BF16)],
        compiler_params=_cparams(("arbitrary", "arbitrary", "arbitrary")),
        name="diff_attn",
    )(qkv, qkv, qkv, kv_ctx, kv_ctx, lq1, lk1, lq2, lk2, norm_w)


def _split3(x):
    hi = x.astype(BF16)
    r1 = x - hi.astype(F32)
    mid = r1.astype(BF16)
    lo = (r1 - mid.astype(F32)).astype(BF16)
    return hi, mid, lo


def _cumsum_rows(tri, x):
    hi, mid, lo = _split3(x)
    return (jnp.dot(tri, hi, preferred_element_type=F32)
            + jnp.dot(tri, mid, preferred_element_type=F32)
            + jnp.dot(tri, lo, preferred_element_type=F32))


def _tri(n, lower):
    r = lax.broadcasted_iota(jnp.int32, (n, n), 0)
    c = lax.broadcasted_iota(jnp.int32, (n, n), 1)
    return jnp.where((r >= c) if lower else (r <= c), 1.0, 0.0).astype(BF16)


def _hgrn_ctx_state(lg2f, v, forward):
    n = lg2f.shape[0]
    b = _cumsum_rows(_tri(n, forward), lg2f)
    b_end = b[n - 1:n] if forward else b[0:1]
    k_hat = ((1.0 - jnp.exp2(lg2f)) * jnp.exp2(b_end - b)).astype(BF16)
    return _dot_tn(v, k_hat)


def _chunk_cumsum(x, forward):
    g, n, c = x.shape
    row = lax.broadcasted_iota(jnp.int32, (1, n, c), 1)
    s = 1
    while s < n:
        if s % F32_SUBLANES == 0:
            z = jnp.zeros((g, s, c), x.dtype)
            shifted = (jnp.concatenate([z, x[:, :n - s]], axis=1) if forward
                       else jnp.concatenate([x[:, s:], z], axis=1))
        else:
            rolled = pltpu.roll(x, s if forward else n - s, 1)
            shifted = jnp.where((row >= s) if forward else (row < n - s), rolled, 0.0)
        x = x + shifted
        s *= 2
    return x


def _hgrn_stream(q_ref, lg_ref, v_ref, rows, state, forward, out_ref):
    L = HG_CHUNK
    n_sub = L // HG_SUB
    G = (rows.stop - rows.start) // L
    x3 = lg_ref[rows, :].reshape(G, L, HEAD)
    b = _chunk_cumsum(x3, forward)
    b_end = b[:, L - 1:L] if forward else b[:, 0:1]
    k = 1.0 - jnp.exp2(x3)
    qf = q_ref[rows, :].reshape(G, L, HEAD).astype(F32)
    q_in = (qf * jnp.exp2(b)).astype(BF16)
    k_hat = (k * jnp.exp2(b_end - b)).astype(BF16)
    e_end = jnp.exp2(b_end)

    refs = [b[:, j * HG_SUB + HG_SUB // 2:j * HG_SUB + HG_SUB // 2 + 1] for j in range(n_sub)]
    ref_own = jnp.concatenate([jnp.broadcast_to(r, (G, HG_SUB, HEAD)) for r in refs], axis=1)
    k_own = (k * jnp.exp2(ref_own - b)).astype(BF16)

    def zero_rows(n):
        return [jnp.zeros((G, n, HEAD), BF16)] if n else []

    qs, ks = [], []
    for j in range(n_sub):
        lo_r, hi_r = j * HG_SUB, (j + 1) * HG_SUB
        live = slice(lo_r, L) if forward else slice(0, hi_r)
        piece = (qf[:, live] * jnp.exp2(b[:, live] - refs[j])).astype(BF16)
        qs.append(jnp.concatenate(zero_rows(lo_r) + [piece] if forward else [piece] + zero_rows(L - hi_r), axis=1))
        ks.append(jnp.concatenate(zero_rows(lo_r) + [k_own[:, lo_r:hi_r]] + zero_rows(L - hi_r), axis=1))
    q_cat = jnp.concatenate(qs, axis=2)
    k_cat = jnp.concatenate(ks, axis=2)
    v3 = v_ref[rows, :].reshape(G, L, HEAD)
    yield
    a_raw = [_dot_nt(q_cat[c], k_cat[c]) for c in range(G)]
    ds = [_dot_tn(v3[c], k_hat[c]) for c in range(G)]
    yield
    ar = lax.broadcasted_iota(jnp.int32, (L, L), 0)
    ac = lax.broadcasted_iota(jnp.int32, (L, L), 1)
    causal = (ar >= ac) if forward else (ar <= ac)
    o_intra = [jnp.dot(jnp.where(causal, a_raw[c], 0.0).astype(BF16), v3[c], preferred_element_type=F32)
               for c in range(G)]
    yield
    p = state[forward]
    p_before = [None] * G
    for c in (range(G) if forward else reversed(range(G))):
        p_before[c] = p.astype(BF16)
        p = p * e_end[c] + ds[c]
    state[forward] = p
    yield
    out_ref[rows, :] = jnp.concatenate([o_intra[c] + _dot_nt(q_in[c], p_before[c]) for c in range(G)], axis=0)


def _advance(gens):
    for gen in list(gens):
        if next(gen, "done") == "done":
            gens.remove(gen)


def _hgrn_head(q_ref, lf_ref, lb_ref, v_ref, g_ref, lfc_ref, lbc_ref, vc_ref, nw_ref, o_ref, of_scr, ob_scr,
               lanes):
    seq = q_ref.shape[0]
    rows = min(HG_GROUP * HG_CHUNK, seq)
    ng = seq // rows
    vc = vc_ref[:, lanes]
    state = {True: _hgrn_ctx_state(lfc_ref[:, lanes], vc, True),
             False: _hgrn_ctx_state(lbc_ref[:, lanes], vc, False)}
    yield
    q_h, lf_h, lb_h, v_h = q_ref.at[:, lanes], lf_ref.at[:, lanes], lb_ref.at[:, lanes], v_ref.at[:, lanes]
    of_h, ob_h = of_scr.at[:, lanes], ob_scr.at[:, lanes]
    streams = []
    for g in range(ng):
        rf = slice(g * rows, (g + 1) * rows)
        rb = slice((ng - 1 - g) * rows, (ng - g) * rows)
        streams.append(_hgrn_stream(q_h, lf_h, v_h, rf, state, True, of_h))
        streams.append(_hgrn_stream(q_h, lb_h, v_h, rb, state, False, ob_h))
    live = []
    while streams or live:
        if streams:
            live.append(streams.pop(0))
        _advance(live)
        yield
    o = of_h[...] + ob_h[...]
    ms = jnp.mean(o * o, axis=1, keepdims=True)
    o_ref[:, lanes] = (o * lax.rsqrt(ms + EPS) * nw_ref[...] * g_ref[:, lanes].astype(F32)).astype(o_ref.dtype)


def _hgrn_kernel(q_ref, lf_ref, lb_ref, v_ref, g_ref, lfc_ref, lbc_ref, vc_ref, nw_ref, o_ref, of_scr, ob_scr):
    n_heads = q_ref.shape[1] // HEAD
    heads = [_hgrn_head(q_ref, lf_ref, lb_ref, v_ref, g_ref, lfc_ref, lbc_ref, vc_ref, nw_ref, o_ref,
                        of_scr, ob_scr, slice(h * HEAD, (h + 1) * HEAD)) for h in range(n_heads)]
    live, tick = [], 0
    while heads or live:
        if heads and tick % HG_HEAD_LAG == 0:
            live.append(heads.pop(0))
        _advance(live)
        tick += 1


def _hgrn(hq, logf, hq_ctx, logf_ctx, norm_w, *, batch, seq, ctx_len):
    w = HG_HEADS_PER_STEP * HEAD
    nb = GROUP_W // w
    return pl.pallas_call(
        _hgrn_kernel,
        grid=(batch, nb),
        in_specs=[pl.BlockSpec((seq, w), lambda b, h: (b, h)),
                  pl.BlockSpec((seq, w), lambda b, h: (b, h)),
                  pl.BlockSpec((seq, w), lambda b, h: (b, nb + h)),
                  pl.BlockSpec((seq, w), lambda b, h: (b, nb + h)),
                  pl.BlockSpec((seq, w), lambda b, h: (b, 2 * nb + h)),
                  pl.BlockSpec((ctx_len, w), lambda b, h: (b, h)),
                  pl.BlockSpec((ctx_len, w), lambda b, h: (b, nb + h)),
                  pl.BlockSpec((ctx_len, w), lambda b, h: (b, 2 * nb + h)),
                  pl.BlockSpec((1, HEAD), lambda b, h: (0, 0))],
        out_specs=pl.BlockSpec((seq, w), lambda b, h: (b, h)),
        out_shape=jax.ShapeDtypeStruct((batch * seq, GROUP_W), BF16),
        scratch_shapes=[pltpu.VMEM((seq, w), F32), pltpu.VMEM((seq, w), F32)],
        compiler_params=_cparams(("arbitrary", "arbitrary")),
        name="hgrn2",
    )(hq, logf, logf, hq, hq, logf_ctx, logf_ctx, hq_ctx, norm_w)


def _layer_norm(z, g, b):
    mu = jnp.mean(z, axis=1, keepdims=True)
    zc = z - mu
    var = jnp.mean(zc * zc, axis=1, keepdims=True)
    return zc * lax.rsqrt(var + EPS) * g + b


def _outproj_kernel(oa_ref, oh_ref, x_ref, w_ref, gt_ref, zc_ref, rstd_ref):
    y = (jnp.dot(oa_ref[...], w_ref[:GROUP_W, :], preferred_element_type=F32)
         + jnp.dot(oh_ref[...], w_ref[GROUP_W:, :], preferred_element_type=F32))
    z = ALPHA * x_ref[...] + gt_ref[0] * y
    zc = z - jnp.mean(z, axis=1, keepdims=True)
    zc_ref[...] = zc
    rstd_ref[...] = lax.rsqrt(jnp.mean(zc * zc, axis=1, keepdims=True) + EPS)


def _outproj(o_da, o_hg, x2d, w_out, gt1, *, seq, tm):
    m, d = x2d.shape
    tiles_per_seq = seq // tm
    return pl.pallas_call(
        _outproj_kernel,
        grid=(m // tm,),
        in_specs=[pl.BlockSpec((tm, GROUP_W), lambda i: (i, 0)),
                  pl.BlockSpec((tm, GROUP_W), lambda i: (i, 0)),
                  pl.BlockSpec((tm, d), lambda i: (i, 0)),
                  pl.BlockSpec((d, d), lambda i: (0, 0), pipeline_mode=pl.Buffered(1)),
                  pl.BlockSpec((1, 1, d), lambda i: (i // tiles_per_seq, 0, 0))],
        out_specs=[pl.BlockSpec((tm, d), lambda i: (i, 0)), pl.BlockSpec((tm, 1), lambda i: (i, 0))],
        out_shape=[jax.ShapeDtypeStruct((m, d), F32), jax.ShapeDtypeStruct((m, 1), F32)],
        compiler_params=_cparams(("arbitrary",)),
        name="outproj_ln1",
    )(o_da, o_hg, x2d, w_out, gt1)


def _ffn_kernel(zc_ref, rstd_ref, w1_ref, w2_ref, sh_ref, sc_ref, gt_ref, g1_ref, b1_ref, g2_ref, b2_ref,
                o_ref, acc_scr, h_scr, u_scr):
    j = pl.program_id(1)
    last = pl.num_programs(1) - 1
    d = zc_ref.shape[1]

    def partial_sum():
        a = jnp.maximum(jnp.dot(u_scr[...], w1_ref[...], preferred_element_type=F32), 0.0)
        return jnp.dot((a * a).astype(BF16), w2_ref[...], preferred_element_type=F32)

    @pl.when(j == 0)
    def _():
        rstd = rstd_ref[...]
        for k in range(d // FFN_NORM_SUB):
            cols = slice(k * FFN_NORM_SUB, (k + 1) * FFN_NORM_SUB)
            h = zc_ref[:, cols] * rstd * g1_ref[:, cols] + b1_ref[:, cols]
            h_scr[:, cols] = h
            u_scr[:, cols] = (h * (1.0 + sc_ref[0][:, cols]) + sh_ref[0][:, cols]).astype(BF16)
        acc_scr[...] = partial_sum()

    @pl.when((j > 0) & (j < last))
    def _():
        acc_scr[...] += partial_sum()

    @pl.when(j == last)
    def _():
        ff = acc_scr[...] + partial_sum()
        o_ref[...] = _layer_norm(ALPHA * h_scr[...] + gt_ref[0] * ff, g2_ref[...], b2_ref[...])


def _ffn(zc, rstd, w1, w2, sh2, sc2, gt2, ln1_g, ln1_b, ln2_g, ln2_b, *, seq, tm, tf):
    m, d = zc.shape
    dff = w1.shape[1]
    assert dff // tf >= 2
    tiles_per_seq = seq // tm
    mod = pl.BlockSpec((1, 1, d), lambda i, j: (i // tiles_per_seq, 0, 0))
    vec = pl.BlockSpec((1, d), lambda i, j: (0, 0))
    return pl.pallas_call(
        _ffn_kernel,
        grid=(m // tm, dff // tf),
        in_specs=[pl.BlockSpec((tm, d), lambda i, j: (i, 0)),
                  pl.BlockSpec((tm, 1), lambda i, j: (i, 0)),
                  pl.BlockSpec((d, tf), lambda i, j: (0, j)),
                  pl.BlockSpec((tf, d), lambda i, j: (j, 0)),
                  mod, mod, mod, vec, vec, vec, vec],
        out_specs=pl.BlockSpec((tm, d), lambda i, j: (i, 0)),
        out_shape=jax.ShapeDtypeStruct((m, d), F32),
        scratch_shapes=[pltpu.VMEM((tm, d), F32), pltpu.VMEM((tm, d), F32), pltpu.VMEM((tm, d), BF16)],
        compiler_params=_cparams(("arbitrary", "arbitrary")),
        name="ffn_ln2",
    )(zc, rstd, w1, w2, sh2, sc2, gt2, ln1_g, ln1_b, ln2_g, ln2_b)


def kernel(x, c, ctx, c_ctx, w_ada, b_ada, w_in, lam_q1, lam_k1, lam_q2, lam_k2, da_norm_w,
           hg_lb_fwd, hg_lb_bwd, hg_norm_w, w_out, ln1_g, ln1_b, w_ff1, w_ff2, ln2_g, ln2_b):
    batch, seq, d = x.shape
    ctx_len = ctx.shape[1]
    assert d == D_MODEL and w_in.shape[0] == DEPTH and seq % GRID_W == 0
    l = 0

    n_cond = batch + 1
    pad = (-n_cond) % 8
    cond = jnp.concatenate([c, c_ctx[None], jnp.zeros((pad, d), F32)], axis=0)
    mod = _adaln(cond, w_ada[l], b_ada[l][None]).reshape(n_cond + pad, N_ADA, 1, d)
    sh1, sc1, gt1, sh2, sc2, gt2 = [mod[:batch, j] for j in range(N_ADA)]
    csh1, csc1 = mod[batch:batch + 1, 0], mod[batch:batch + 1, 1]

    w_in_b = w_in[l].astype(BF16)
    x2d = x.reshape(batch * seq, d)
    ctx2d = ctx.reshape(batch * ctx_len, d)
    lbs = jnp.stack([hg_lb_fwd.reshape(DEPTH + 1, GROUP_W), hg_lb_bwd.reshape(DEPTH + 1, GROUP_W)])

    tm = min(512, seq)
    tmi = min(256, seq)
    tmc = min(256, ctx_len)
    rope_q = _rope_tables(seq, D_QK ** -0.5 * LOG2E)
    rope_k = _rope_tables(seq, 1.0)
    qkv, hq, logf, w_out_b, w_ff1_b, w_ff2_b = _inproj(
        x2d, sh1, sc1, w_in_b, (0, 1, 2, 3, 6, 7, 4, 5),
        ("rope", "rope", "copy", "silu", "copy", "silu", "logf", "logf"),
        ((BF16, 3), (BF16, 3), (F32, 2)),
        rows_per_mod=seq, tm=tmi, ropes=(rope_q, rope_k), seq=seq, lbs=lbs,
        casts=(w_out[l], w_ff1[l], w_ff2[l]), name="inproj")
    m_ctx = batch * ctx_len
    kvh_ctx, logf_ctx = _inproj(
        ctx2d, csh1, csc1, w_in_b, (1, 2, 6, 4, 5), ("copy", "copy", "copy", "logf", "logf"),
        ((BF16, 3), (F32, 2)), rows_per_mod=m_ctx, tm=tmc, lbs=lbs, name="inproj_ctx")

    o_da = _attention(qkv, kvh_ctx, lam_q1[l][None], lam_k1[l][None], lam_q2[l][None], lam_k2[l][None],
                      da_norm_w[l][None], batch=batch, seq=seq, ctx_len=ctx_len, tq=min(2048, seq))
    o_hg = _hgrn(hq, logf, kvh_ctx, logf_ctx, hg_norm_w[l][None], batch=batch, seq=seq, ctx_len=ctx_len)

    zc, rstd = _outproj(o_da, o_hg, x2d, w_out_b, gt1, seq=seq, tm=min(256, seq))
    out = _ffn(zc, rstd, w_ff1_b, w_ff2_b, sh2, sc2, gt2, ln1_g[l][None], ln1_b[l][None],
               ln2_g[l][None], ln2_b[l][None], seq=seq, tm=tm, tf=1024)
    return out.reshape(batch, seq, d)
```

```python
import functools
import math

import jax
import jax.numpy as jnp
import numpy as np
from jax import lax
from jax.experimental import pallas as pl
from jax.experimental.pallas import tpu as pltpu

F32 = jnp.float32
BF16 = jnp.bfloat16

D_MODEL = 2048
GRID_W = 64
HEAD = 128
N_HEADS = 8
GROUP_W = N_HEADS * HEAD
N_GROUPS = 8
D_QK = 64
ROT_AXIS = 32
ROPE_BASE = 10000.0
D_FF = 4 * D_MODEL
N_ADA = 6
EPS = 1e-5
DEPTH = 1
ALPHA = (2.0 * DEPTH) ** 0.25
LAM_INIT = 0.8 - 0.6 * math.exp(-0.3 * 0)
HG_CHUNK = 64
HG_SUB = 16
HG_GROUP = 8
HG_HEADS_PER_STEP = 2
HG_HEAD_LAG = 7
ATTN_SUB = 256
ATTN_ONES_ROWS = 16
FFN_NORM_SUB = 256
INPROJ_SUB = 256
LOG2E = 1.4426950408889634
BF16_SUBLANES = 16
F32_SUBLANES = 8

VMEM_LIMIT = 56 * 1024 * 1024


def _cparams(sem):
    return pltpu.CompilerParams(dimension_semantics=sem, vmem_limit_bytes=VMEM_LIMIT)


def _sigmoid(z):
    return 1.0 / (1.0 + jnp.exp(-z))


def _dot_nt(a, b):
    return lax.dot_general(a, b, (((1,), (1,)), ((), ())), preferred_element_type=F32)


def _dot_tn(a, b):
    return lax.dot_general(a, b, (((0,), (0,)), ((), ())), preferred_element_type=F32)


def _adaln_kernel(cond_ref, w_ref, b_ref, o_ref):
    c = cond_ref[...]
    s = (c * _sigmoid(c)).astype(BF16)
    o_ref[...] = jnp.dot(s, w_ref[...].astype(BF16), preferred_element_type=F32) + b_ref[...]


def _adaln(cond, w, b):
    rows, d = cond.shape
    n = w.shape[1]
    tn = 1024
    return pl.pallas_call(
        _adaln_kernel,
        grid=(n // tn,),
        in_specs=[pl.BlockSpec((rows, d), lambda j: (0, 0)),
                  pl.BlockSpec((d, tn), lambda j: (0, j)),
                  pl.BlockSpec((1, tn), lambda j: (0, j))],
        out_specs=pl.BlockSpec((rows, tn), lambda j: (0, j)),
        out_shape=jax.ShapeDtypeStruct((rows, n), F32),
        compiler_params=_cparams(("arbitrary",)),
        name="adaln",
    )(cond, w, b)


def _rope_tables(seq, scale):
    t = np.arange(seq)
    row = (t // GRID_W).astype(np.float64)
    col = (t % GRID_W).astype(np.float64)
    lane = np.arange(HEAD)
    inv_freq = ROPE_BASE ** (-np.arange(0, ROT_AXIS, 2, dtype=np.float64) / ROT_AXIS)
    freq = inv_freq[lane % (ROT_AXIS // 2)]
    use_col = ((lane % D_QK) // ROT_AXIS) == 1
    pos = np.where(use_col[None, :], col[:, None], row[:, None])
    ang = pos * freq[None, :]
    first_half = (lane % ROT_AXIS) < (ROT_AXIS // 2)
    cos, sin = np.cos(ang), np.sin(ang)
    t1 = np.where(first_half[None, :], -sin, 0.0)
    t2 = np.where(first_half[None, :], 0.0, sin)
    return jnp.asarray((np.stack([cos, t1, t2]) * scale).astype(np.float32))


def _inproj_kernel(*refs, groups, kinds, places, n_rope, n_out, n_cast):
    x_ref, sh_ref, sc_ref, w_ref = refs[:4]
    rope_refs = refs[4:4 + n_rope]
    n_lb = 1 if "logf" in kinds else 0
    lb_ref = refs[4 + n_rope] if n_lb else None
    cast_in = refs[4 + n_rope + n_lb:4 + n_rope + n_lb + n_cast]
    out_refs = refs[-1 - n_out - n_cast:-1 - n_cast]
    cast_out = refs[-1 - n_cast:-1]
    u_scr = refs[-1]
    n_sub = GROUP_W // INPROJ_SUB
    heads_per_sub = INPROJ_SUB // HEAD
    first_logf = kinds.index("logf") if "logf" in kinds else None

    u_scr[...] = (x_ref[...] * (1.0 + sc_ref[0]) + sh_ref[0]).astype(BF16)

    def sub_dot(g, k):
        c0 = groups[g] * GROUP_W + k * INPROJ_SUB
        return jnp.dot(u_scr[...], w_ref[:, c0:c0 + INPROJ_SUB], preferred_element_type=F32)

    def epilogue(g, k, acc, rope_ref):
        kind = kinds[g]
        o_ref = out_refs[places[g][0]]
        c0 = places[g][1] * GROUP_W + k * INPROJ_SUB
        if kind == "rope":
            for h in range(heads_per_sub):
                xh = acc[:, h * HEAD:(h + 1) * HEAD]
                y = (xh * rope_ref[0]
                     + pltpu.roll(xh, HEAD - ROT_AXIS // 2, 1) * rope_ref[1]
                     + pltpu.roll(xh, ROT_AXIS // 2, 1) * rope_ref[2])
                o_ref[:, c0 + h * HEAD:c0 + (h + 1) * HEAD] = y.astype(o_ref.dtype)
        elif kind == "copy":
            o_ref[:, c0:c0 + INPROJ_SUB] = acc.astype(o_ref.dtype)
        elif kind == "silu":
            o_ref[:, c0:c0 + INPROJ_SUB] = (acc * _sigmoid(acc)).astype(o_ref.dtype)
        elif kind == "logf":
            a = lb_ref[g - first_logf][:, k * INPROJ_SUB:(k + 1) * INPROJ_SUB]
            e = jnp.exp(a - jnp.max(a, axis=0, keepdims=True))
            lb = e[0:1] / jnp.sum(e, axis=0, keepdims=True)
            o_ref[:, c0:c0 + INPROJ_SUB] = jnp.log2(lb + (1.0 - lb) * _sigmoid(acc)).astype(o_ref.dtype)
        else:
            raise ValueError(kind)

    rope_of, ri = {}, 0
    for g, kind in enumerate(kinds):
        if kind == "rope":
            rope_of[g] = rope_refs[ri]
            ri += 1
    jobs = [(g, k) for g in range(len(kinds)) for k in range(n_sub)]
    cast_at = {(c * len(jobs)) // n_cast: c for c in range(n_cast)}
    nxt = sub_dot(*jobs[0])
    for idx, (g, k) in enumerate(jobs):
        acc = nxt
        if idx + 1 < len(jobs):
            nxt = sub_dot(*jobs[idx + 1])
        epilogue(g, k, acc, rope_of.get(g))
        if idx in cast_at:
            c = cast_at[idx]
            cast_out[c][...] = cast_in[c][...].astype(BF16)


def _inproj(x2d, sh, sc, w, groups, kinds, outs, *, rows_per_mod, tm, ropes=(), seq=None, lbs=None, casts=(),
            name):
    m, d = x2d.shape
    ng = len(groups)
    assert m % tm == 0 and rows_per_mod % tm == 0 and sum(n for _, n in outs) == ng
    tiles_per_mod = rows_per_mod // tm
    in_specs = [pl.BlockSpec((tm, d), lambda i: (i, 0)),
                pl.BlockSpec((1, 1, d), lambda i: (i // tiles_per_mod, 0, 0)),
                pl.BlockSpec((1, 1, d), lambda i: (i // tiles_per_mod, 0, 0)),
                pl.BlockSpec(w.shape, lambda i: (0, 0), pipeline_mode=pl.Buffered(1))]
    args = [x2d, sh, sc, w]
    for tab in ropes:
        tiles_per_seq = seq // tm
        in_specs.append(pl.BlockSpec((3, tm, HEAD), lambda i: (0, i % tiles_per_seq, 0)))
        args.append(tab)
    if lbs is not None:
        first_logf = kinds.index("logf")
        assert all(k == "logf" for k in kinds[first_logf:]) and lbs.shape[0] == ng - first_logf
        in_specs.append(pl.BlockSpec(lbs.shape, lambda i: (0, 0, 0)))
        args.append(lbs)
    out_specs, out_shape, places = [], [], []
    for o, (dtype, n) in enumerate(outs):
        out_specs.append(pl.BlockSpec((tm, n * GROUP_W), lambda i: (i, 0)))
        out_shape.append(jax.ShapeDtypeStruct((m, n * GROUP_W), dtype))
        places += [(o, p) for p in range(n)]
    steps = m // tm
    for cw in casts:
        slab = cw.shape[0] // steps
        assert cw.shape[0] % steps == 0 and slab % BF16_SUBLANES == 0
        in_specs.append(pl.BlockSpec((slab, cw.shape[1]), lambda i: (i, 0)))
        args.append(cw)
        out_specs.append(pl.BlockSpec((slab, cw.shape[1]), lambda i: (i, 0)))
        out_shape.append(jax.ShapeDtypeStruct(cw.shape, BF16))
    return pl.pallas_call(
        functools.partial(_inproj_kernel, groups=tuple(groups), kinds=tuple(kinds), places=tuple(places),
                          n_rope=len(ropes), n_out=len(outs), n_cast=len(casts)),
        grid=(steps,),
        in_specs=in_specs,
        out_specs=out_specs,
        out_shape=out_shape,
        scratch_shapes=[pltpu.VMEM((tm, d), BF16)],
        compiler_params=_cparams(("arbitrary",)),
        name=name,
    )(*args)


def _attn_kernel(q_ref, k_ref, v_ref, kc_ref, vc_ref, lq1_ref, lk1_ref, lq2_ref, lk2_ref, nw_ref, o_ref,
                 k_scr, v_scr):
    seq, ctx_len = k_ref.shape[0], kc_ref.shape[0]
    n_keys = seq + ctx_len
    k_split = max(256, (n_keys // 2) // 256 * 256)

    @pl.when(pl.program_id(2) == 0)
    def _():
        k_scr[0:seq, :] = k_ref[...]
        k_scr[seq:n_keys, :] = kc_ref[...]
        v_scr[0:HEAD, 0:seq] = v_ref[...].astype(F32).T.astype(BF16)
        v_scr[0:HEAD, seq:n_keys] = vc_ref[...].astype(F32).T.astype(BF16)
        v_scr[HEAD:, :] = jnp.ones((v_scr.shape[0] - HEAD, n_keys), BF16)

    lam = (jnp.exp(jnp.sum(lq1_ref[...] * lk1_ref[...], axis=1, keepdims=True))
           - jnp.exp(jnp.sum(lq2_ref[...] * lk2_ref[...], axis=1, keepdims=True)) + LAM_INIT)
    ts = min(ATTN_SUB, q_ref.shape[0])
    lane = lax.broadcasted_iota(jnp.int32, (ts, HEAD), 1)
    zero = jnp.zeros((ts, HEAD), BF16)
    n_sub = q_ref.shape[0] // ts

    def scores(r):
        q = q_ref[r * ts:(r + 1) * ts, :]
        q2 = jnp.concatenate([jnp.where(lane < D_QK, q, zero), jnp.where(lane >= D_QK, q, zero)], axis=0)
        sa = _dot_nt(k_scr[0:k_split, :], q2)
        sb = _dot_nt(k_scr[k_split:n_keys, :], q2)
        return sa, sb, jnp.maximum(jnp.max(sa, axis=0, keepdims=True), jnp.max(sb, axis=0, keepdims=True))

    def probs(sa, sb, m):
        return jnp.exp2(sa - m).astype(BF16), jnp.exp2(sb - m).astype(BF16)

    def attend(r, pa, pb):
        ox = (jnp.dot(v_scr[:, 0:k_split], pa, preferred_element_type=F32)
              + jnp.dot(v_scr[:, k_split:n_keys], pb, preferred_element_type=F32))
        on = ox[0:HEAD] / ox[HEAD:HEAD + 1]
        o = (on[:, :ts] - lam * on[:, ts:]).T
        ms = jnp.mean(o * o, axis=1, keepdims=True)
        o_ref[r * ts:(r + 1) * ts, :] = (
            o * lax.rsqrt(ms + EPS) * nw_ref[...] * (1.0 - LAM_INIT)).astype(o_ref.dtype)

    st_a, st_b = {}, {}
    for t in range(n_sub + 2):
        if t < n_sub:
            st_a[t] = scores(t)
        if 0 <= t - 1 < n_sub:
            st_b[t - 1] = probs(*st_a.pop(t - 1))
        if 0 <= t - 2 < n_sub:
            attend(t - 2, *st_b.pop(t - 2))


def _attention(qkv, kv_ctx, lq1, lk1, lq2, lk2, norm_w, *, batch, seq, ctx_len, tq):
    nq = seq // tq
    vec = lambda n: pl.BlockSpec((1, n), lambda b, h, i: (0, 0))
    return pl.pallas_call(
        _attn_kernel,
        grid=(batch, N_HEADS, nq),
        in_specs=[pl.BlockSpec((tq, HEAD), lambda b, h, i: (b * nq + i, h)),
                  pl.BlockSpec((seq, HEAD), lambda b, h, i: (b, N_HEADS + h)),
                  pl.BlockSpec((seq, HEAD), lambda b, h, i: (b, 2 * N_HEADS + h)),
                  pl.BlockSpec((ctx_len, HEAD), lambda b, h, i: (b, h)),
                  pl.BlockSpec((ctx_len, HEAD), lambda b, h, i: (b, N_HEADS + h)),
                  vec(D_QK), vec(D_QK), vec(D_QK), vec(D_QK), vec(HEAD)],
        out_specs=pl.BlockSpec((tq, HEAD), lambda b, h, i: (b * nq + i, h)),
        out_shape=jax.ShapeDtypeStruct((batch * seq, GROUP_W), BF16),
        scratch_shapes=[pltpu.VMEM((seq + ctx_len, HEAD), BF16),
                        pltpu.VMEM((HEAD + ATTN_ONES_ROWS, seq + ctx_len), BF16)],
        compiler_params=_cparams(("arbitrary", "arbitrary", "arbitrary")),
        name="diff_attn",
    )(qkv, qkv, qkv, kv_ctx, kv_ctx, lq1, lk1, lq2, lk2, norm_w)


def _split3(x):
    hi = x.astype(BF16)
    r1 = x - hi.astype(F32)
    mid = r1.astype(BF16)
    lo = (r1 - mid.astype(F32)).astype(BF16)
    return hi, mid, lo


def _cumsum_rows(tri, x):
    hi, mid, lo = _split3(x)
    return (jnp.dot(tri, hi, preferred_element_type=F32)
            + jnp.dot(tri, mid, preferred_element_type=F32)
            + jnp.dot(tri, lo, preferred_element_type=F32))


def _tri(n, lower):
    r = lax.broadcasted_iota(jnp.int32, (n, n), 0)
    c = lax.broadcasted_iota(jnp.int32, (n, n), 1)
    return jnp.where((r >= c) if lower else (r <= c), 1.0, 0.0).astype(BF16)


def _hgrn_ctx_state(lg2f, v, forward):
    n = lg2f.shape[0]
    b = _cumsum_rows(_tri(n, forward), lg2f)
    b_end = b[n - 1:n] if forward else b[0:1]
    k_hat = ((1.0 - jnp.exp2(lg2f)) * jnp.exp2(b_end - b)).astype(BF16)
    return _dot_tn(v, k_hat)


def _chunk_cumsum(x, forward):
    g, n, c = x.shape
    row = lax.broadcasted_iota(jnp.int32, (1, n, c), 1)
    s = 1
    while s < n:
        if s % F32_SUBLANES == 0:
            z = jnp.zeros((g, s, c), x.dtype)
            shifted = (jnp.concatenate([z, x[:, :n - s]], axis=1) if forward
                       else jnp.concatenate([x[:, s:], z], axis=1))
        else:
            rolled = pltpu.roll(x, s if forward else n - s, 1)
            shifted = jnp.where((row >= s) if forward else (row < n - s), rolled, 0.0)
        x = x + shifted
        s *= 2
    return x


def _hgrn_stream(q_ref, lg_ref, v_ref, rows, state, forward, out_ref):
    L = HG_CHUNK
    n_sub = L // HG_SUB
    G = (rows.stop - rows.start) // L
    x3 = lg_ref[rows, :].reshape(G, L, HEAD)
    b = _chunk_cumsum(x3, forward)
    b_end = b[:, L - 1:L] if forward else b[:, 0:1]
    k = 1.0 - jnp.exp2(x3)
    qf = q_ref[rows, :].reshape(G, L, HEAD).astype(F32)
    q_in = (qf * jnp.exp2(b)).astype(BF16)
    k_hat = (k * jnp.exp2(b_end - b)).astype(BF16)
    e_end = jnp.exp2(b_end)

    refs = [b[:, j * HG_SUB + HG_SUB // 2:j * HG_SUB + HG_SUB // 2 + 1] for j in range(n_sub)]
    ref_own = jnp.concatenate([jnp.broadcast_to(r, (G, HG_SUB, HEAD)) for r in refs], axis=1)
    k_own = (k * jnp.exp2(ref_own - b)).astype(BF16)

    def zero_rows(n):
        return [jnp.zeros((G, n, HEAD), BF16)] if n else []

    qs, ks = [], []
    for j in range(n_sub):
        lo_r, hi_r = j * HG_SUB, (j + 1) * HG_SUB
        live = slice(lo_r, L) if forward else slice(0, hi_r)
        piece = (qf[:, live] * jnp.exp2(b[:, live] - refs[j])).astype(BF16)
        qs.append(jnp.concatenate(zero_rows(lo_r) + [piece] if forward else [piece] + zero_rows(L - hi_r), axis=1))
        ks.append(jnp.concatenate(zero_rows(lo_r) + [k_own[:, lo_r:hi_r]] + zero_rows(L - hi_r), axis=1))
    q_cat = jnp.concatenate(qs, axis=2)
    k_cat = jnp.concatenate(ks, axis=2)
    v3 = v_ref[rows, :].reshape(G, L, HEAD)
    yield
    a_raw = [_dot_nt(q_cat[c], k_cat[c]) for c in range(G)]
    ds = [_dot_tn(v3[c], k_hat[c]) for c in range(G)]
    yield
    ar = lax.broadcasted_iota(jnp.int32, (L, L), 0)
    ac = lax.broadcasted_iota(jnp.int32, (L, L), 1)
    causal = (ar >= ac) if forward else (ar <= ac)
    o_intra = [jnp.dot(jnp.where(causal, a_raw[c], 0.0).astype(BF16), v3[c], preferred_element_type=F32)
               for c in range(G)]
    yield
    p = state[forward]
    p_before = [None] * G
    for c in (range(G) if forward else reversed(range(G))):
        p_before[c] = p.astype(BF16)
        p = p * e_end[c] + ds[c]
    state[forward] = p
    yield
    out_ref[rows, :] = jnp.concatenate([o_intra[c] + _dot_nt(q_in[c], p_before[c]) for c in range(G)], axis=0)


def _advance(gens):
    for gen in list(gens):
        if next(gen, "done") == "done":
            gens.remove(gen)


def _hgrn_head(q_ref, lf_ref, lb_ref, v_ref, g_ref, lfc_ref, lbc_ref, vc_ref, nw_ref, o_ref, of_scr, ob_scr,
               lanes):
    seq = q_ref.shape[0]
    rows = min(HG_GROUP * HG_CHUNK, seq)
    ng = seq // rows
    vc = vc_ref[:, lanes]
    state = {True: _hgrn_ctx_state(lfc_ref[:, lanes], vc, True),
             False: _hgrn_ctx_state(lbc_ref[:, lanes], vc, False)}
    yield
    q_h, lf_h, lb_h, v_h = q_ref.at[:, lanes], lf_ref.at[:, lanes], lb_ref.at[:, lanes], v_ref.at[:, lanes]
    of_h, ob_h = of_scr.at[:, lanes], ob_scr.at[:, lanes]
    streams = []
    for g in range(ng):
        rf = slice(g * rows, (g + 1) * rows)
        rb = slice((ng - 1 - g) * rows, (ng - g) * rows)
        streams.append(_hgrn_stream(q_h, lf_h, v_h, rf, state, True, of_h))
        streams.append(_hgrn_stream(q_h, lb_h, v_h, rb, state, False, ob_h))
    live = []
    while streams or live:
        if streams:
            live.append(streams.pop(0))
        _advance(live)
        yield
    o = of_h[...] + ob_h[...]
    ms = jnp.mean(o * o, axis=1, keepdims=True)
    o_ref[:, lanes] = (o * lax.rsqrt(ms + EPS) * nw_ref[...] * g_ref[:, lanes].astype(F32)).astype(o_ref.dtype)


def _hgrn_kernel(q_ref, lf_ref, lb_ref, v_ref, g_ref, lfc_ref, lbc_ref, vc_ref, nw_ref, o_ref, of_scr, ob_scr):
    n_heads = q_ref.shape[1] // HEAD
    heads = [_hgrn_head(q_ref, lf_ref, lb_ref, v_ref, g_ref, lfc_ref, lbc_ref, vc_ref, nw_ref, o_ref,
                        of_scr, ob_scr, slice(h * HEAD, (h + 1) * HEAD)) for h in range(n_heads)]
    live, tick = [], 0
    while heads or live:
        if heads and tick % HG_HEAD_LAG == 0:
            live.append(heads.pop(0))
        _advance(live)
        tick += 1


def _hgrn(hq, logf, hq_ctx, logf_ctx, norm_w, *, batch, seq, ctx_len):
    w = HG_HEADS_PER_STEP * HEAD
    nb = GROUP_W // w
    return pl.pallas_call(
        _hgrn_kernel,
        grid=(batch, nb),
        in_specs=[pl.BlockSpec((seq, w), lambda b, h: (b, h)),
                  pl.BlockSpec((seq, w), lambda b, h: (b, h)),
                  pl.BlockSpec((seq, w), lambda b, h: (b, nb + h)),
                  pl.BlockSpec((seq, w), lambda b, h: (b, nb + h)),
                  pl.BlockSpec((seq, w), lambda b, h: (b, 2 * nb + h)),
                  pl.BlockSpec((ctx_len, w), lambda b, h: (b, h)),
                  pl.BlockSpec((ctx_len, w), lambda b, h: (b, nb + h)),
                  pl.BlockSpec((ctx_len, w), lambda b, h: (b, 2 * nb + h)),
                  pl.BlockSpec((1, HEAD), lambda b, h: (0, 0))],
        out_specs=pl.BlockSpec((seq, w), lambda b, h: (b, h)),
        out_shape=jax.ShapeDtypeStruct((batch * seq, GROUP_W), BF16),
        scratch_shapes=[pltpu.VMEM((seq, w), F32), pltpu.VMEM((seq, w), F32)],
        compiler_params=_cparams(("arbitrary", "arbitrary")),
        name="hgrn2",
    )(hq, logf, logf, hq, hq, logf_ctx, logf_ctx, hq_ctx, norm_w)


def _layer_norm(z, g, b):
    mu = jnp.mean(z, axis=1, keepdims=True)
    zc = z - mu
    var = jnp.mean(zc * zc, axis=1, keepdims=True)
    return zc * lax.rsqrt(var + EPS) * g + b


def _outproj_kernel(oa_ref, oh_ref, x_ref, w_ref, gt_ref, zc_ref, rstd_ref):
    y = (jnp.dot(oa_ref[...], w_ref[:GROUP_W, :], preferred_element_type=F32)
         + jnp.dot(oh_ref[...], w_ref[GROUP_W:, :], preferred_element_type=F32))
    z = ALPHA * x_ref[...] + gt_ref[0] * y
    zc = z - jnp.mean(z, axis=1, keepdims=True)
    zc_ref[...] = zc
    rstd_ref[...] = lax.rsqrt(jnp.mean(zc * zc, axis=1, keepdims=True) + EPS)


def _outproj(o_da, o_hg, x2d, w_out, gt1, *, seq, tm):
    m, d = x2d.shape
    tiles_per_seq = seq // tm
    return pl.pallas_call(
        _outproj_kernel,
        grid=(m // tm,),
        in_specs=[pl.BlockSpec((tm, GROUP_W), lambda i: (i, 0)),
                  pl.BlockSpec((tm, GROUP_W), lambda i: (i, 0)),
                  pl.BlockSpec((tm, d), lambda i: (i, 0)),
                  pl.BlockSpec((d, d), lambda i: (0, 0), pipeline_mode=pl.Buffered(1)),
                  pl.BlockSpec((1, 1, d), lambda i: (i // tiles_per_seq, 0, 0))],
        out_specs=[pl.BlockSpec((tm, d), lambda i: (i, 0)), pl.BlockSpec((tm, 1), lambda i: (i, 0))],
        out_shape=[jax.ShapeDtypeStruct((m, d), F32), jax.ShapeDtypeStruct((m, 1), F32)],
        compiler_params=_cparams(("arbitrary",)),
        name="outproj_ln1",
    )(o_da, o_hg, x2d, w_out, gt1)


def _ffn_kernel(zc_ref, rstd_ref, w1_ref, w2_ref, sh_ref, sc_ref, gt_ref, g1_ref, b1_ref, g2_ref, b2_ref,
                o_ref, acc_scr, h_scr, u_scr):
    j = pl.program_id(1)
    last = pl.num_programs(1) - 1
    d = zc_ref.shape[1]

    def partial_sum():
        a = jnp.maximum(jnp.dot(u_scr[...], w1_ref[...], preferred_element_type=F32), 0.0)
        return jnp.dot((a * a).astype(BF16), w2_ref[...], preferred_element_type=F32)

    @pl.when(j == 0)
    def _():
        rstd = rstd_ref[...]
        for k in range(d // FFN_NORM_SUB):
            cols = slice(k * FFN_NORM_SUB, (k + 1) * FFN_NORM_SUB)
            h = zc_ref[:, cols] * rstd * g1_ref[:, cols] + b1_ref[:, cols]
            h_scr[:, cols] = h
            u_scr[:, cols] = (h * (1.0 + sc_ref[0][:, cols]) + sh_ref[0][:, cols]).astype(BF16)
        acc_scr[...] = partial_sum()

    @pl.when((j > 0) & (j < last))
    def _():
        acc_scr[...] += partial_sum()

    @pl.when(j == last)
    def _():
        ff = acc_scr[...] + partial_sum()
        o_ref[...] = _layer_norm(ALPHA * h_scr[...] + gt_ref[0] * ff, g2_ref[...], b2_ref[...])


def _ffn(zc, rstd, w1, w2, sh2, sc2, gt2, ln1_g, ln1_b, ln2_g, ln2_b, *, seq, tm, tf):
    m, d = zc.shape
    dff = w1.shape[1]
    assert dff // tf >= 2
    tiles_per_seq = seq // tm
    mod = pl.BlockSpec((1, 1, d), lambda i, j: (i // tiles_per_seq, 0, 0))
    vec = pl.BlockSpec((1, d), lambda i, j: (0, 0))
    return pl.pallas_call(
        _ffn_kernel,
        grid=(m // tm, dff // tf),
        in_specs=[pl.BlockSpec((tm, d), lambda i, j: (i, 0)),
                  pl.BlockSpec((tm, 1), lambda i, j: (i, 0)),
                  pl.BlockSpec((d, tf), lambda i, j: (0, j)),
                  pl.BlockSpec((tf, d), lambda i, j: (j, 0)),
                  mod, mod, mod, vec, vec, vec, vec],
        out_specs=pl.BlockSpec((tm, d), lambda i, j: (i, 0)),
        out_shape=jax.ShapeDtypeStruct((m, d), F32),
        scratch_shapes=[pltpu.VMEM((tm, d), F32), pltpu.VMEM((tm, d), F32), pltpu.VMEM((tm, d), BF16)],
        compiler_params=_cparams(("arbitrary", "arbitrary")),
        name="ffn_ln2",
    )(zc, rstd, w1, w2, sh2, sc2, gt2, ln1_g, ln1_b, ln2_g, ln2_b)


def kernel(x, c, ctx, c_ctx, w_ada, b_ada, w_in, lam_q1, lam_k1, lam_q2, lam_k2, da_norm_w,
           hg_lb_fwd, hg_lb_bwd, hg_norm_w, w_out, ln1_g, ln1_b, w_ff1, w_ff2, ln2_g, ln2_b):
    batch, seq, d = x.shape
    ctx_len = ctx.shape[1]
    assert d == D_MODEL and w_in.shape[0] == DEPTH and seq % GRID_W == 0
    l = 0

    n_cond = batch + 1
    pad = (-n_cond) % 8
    cond = jnp.concatenate([c, c_ctx[None], jnp.zeros((pad, d), F32)], axis=0)
    mod = _adaln(cond, w_ada[l], b_ada[l][None]).reshape(n_cond + pad, N_ADA, 1, d)
    sh1, sc1, gt1, sh2, sc2, gt2 = [mod[:batch, j] for j in range(N_ADA)]
    csh1, csc1 = mod[batch:batch + 1, 0], mod[batch:batch + 1, 1]

    w_in_b = w_in[l].astype(BF16)
    x2d = x.reshape(batch * seq, d)
    ctx2d = ctx.reshape(batch * ctx_len, d)
    lbs = jnp.stack([hg_lb_fwd.reshape(DEPTH + 1, GROUP_W), hg_lb_bwd.reshape(DEPTH + 1, GROUP_W)])

    tm = min(512, seq)
    tmi = min(256, seq)
    tmc = min(256, ctx_len)
    rope_q = _rope_tables(seq, D_QK ** -0.5 * LOG2E)
    rope_k = _rope_tables(seq, 1.0)
    qkv, hq, logf, w_out_b, w_ff1_b, w_ff2_b = _inproj(
        x2d, sh1, sc1, w_in_b, (0, 1, 2, 3, 6, 7, 4, 5),
        ("rope", "rope", "copy", "silu", "copy", "silu", "logf", "logf"),
        ((BF16, 3), (BF16, 3), (F32, 2)),
        rows_per_mod=seq, tm=tmi, ropes=(rope_q, rope_k), seq=seq, lbs=lbs,
        casts=(w_out[l], w_ff1[l], w_ff2[l]), name="inproj")
    m_ctx = batch * ctx_len
    kvh_ctx, logf_ctx = _inproj(
        ctx2d, csh1, csc1, w_in_b, (1, 2, 6, 4, 5), ("copy", "copy", "copy", "logf", "logf"),
        ((BF16, 3), (F32, 2)), rows_per_mod=m_ctx, tm=tmc, lbs=lbs, name="inproj_ctx")

    o_da = _attention(qkv, kvh_ctx, lam_q1[l][None], lam_k1[l][None], lam_q2[l][None], lam_k2[l][None],
                      da_norm_w[l][None], batch=batch, seq=seq, ctx_len=ctx_len, tq=min(2048, seq))
    o_hg = _hgrn(hq, logf, kvh_ctx, logf_ctx, hg_norm_w[l][None], batch=batch, seq=seq, ctx_len=ctx_len)

    zc, rstd = _outproj(o_da, o_hg, x2d, w_out_b, gt1, seq=seq, tm=min(256, seq))
    out = _ffn(zc, rstd, w_ff1_b, w_ff2_b, sh2, sc2, gt2, ln1_g[l][None], ln1_b[l][None],
               ln2_g[l][None], ln2_b[l][None], seq=seq, tm=tm, tf=1024)
    return out.reshape(batch, seq, d)
```

```python
import functools
import math

import jax
import jax.numpy as jnp
import numpy as np
from jax import lax
from jax.experimental import pallas as pl
from jax.experimental.pallas import tpu as pltpu

F32 = jnp.float32
BF16 = jnp.bfloat16

D_MODEL = 2048
GRID_W = 64
HEAD = 128
N_HEADS = 8
GROUP_W = N_HEADS * HEAD
N_GROUPS = 8
D_QK = 64
ROT_AXIS = 32
ROPE_BASE = 10000.0
D_FF = 4 * D_MODEL
N_ADA = 6
EPS = 1e-5
DEPTH = 1
ALPHA = (2.0 * DEPTH) ** 0.25
LAM_INIT = 0.8 - 0.6 * math.exp(-0.3 * 0)
HG_CHUNK = 64
HG_SUB = 16
HG_GROUP = 8
HG_HEADS_PER_STEP = 2
HG_HEAD_LAG = 7
ATTN_SUB = 256
ADALN_STEPS = 8
ATTN_KEY_SEGS = 3
ATTN_ONES_ROWS = 16
FFN_NORM_SUB = 256
INPROJ_SUB = 256
LOG2E = 1.4426950408889634
BF16_SUBLANES = 16
F32_SUBLANES = 8

VMEM_LIMIT = 56 * 1024 * 1024


def _cparams(sem):
    return pltpu.CompilerParams(dimension_semantics=sem, vmem_limit_bytes=VMEM_LIMIT)


def _sigmoid(z):
    return 1.0 / (1.0 + jnp.exp(-z))


def _dot_nt(a, b):
    return lax.dot_general(a, b, (((1,), (1,)), ((), ())), preferred_element_type=F32)


def _dot_tn(a, b):
    return lax.dot_general(a, b, (((0,), (0,)), ((), ())), preferred_element_type=F32)


def _adaln_kernel(cond_ref, w_ref, b_ref, cast_in_ref, o_ref, cast_out_ref):
    c = cond_ref[...]
    s = (c * _sigmoid(c)).astype(BF16)
    o_ref[...] = jnp.dot(s, w_ref[...].astype(BF16), preferred_element_type=F32) + b_ref[...]
    cast_out_ref[...] = cast_in_ref[...].astype(BF16)


def _adaln(cond, w, b, w_cast):
    rows, d = cond.shape
    n = w.shape[1]
    steps = ADALN_STEPS
    tn = n // steps
    slab = w_cast.shape[0] // steps
    assert n % steps == 0 and tn % HEAD == 0 and w_cast.shape[0] % steps == 0 and slab % BF16_SUBLANES == 0
    return pl.pallas_call(
        _adaln_kernel,
        grid=(steps,),
        in_specs=[pl.BlockSpec((rows, d), lambda j: (0, 0)),
                  pl.BlockSpec((d, tn), lambda j: (0, j)),
                  pl.BlockSpec((1, tn), lambda j: (0, j)),
                  pl.BlockSpec((slab, w_cast.shape[1]), lambda j: (j, 0))],
        out_specs=[pl.BlockSpec((rows, tn), lambda j: (0, j)),
                   pl.BlockSpec((slab, w_cast.shape[1]), lambda j: (j, 0))],
        out_shape=[jax.ShapeDtypeStruct((rows, n), F32), jax.ShapeDtypeStruct(w_cast.shape, BF16)],
        compiler_params=_cparams(("arbitrary",)),
        name="adaln",
    )(cond, w, b, w_cast)


def _rope_tables(seq, scale):
    t = np.arange(seq)
    row = (t // GRID_W).astype(np.float64)
    col = (t % GRID_W).astype(np.float64)
    lane = np.arange(HEAD)
    inv_freq = ROPE_BASE ** (-np.arange(0, ROT_AXIS, 2, dtype=np.float64) / ROT_AXIS)
    freq = inv_freq[lane % (ROT_AXIS // 2)]
    use_col = ((lane % D_QK) // ROT_AXIS) == 1
    pos = np.where(use_col[None, :], col[:, None], row[:, None])
    ang = pos * freq[None, :]
    first_half = (lane % ROT_AXIS) < (ROT_AXIS // 2)
    cos, sin = np.cos(ang), np.sin(ang)
    t1 = np.where(first_half[None, :], -sin, 0.0)
    t2 = np.where(first_half[None, :], 0.0, sin)
    return jnp.asarray((np.stack([cos, t1, t2]) * scale).astype(np.float32))


def _inproj_kernel(*refs, groups, kinds, places, n_rope, n_out, n_cast):
    x_ref, sh_ref, sc_ref, w_ref = refs[:4]
    rope_refs = refs[4:4 + n_rope]
    n_lb = 1 if "logf" in kinds else 0
    lb_ref = refs[4 + n_rope] if n_lb else None
    cast_in = refs[4 + n_rope + n_lb:4 + n_rope + n_lb + n_cast]
    out_refs = refs[-1 - n_out - n_cast:-1 - n_cast]
    cast_out = refs[-1 - n_cast:-1]
    u_scr = refs[-1]
    n_sub = GROUP_W // INPROJ_SUB
    heads_per_sub = INPROJ_SUB // HEAD
    first_logf = kinds.index("logf") if "logf" in kinds else None

    u_scr[...] = (x_ref[...] * (1.0 + sc_ref[0]) + sh_ref[0]).astype(BF16)

    def sub_dot(g, k):
        c0 = groups[g] * GROUP_W + k * INPROJ_SUB
        return jnp.dot(u_scr[...], w_ref[:, c0:c0 + INPROJ_SUB], preferred_element_type=F32)

    def epilogue(g, k, acc, rope_ref):
        kind = kinds[g]
        o_ref = out_refs[places[g][0]]
        c0 = places[g][1] * GROUP_W + k * INPROJ_SUB
        if kind == "rope":
            for h in range(heads_per_sub):
                xh = acc[:, h * HEAD:(h + 1) * HEAD]
                y = (xh * rope_ref[0]
                     + pltpu.roll(xh, HEAD - ROT_AXIS // 2, 1) * rope_ref[1]
                     + pltpu.roll(xh, ROT_AXIS // 2, 1) * rope_ref[2])
                o_ref[:, c0 + h * HEAD:c0 + (h + 1) * HEAD] = y.astype(o_ref.dtype)
        elif kind == "copy":
            o_ref[:, c0:c0 + INPROJ_SUB] = acc.astype(o_ref.dtype)
        elif kind == "silu":
            o_ref[:, c0:c0 + INPROJ_SUB] = (acc * _sigmoid(acc)).astype(o_ref.dtype)
        elif kind == "logf":
            a = lb_ref[g - first_logf][:, k * INPROJ_SUB:(k + 1) * INPROJ_SUB]
            e = jnp.exp(a - jnp.max(a, axis=0, keepdims=True))
            lb = e[0:1] / jnp.sum(e, axis=0, keepdims=True)
            o_ref[:, c0:c0 + INPROJ_SUB] = jnp.log2(lb + (1.0 - lb) * _sigmoid(acc)).astype(o_ref.dtype)
        else:
            raise ValueError(kind)

    rope_of, ri = {}, 0
    for g, kind in enumerate(kinds):
        if kind == "rope":
            rope_of[g] = rope_refs[ri]
            ri += 1
    jobs = [(g, k) for g in range(len(kinds)) for k in range(n_sub)]
    cast_at = {(c * len(jobs)) // n_cast: c for c in range(n_cast)}
    nxt = sub_dot(*jobs[0])
    for idx, (g, k) in enumerate(jobs):
        acc = nxt
        if idx + 1 < len(jobs):
            nxt = sub_dot(*jobs[idx + 1])
        epilogue(g, k, acc, rope_of.get(g))
        if idx in cast_at:
            c = cast_at[idx]
            cast_out[c][...] = cast_in[c][...].astype(BF16)


def _inproj(x2d, sh, sc, w, groups, kinds, outs, *, rows_per_mod, tm, ropes=(), seq=None, lbs=None, casts=(),
            name):
    m, d = x2d.shape
    ng = len(groups)
    assert m % tm == 0 and rows_per_mod % tm == 0 and sum(n for _, n in outs) == ng
    tiles_per_mod = rows_per_mod // tm
    in_specs = [pl.BlockSpec((tm, d), lambda i: (i, 0)),
                pl.BlockSpec((1, 1, d), lambda i: (i // tiles_per_mod, 0, 0)),
                pl.BlockSpec((1, 1, d), lambda i: (i // tiles_per_mod, 0, 0)),
                pl.BlockSpec(w.shape, lambda i: (0, 0), pipeline_mode=pl.Buffered(1))]
    args = [x2d, sh, sc, w]
    for tab in ropes:
        tiles_per_seq = seq // tm
        in_specs.append(pl.BlockSpec((3, tm, HEAD), lambda i: (0, i % tiles_per_seq, 0)))
        args.append(tab)
    if lbs is not None:
        first_logf = kinds.index("logf")
        assert all(k == "logf" for k in kinds[first_logf:]) and lbs.shape[0] == ng - first_logf
        in_specs.append(pl.BlockSpec(lbs.shape, lambda i: (0, 0, 0)))
        args.append(lbs)
    out_specs, out_shape, places = [], [], []
    for o, (dtype, n) in enumerate(outs):
        out_specs.append(pl.BlockSpec((tm, n * GROUP_W), lambda i: (i, 0)))
        out_shape.append(jax.ShapeDtypeStruct((m, n * GROUP_W), dtype))
        places += [(o, p) for p in range(n)]
    steps = m // tm
    for cw in casts:
        slab = cw.shape[0] // steps
        assert cw.shape[0] % steps == 0 and slab % BF16_SUBLANES == 0
        in_specs.append(pl.BlockSpec((slab, cw.shape[1]), lambda i: (i, 0)))
        args.append(cw)
        out_specs.append(pl.BlockSpec((slab, cw.shape[1]), lambda i: (i, 0)))
        out_shape.append(jax.ShapeDtypeStruct(cw.shape, BF16))
    return pl.pallas_call(
        functools.partial(_inproj_kernel, groups=tuple(groups), kinds=tuple(kinds), places=tuple(places),
                          n_rope=len(ropes), n_out=len(outs), n_cast=len(casts)),
        grid=(steps,),
        in_specs=in_specs,
        out_specs=out_specs,
        out_shape=out_shape,
        scratch_shapes=[pltpu.VMEM((tm, d), BF16)],
        compiler_params=_cparams(("arbitrary",)),
        name=name,
    )(*args)


def _attn_kernel(q_ref, k_ref, v_ref, kc_ref, vc_ref, lq1_ref, lk1_ref, lq2_ref, lk2_ref, nw_ref, o_ref,
                 k_scr, v_scr):
    seq, ctx_len = k_ref.shape[0], kc_ref.shape[0]
    n_keys = seq + ctx_len
    seg_len = max(256, -(-n_keys // ATTN_KEY_SEGS) // 256 * 256)
    segs = [(a, min(a + seg_len, n_keys)) for a in range(0, n_keys, seg_len)]

    @pl.when(pl.program_id(2) == 0)
    def _():
        k_scr[0:seq, :] = k_ref[...]
        k_scr[seq:n_keys, :] = kc_ref[...]
        v_scr[0:HEAD, 0:seq] = v_ref[...].astype(F32).T.astype(BF16)
        v_scr[0:HEAD, seq:n_keys] = vc_ref[...].astype(F32).T.astype(BF16)
        v_scr[HEAD:, :] = jnp.ones((v_scr.shape[0] - HEAD, n_keys), BF16)

    lam = (jnp.exp(jnp.sum(lq1_ref[...] * lk1_ref[...], axis=1, keepdims=True))
           - jnp.exp(jnp.sum(lq2_ref[...] * lk2_ref[...], axis=1, keepdims=True)) + LAM_INIT)
    ts = min(ATTN_SUB, q_ref.shape[0])
    lane = lax.broadcasted_iota(jnp.int32, (ts, HEAD), 1)
    zero = jnp.zeros((ts, HEAD), BF16)
    n_sub = q_ref.shape[0] // ts

    def scores(r):
        q = q_ref[r * ts:(r + 1) * ts, :]
        q2 = jnp.concatenate([jnp.where(lane < D_QK, q, zero), jnp.where(lane >= D_QK, q, zero)], axis=0)
        ss = [_dot_nt(k_scr[a:b, :], q2) for a, b in segs]
        m = functools.reduce(jnp.maximum, [jnp.max(s, axis=0, keepdims=True) for s in ss])
        return ss, m

    def probs(ss, m):
        return [jnp.exp2(s - m).astype(BF16) for s in ss]

    def attend(r, ps):
        ox = functools.reduce(lambda x, y: x + y, [jnp.dot(v_scr[:, a:b], p, preferred_element_type=F32)
                                                   for (a, b), p in zip(segs, ps)])
        on = ox[0:HEAD] / ox[HEAD:HEAD + 1]
        o = (on[:, :ts] - lam * on[:, ts:]).T
        ms = jnp.mean(o * o, axis=1, keepdims=True)
        o_ref[r * ts:(r + 1) * ts, :] = (
            o * lax.rsqrt(ms + EPS) * nw_ref[...] * (1.0 - LAM_INIT)).astype(o_ref.dtype)

    st_a, st_b = {}, {}
    for t in range(n_sub + 2):
        if t < n_sub:
            st_a[t] = scores(t)
        if 0 <= t - 1 < n_sub:
            st_b[t - 1] = probs(*st_a.pop(t - 1))
        if 0 <= t - 2 < n_sub:
            attend(t - 2, st_b.pop(t - 2))


def _attention(qkv, kv_ctx, lq1, lk1, lq2, lk2, norm_w, *, batch, seq, ctx_len, tq):
    nq = seq // tq
    vec = lambda n: pl.BlockSpec((1, n), lambda b, h, i: (0, 0))
    return pl.pallas_call(
        _attn_kernel,
        grid=(batch, N_HEADS, nq),
        in_specs=[pl.BlockSpec((tq, HEAD), lambda b, h, i: (b * nq + i, h)),
                  pl.BlockSpec((seq, HEAD), lambda b, h, i: (b, N_HEADS + h)),
                  pl.BlockSpec((seq, HEAD), lambda b, h, i: (b, 2 * N_HEADS + h)),
                  pl.BlockSpec((ctx_len, HEAD), lambda b, h, i: (b, h)),
                  pl.BlockSpec((ctx_len, HEAD), lambda b, h, i: (b, N_HEADS + h)),
                  vec(D_QK), vec(D_QK), vec(D_QK), vec(D_QK), vec(HEAD)],
        out_specs=pl.BlockSpec((tq, HEAD), lambda b, h, i: (b * nq + i, h)),
        out_shape=jax.ShapeDtypeStruct((batch * seq, GROUP_W), BF16),
        scratch_shapes=[pltpu.VMEM((seq + ctx_len, HEAD), BF16),
                        pltpu.VMEM((HEAD + ATTN_ONES_ROWS, seq + ctx_len), BF16)],
        compiler_params=_cparams(("arbitrary", "arbitrary", "arbitrary")),
        name="diff_attn",
    )(qkv, qkv, qkv, kv_ctx, kv_ctx, lq1, lk1, lq2, lk2, norm_w)


def _split3(x):
    hi = x.astype(BF16)
    r1 = x - hi.astype(F32)
    mid = r1.astype(BF16)
    lo = (r1 - mid.astype(F32)).astype(BF16)
    return hi, mid, lo


def _cumsum_rows(tri, x):
    hi, mid, lo = _split3(x)
    return (jnp.dot(tri, hi, preferred_element_type=F32)
            + jnp.dot(tri, mid, preferred_element_type=F32)
            + jnp.dot(tri, lo, preferred_element_type=F32))


def _tri(n, lower):
    r = lax.broadcasted_iota(jnp.int32, (n, n), 0)
    c = lax.broadcasted_iota(jnp.int32, (n, n), 1)
    return jnp.where((r >= c) if lower else (r <= c), 1.0, 0.0).astype(BF16)


def _hgrn_ctx_state(lg2f, v, forward):
    n = lg2f.shape[0]
    b = _cumsum_rows(_tri(n, forward), lg2f)
    b_end = b[n - 1:n] if forward else b[0:1]
    k_hat = ((1.0 - jnp.exp2(lg2f)) * jnp.exp2(b_end - b)).astype(BF16)
    return _dot_tn(v, k_hat)


def _chunk_cumsum(x, forward):
    g, n, c = x.shape
    row = lax.broadcasted_iota(jnp.int32, (1, n, c), 1)
    s = 1
    while s < n:
        if s % F32_SUBLANES == 0:
            z = jnp.zeros((g, s, c), x.dtype)
            shifted = (jnp.concatenate([z, x[:, :n - s]], axis=1) if forward
                       else jnp.concatenate([x[:, s:], z], axis=1))
        else:
            rolled = pltpu.roll(x, s if forward else n - s, 1)
            shifted = jnp.where((row >= s) if forward else (row < n - s), rolled, 0.0)
        x = x + shifted
        s *= 2
    return x


def _hgrn_stream(q_ref, lg_ref, v_ref, rows, state, forward, out_ref):
    L = HG_CHUNK
    n_sub = L // HG_SUB
    G = (rows.stop - rows.start) // L
    x3 = lg_ref[rows, :].reshape(G, L, HEAD)
    b = _chunk_cumsum(x3, forward)
    b_end = b[:, L - 1:L] if forward else b[:, 0:1]
    k = 1.0 - jnp.exp2(x3)
    qf = q_ref[rows, :].reshape(G, L, HEAD).astype(F32)
    q_in = (qf * jnp.exp2(b)).astype(BF16)
    k_hat = (k * jnp.exp2(b_end - b)).astype(BF16)
    e_end = jnp.exp2(b_end)

    refs = [b[:, j * HG_SUB + HG_SUB // 2:j * HG_SUB + HG_SUB // 2 + 1] for j in range(n_sub)]
    ref_own = jnp.concatenate([jnp.broadcast_to(r, (G, HG_SUB, HEAD)) for r in refs], axis=1)
    k_own = (k * jnp.exp2(ref_own - b)).astype(BF16)

    def zero_rows(n):
        return [jnp.zeros((G, n, HEAD), BF16)] if n else []

    qs, ks = [], []
    for j in range(n_sub):
        lo_r, hi_r = j * HG_SUB, (j + 1) * HG_SUB
        live = slice(lo_r, L) if forward else slice(0, hi_r)
        piece = (qf[:, live] * jnp.exp2(b[:, live] - refs[j])).astype(BF16)
        qs.append(jnp.concatenate(zero_rows(lo_r) + [piece] if forward else [piece] + zero_rows(L - hi_r), axis=1))
        ks.append(jnp.concatenate(zero_rows(lo_r) + [k_own[:, lo_r:hi_r]] + zero_rows(L - hi_r), axis=1))
    q_cat = jnp.concatenate(qs, axis=2)
    k_cat = jnp.concatenate(ks, axis=2)
    v3 = v_ref[rows, :].reshape(G, L, HEAD)
    yield
    a_raw = [_dot_nt(q_cat[c], k_cat[c]) for c in range(G)]
    ds = [_dot_tn(v3[c], k_hat[c]) for c in range(G)]
    yield
    ar = lax.broadcasted_iota(jnp.int32, (L, L), 0)
    ac = lax.broadcasted_iota(jnp.int32, (L, L), 1)
    causal = (ar >= ac) if forward else (ar <= ac)
    o_intra = [jnp.dot(jnp.where(causal, a_raw[c], 0.0).astype(BF16), v3[c], preferred_element_type=F32)
               for c in range(G)]
    yield
    p = state[forward]
    p_before = [None] * G
    for c in (range(G) if forward else reversed(range(G))):
        p_before[c] = p.astype(BF16)
        p = p * e_end[c] + ds[c]
    state[forward] = p
    yield
    out_ref[rows, :] = jnp.concatenate([o_intra[c] + _dot_nt(q_in[c], p_before[c]) for c in range(G)], axis=0)


def _advance(gens):
    for gen in list(gens):
        if next(gen, "done") == "done":
            gens.remove(gen)


def _hgrn_head(q_ref, lf_ref, lb_ref, v_ref, g_ref, lfc_ref, lbc_ref, vc_ref, nw_ref, o_ref, of_scr, ob_scr,
               lanes):
    seq = q_ref.shape[0]
    rows = min(HG_GROUP * HG_CHUNK, seq)
    ng = seq // rows
    vc = vc_ref[:, lanes]
    state = {True: _hgrn_ctx_state(lfc_ref[:, lanes], vc, True),
             False: _hgrn_ctx_state(lbc_ref[:, lanes], vc, False)}
    yield
    q_h, lf_h, lb_h, v_h = q_ref.at[:, lanes], lf_ref.at[:, lanes], lb_ref.at[:, lanes], v_ref.at[:, lanes]
    of_h, ob_h = of_scr.at[:, lanes], ob_scr.at[:, lanes]
    streams = []
    for g in range(ng):
        rf = slice(g * rows, (g + 1) * rows)
        rb = slice((ng - 1 - g) * rows, (ng - g) * rows)
        streams.append(_hgrn_stream(q_h, lf_h, v_h, rf, state, True, of_h))
        streams.append(_hgrn_stream(q_h, lb_h, v_h, rb, state, False, ob_h))
    live = []
    while streams or live:
        if streams:
            live.append(streams.pop(0))
        _advance(live)
        yield
    o = of_h[...] + ob_h[...]
    ms = jnp.mean(o * o, axis=1, keepdims=True)
    o_ref[:, lanes] = (o * lax.rsqrt(ms + EPS) * nw_ref[...] * g_ref[:, lanes].astype(F32)).astype(o_ref.dtype)


def _hgrn_kernel(q_ref, lf_ref, lb_ref, v_ref, g_ref, lfc_ref, lbc_ref, vc_ref, nw_ref, o_ref, of_scr, ob_scr):
    n_heads = q_ref.shape[1] // HEAD
    heads = [_hgrn_head(q_ref, lf_ref, lb_ref, v_ref, g_ref, lfc_ref, lbc_ref, vc_ref, nw_ref, o_ref,
                        of_scr, ob_scr, slice(h * HEAD, (h + 1) * HEAD)) for h in range(n_heads)]
    live, tick = [], 0
    while heads or live:
        if heads and tick % HG_HEAD_LAG == 0:
            live.append(heads.pop(0))
        _advance(live)
        tick += 1


def _hgrn(hq, logf, hq_ctx, logf_ctx, norm_w, *, batch, seq, ctx_len):
    w = HG_HEADS_PER_STEP * HEAD
    nb = GROUP_W // w
    return pl.pallas_call(
        _hgrn_kernel,
        grid=(batch, nb),
        in_specs=[pl.BlockSpec((seq, w), lambda b, h: (b, h)),
                  pl.BlockSpec((seq, w), lambda b, h: (b, h)),
                  pl.BlockSpec((seq, w), lambda b, h: (b, nb + h)),
                  pl.BlockSpec((seq, w), lambda b, h: (b, nb + h)),
                  pl.BlockSpec((seq, w), lambda b, h: (b, 2 * nb + h)),
                  pl.BlockSpec((ctx_len, w), lambda b, h: (b, h)),
                  pl.BlockSpec((ctx_len, w), lambda b, h: (b, nb + h)),
                  pl.BlockSpec((ctx_len, w), lambda b, h: (b, 2 * nb + h)),
                  pl.BlockSpec((1, HEAD), lambda b, h: (0, 0))],
        out_specs=pl.BlockSpec((seq, w), lambda b, h: (b, h)),
        out_shape=jax.ShapeDtypeStruct((batch * seq, GROUP_W), BF16),
        scratch_shapes=[pltpu.VMEM((seq, w), F32), pltpu.VMEM((seq, w), F32)],
        compiler_params=_cparams(("arbitrary", "arbitrary")),
        name="hgrn2",
    )(hq, logf, logf, hq, hq, logf_ctx, logf_ctx, hq_ctx, norm_w)


def _layer_norm(z, g, b):
    mu = jnp.mean(z, axis=1, keepdims=True)
    zc = z - mu
    var = jnp.mean(zc * zc, axis=1, keepdims=True)
    return zc * lax.rsqrt(var + EPS) * g + b


def _outproj_kernel(oa_ref, oh_ref, x_ref, w_ref, gt_ref, zc_ref, rstd_ref):
    y = (jnp.dot(oa_ref[...], w_ref[:GROUP_W, :], preferred_element_type=F32)
         + jnp.dot(oh_ref[...], w_ref[GROUP_W:, :], preferred_element_type=F32))
    z = ALPHA * x_ref[...] + gt_ref[0] * y
    zc = z - jnp.mean(z, axis=1, keepdims=True)
    zc_ref[...] = zc
    rstd_ref[...] = lax.rsqrt(jnp.mean(zc * zc, axis=1, keepdims=True) + EPS)


def _outproj(o_da, o_hg, x2d, w_out, gt1, *, seq, tm):
    m, d = x2d.shape
    tiles_per_seq = seq // tm
    return pl.pallas_call(
        _outproj_kernel,
        grid=(m // tm,),
        in_specs=[pl.BlockSpec((tm, GROUP_W), lambda i: (i, 0)),
                  pl.BlockSpec((tm, GROUP_W), lambda i: (i, 0)),
                  pl.BlockSpec((tm, d), lambda i: (i, 0)),
                  pl.BlockSpec((d, d), lambda i: (0, 0), pipeline_mode=pl.Buffered(1)),
                  pl.BlockSpec((1, 1, d), lambda i: (i // tiles_per_seq, 0, 0))],
        out_specs=[pl.BlockSpec((tm, d), lambda i: (i, 0)), pl.BlockSpec((tm, 1), lambda i: (i, 0))],
        out_shape=[jax.ShapeDtypeStruct((m, d), F32), jax.ShapeDtypeStruct((m, 1), F32)],
        compiler_params=_cparams(("arbitrary",)),
        name="outproj_ln1",
    )(o_da, o_hg, x2d, w_out, gt1)


def _ffn_kernel(zc_ref, rstd_ref, w1_ref, w2_ref, sh_ref, sc_ref, gt_ref, g1_ref, b1_ref, g2_ref, b2_ref,
                o_ref, acc_scr, h_scr, u_scr):
    j = pl.program_id(1)
    last = pl.num_programs(1) - 1
    d = zc_ref.shape[1]

    def partial_sum():
        a = jnp.maximum(jnp.dot(u_scr[...], w1_ref[...], preferred_element_type=F32), 0.0)
        return jnp.dot((a * a).astype(BF16), w2_ref[...], preferred_element_type=F32)

    @pl.when(j == 0)
    def _():
        rstd = rstd_ref[...]
        for k in range(d // FFN_NORM_SUB):
            cols = slice(k * FFN_NORM_SUB, (k + 1) * FFN_NORM_SUB)
            h = zc_ref[:, cols] * rstd * g1_ref[:, cols] + b1_ref[:, cols]
            h_scr[:, cols] = h
            u_scr[:, cols] = (h * (1.0 + sc_ref[0][:, cols]) + sh_ref[0][:, cols]).astype(BF16)
        acc_scr[...] = partial_sum()

    @pl.when((j > 0) & (j < last))
    def _():
        acc_scr[...] += partial_sum()

    @pl.when(j == last)
    def _():
        ff = acc_scr[...] + partial_sum()
        o_ref[...] = _layer_norm(ALPHA * h_scr[...] + gt_ref[0] * ff, g2_ref[...], b2_ref[...])


def _ffn(zc, rstd, w1, w2, sh2, sc2, gt2, ln1_g, ln1_b, ln2_g, ln2_b, *, seq, tm, tf):
    m, d = zc.shape
    dff = w1.shape[1]
    assert dff // tf >= 2
    tiles_per_seq = seq // tm
    mod = pl.BlockSpec((1, 1, d), lambda i, j: (i // tiles_per_seq, 0, 0))
    vec = pl.BlockSpec((1, d), lambda i, j: (0, 0))
    return pl.pallas_call(
        _ffn_kernel,
        grid=(m // tm, dff // tf),
        in_specs=[pl.BlockSpec((tm, d), lambda i, j: (i, 0)),
                  pl.BlockSpec((tm, 1), lambda i, j: (i, 0)),
                  pl.BlockSpec((d, tf), lambda i, j: (0, j)),
                  pl.BlockSpec((tf, d), lambda i, j: (j, 0)),
                  mod, mod, mod, vec, vec, vec, vec],
        out_specs=pl.BlockSpec((tm, d), lambda i, j: (i, 0)),
        out_shape=jax.ShapeDtypeStruct((m, d), F32),
        scratch_shapes=[pltpu.VMEM((tm, d), F32), pltpu.VMEM((tm, d), F32), pltpu.VMEM((tm, d), BF16)],
        compiler_params=_cparams(("arbitrary", "arbitrary")),
        name="ffn_ln2",
    )(zc, rstd, w1, w2, sh2, sc2, gt2, ln1_g, ln1_b, ln2_g, ln2_b)


def kernel(x, c, ctx, c_ctx, w_ada, b_ada, w_in, lam_q1, lam_k1, lam_q2, lam_k2, da_norm_w,
           hg_lb_fwd, hg_lb_bwd, hg_norm_w, w_out, ln1_g, ln1_b, w_ff1, w_ff2, ln2_g, ln2_b):
    batch, seq, d = x.shape
    ctx_len = ctx.shape[1]
    assert d == D_MODEL and w_in.shape[0] == DEPTH and seq % GRID_W == 0
    l = 0

    n_cond = batch + 1
    pad = (-n_cond) % 8
    cond = jnp.concatenate([c, c_ctx[None], jnp.zeros((pad, d), F32)], axis=0)
    mod, w_in_b = _adaln(cond, w_ada[l], b_ada[l][None], w_in[l])
    mod = mod.reshape(n_cond + pad, N_ADA, 1, d)
    sh1, sc1, gt1, sh2, sc2, gt2 = [mod[:batch, j] for j in range(N_ADA)]
    csh1, csc1 = mod[batch:batch + 1, 0], mod[batch:batch + 1, 1]

    x2d = x.reshape(batch * seq, d)
    ctx2d = ctx.reshape(batch * ctx_len, d)
    lbs = jnp.stack([hg_lb_fwd.reshape(DEPTH + 1, GROUP_W), hg_lb_bwd.reshape(DEPTH + 1, GROUP_W)])

    tm = min(512, seq)
    tmi = min(256, seq)
    tmc = min(256, ctx_len)
    rope_q = _rope_tables(seq, D_QK ** -0.5 * LOG2E)
    rope_k = _rope_tables(seq, 1.0)
    qkv, hq, logf, w_out_b, w_ff1_b, w_ff2_b = _inproj(
        x2d, sh1, sc1, w_in_b, (0, 1, 2, 3, 6, 7, 4, 5),
        ("rope", "rope", "copy", "silu", "copy", "silu", "logf", "logf"),
        ((BF16, 3), (BF16, 3), (F32, 2)),
        rows_per_mod=seq, tm=tmi, ropes=(rope_q, rope_k), seq=seq, lbs=lbs,
        casts=(w_out[l], w_ff1[l], w_ff2[l]), name="inproj")
    m_ctx = batch * ctx_len
    kvh_ctx, logf_ctx = _inproj(
        ctx2d, csh1, csc1, w_in_b, (1, 2, 6, 4, 5), ("copy", "copy", "copy", "logf", "logf"),
        ((BF16, 3), (F32, 2)), rows_per_mod=m_ctx, tm=tmc, lbs=lbs, name="inproj_ctx")

    o_da = _attention(qkv, kvh_ctx, lam_q1[l][None], lam_k1[l][None], lam_q2[l][None], lam_k2[l][None],
                      da_norm_w[l][None], batch=batch, seq=seq, ctx_len=ctx_len, tq=min(2048, seq))
    o_hg = _hgrn(hq, logf, kvh_ctx, logf_ctx, hg_norm_w[l][None], batch=batch, seq=seq, ctx_len=ctx_len)

    zc, rstd = _outproj(o_da, o_hg, x2d, w_out_b, gt1, seq=seq, tm=min(256, seq))
    out = _ffn(zc, rstd, w_ff1_b, w_ff2_b, sh2, sc2, gt2, ln1_g[l][None], ln1_b[l][None],
               ln2_g[l][None], ln2_b[l][None], seq=seq, tm=tm, tf=1024)
    return out.reshape(batch, seq, d)
```

```python
import functools
import math

import jax
import jax.numpy as jnp
import numpy as np
from jax import lax
from jax.experimental import pallas as pl
from jax.experimental.pallas import tpu as pltpu

F32 = jnp.float32
BF16 = jnp.bfloat16

D_MODEL = 2048
GRID_W = 64
HEAD = 128
N_HEADS = 8
GROUP_W = N_HEADS * HEAD
N_GROUPS = 8
D_QK = 64
ROT_AXIS = 32
ROPE_BASE = 10000.0
D_FF = 4 * D_MODEL
N_ADA = 6
EPS = 1e-5
DEPTH = 1
ALPHA = (2.0 * DEPTH) ** 0.25
LAM_INIT = 0.8 - 0.6 * math.exp(-0.3 * 0)
HG_CHUNK = 64
HG_SUB = 16
HG_GROUP = 8
HG_HEADS_PER_STEP = 2
HG_HEAD_LAG = 7
ATTN_SUB = 256
ADALN_STEPS = 8
ATTN_KEY_SEGS = 3
ATTN_ONES_ROWS = 16
FFN_NORM_SUB = 256
INPROJ_SUB = 256
LOG2E = 1.4426950408889634
BF16_SUBLANES = 16
F32_SUBLANES = 8

VMEM_LIMIT = 56 * 1024 * 1024


def _cparams(sem):
    return pltpu.CompilerParams(dimension_semantics=sem, vmem_limit_bytes=VMEM_LIMIT)


def _sigmoid(z):
    return 1.0 / (1.0 + jnp.exp(-z))


def _dot_nt(a, b):
    return lax.dot_general(a, b, (((1,), (1,)), ((), ())), preferred_element_type=F32)


def _dot_tn(a, b):
    return lax.dot_general(a, b, (((0,), (0,)), ((), ())), preferred_element_type=F32)


def _adaln_kernel(cond_ref, w_ref, b_ref, cast_in_ref, o_ref, cast_out_ref):
    c = cond_ref[...]
    s = (c * _sigmoid(c)).astype(BF16)
    o_ref[...] = jnp.dot(s, w_ref[...].astype(BF16), preferred_element_type=F32) + b_ref[...]
    cast_out_ref[...] = cast_in_ref[...].astype(BF16)


def _adaln(cond, w, b, w_cast):
    rows, d = cond.shape
    n = w.shape[1]
    steps = ADALN_STEPS
    tn = n // steps
    slab = w_cast.shape[0] // steps
    assert n % steps == 0 and tn % HEAD == 0 and w_cast.shape[0] % steps == 0 and slab % BF16_SUBLANES == 0
    return pl.pallas_call(
        _adaln_kernel,
        grid=(steps,),
        in_specs=[pl.BlockSpec((rows, d), lambda j: (0, 0)),
                  pl.BlockSpec((d, tn), lambda j: (0, j)),
                  pl.BlockSpec((1, tn), lambda j: (0, j)),
                  pl.BlockSpec((slab, w_cast.shape[1]), lambda j: (j, 0))],
        out_specs=[pl.BlockSpec((rows, tn), lambda j: (0, j)),
                   pl.BlockSpec((slab, w_cast.shape[1]), lambda j: (j, 0))],
        out_shape=[jax.ShapeDtypeStruct((rows, n), F32), jax.ShapeDtypeStruct(w_cast.shape, BF16)],
        compiler_params=_cparams(("arbitrary",)),
        name="adaln",
    )(cond, w, b, w_cast)


def _rope_tables(seq, scale):
    t = np.arange(seq)
    row = (t // GRID_W).astype(np.float64)
    col = (t % GRID_W).astype(np.float64)
    lane = np.arange(HEAD)
    inv_freq = ROPE_BASE ** (-np.arange(0, ROT_AXIS, 2, dtype=np.float64) / ROT_AXIS)
    freq = inv_freq[lane % (ROT_AXIS // 2)]
    use_col = ((lane % D_QK) // ROT_AXIS) == 1
    pos = np.where(use_col[None, :], col[:, None], row[:, None])
    ang = pos * freq[None, :]
    first_half = (lane % ROT_AXIS) < (ROT_AXIS // 2)
    cos, sin = np.cos(ang), np.sin(ang)
    t1 = np.where(first_half[None, :], -sin, 0.0)
    t2 = np.where(first_half[None, :], 0.0, sin)
    return jnp.asarray((np.stack([cos, t1, t2]) * scale).astype(np.float32))


def _inproj_kernel(*refs, groups, kinds, places, n_rope, n_out, n_cast):
    x_ref, sh_ref, sc_ref, w_ref = refs[:4]
    rope_refs = refs[4:4 + n_rope]
    n_lb = 1 if "logf" in kinds else 0
    lb_ref = refs[4 + n_rope] if n_lb else None
    cast_in = refs[4 + n_rope + n_lb:4 + n_rope + n_lb + n_cast]
    out_refs = refs[-1 - n_out - n_cast:-1 - n_cast]
    cast_out = refs[-1 - n_cast:-1]
    u_scr = refs[-1]
    n_sub = GROUP_W // INPROJ_SUB
    heads_per_sub = INPROJ_SUB // HEAD
    first_logf = kinds.index("logf") if "logf" in kinds else None

    u_scr[...] = (x_ref[...] * (1.0 + sc_ref[0]) + sh_ref[0]).astype(BF16)

    def sub_dot(g, k):
        c0 = groups[g] * GROUP_W + k * INPROJ_SUB
        return jnp.dot(u_scr[...], w_ref[:, c0:c0 + INPROJ_SUB], preferred_element_type=F32)

    def epilogue(g, k, acc, rope_ref):
        kind = kinds[g]
        o_ref = out_refs[places[g][0]]
        c0 = places[g][1] * GROUP_W + k * INPROJ_SUB
        if kind == "rope":
            for h in range(heads_per_sub):
                xh = acc[:, h * HEAD:(h + 1) * HEAD]
                y = (xh * rope_ref[0]
                     + pltpu.roll(xh, HEAD - ROT_AXIS // 2, 1) * rope_ref[1]
                     + pltpu.roll(xh, ROT_AXIS // 2, 1) * rope_ref[2])
                o_ref[:, c0 + h * HEAD:c0 + (h + 1) * HEAD] = y.astype(o_ref.dtype)
        elif kind == "copy":
            o_ref[:, c0:c0 + INPROJ_SUB] = acc.astype(o_ref.dtype)
        elif kind == "silu":
            o_ref[:, c0:c0 + INPROJ_SUB] = (acc * _sigmoid(acc)).astype(o_ref.dtype)
        elif kind == "logf":
            a = lb_ref[g - first_logf][:, k * INPROJ_SUB:(k + 1) * INPROJ_SUB]
            e = jnp.exp(a - jnp.max(a, axis=0, keepdims=True))
            lb = e[0:1] / jnp.sum(e, axis=0, keepdims=True)
            o_ref[:, c0:c0 + INPROJ_SUB] = jnp.log2(lb + (1.0 - lb) * _sigmoid(acc)).astype(o_ref.dtype)
        else:
            raise ValueError(kind)

    rope_of, ri = {}, 0
    for g, kind in enumerate(kinds):
        if kind == "rope":
            rope_of[g] = rope_refs[ri]
            ri += 1
    jobs = [(g, k) for g in range(len(kinds)) for k in range(n_sub)]
    cast_at = {(c * len(jobs)) // n_cast: c for c in range(n_cast)}
    nxt = sub_dot(*jobs[0])
    for idx, (g, k) in enumerate(jobs):
        acc = nxt
        if idx + 1 < len(jobs):
            nxt = sub_dot(*jobs[idx + 1])
        epilogue(g, k, acc, rope_of.get(g))
        if idx in cast_at:
            c = cast_at[idx]
            cast_out[c][...] = cast_in[c][...].astype(BF16)


def _inproj(x2d, sh, sc, w, groups, kinds, outs, *, rows_per_mod, tm, ropes=(), seq=None, lbs=None, casts=(),
            name):
    m, d = x2d.shape
    ng = len(groups)
    assert m % tm == 0 and rows_per_mod % tm == 0 and sum(n for _, n in outs) == ng
    tiles_per_mod = rows_per_mod // tm
    in_specs = [pl.BlockSpec((tm, d), lambda i: (i, 0)),
                pl.BlockSpec((1, 1, d), lambda i: (i // tiles_per_mod, 0, 0)),
                pl.BlockSpec((1, 1, d), lambda i: (i // tiles_per_mod, 0, 0)),
                pl.BlockSpec(w.shape, lambda i: (0, 0), pipeline_mode=pl.Buffered(1))]
    args = [x2d, sh, sc, w]
    for tab in ropes:
        tiles_per_seq = seq // tm
        in_specs.append(pl.BlockSpec((3, tm, HEAD), lambda i: (0, i % tiles_per_seq, 0)))
        args.append(tab)
    if lbs is not None:
        first_logf = kinds.index("logf")
        assert all(k == "logf" for k in kinds[first_logf:]) and lbs.shape[0] == ng - first_logf
        in_specs.append(pl.BlockSpec(lbs.shape, lambda i: (0, 0, 0)))
        args.append(lbs)
    out_specs, out_shape, places = [], [], []
    for o, (dtype, n) in enumerate(outs):
        out_specs.append(pl.BlockSpec((tm, n * GROUP_W), lambda i: (i, 0)))
        out_shape.append(jax.ShapeDtypeStruct((m, n * GROUP_W), dtype))
        places += [(o, p) for p in range(n)]
    steps = m // tm
    for cw in casts:
        slab = cw.shape[0] // steps
        assert cw.shape[0] % steps == 0 and slab % BF16_SUBLANES == 0
        in_specs.append(pl.BlockSpec((slab, cw.shape[1]), lambda i: (i, 0)))
        args.append(cw)
        out_specs.append(pl.BlockSpec((slab, cw.shape[1]), lambda i: (i, 0)))
        out_shape.append(jax.ShapeDtypeStruct(cw.shape, BF16))
    return pl.pallas_call(
        functools.partial(_inproj_kernel, groups=tuple(groups), kinds=tuple(kinds), places=tuple(places),
                          n_rope=len(ropes), n_out=len(outs), n_cast=len(casts)),
        grid=(steps,),
        in_specs=in_specs,
        out_specs=out_specs,
        out_shape=out_shape,
        scratch_shapes=[pltpu.VMEM((tm, d), BF16)],
        compiler_params=_cparams(("arbitrary",)),
        name=name,
    )(*args)


def _attn_kernel(q_ref, k_ref, v_ref, kc_ref, vc_ref, lq1_ref, lk1_ref, lq2_ref, lk2_ref, nw_ref, o_ref,
                 k_scr, v_scr):
    seq, ctx_len = k_ref.shape[0], kc_ref.shape[0]
    n_keys = seq + ctx_len
    seg_len = max(256, -(-n_keys // ATTN_KEY_SEGS) // 256 * 256)
    segs = [(a, min(a + seg_len, n_keys)) for a in range(0, n_keys, seg_len)]

    @pl.when(pl.program_id(2) == 0)
    def _():
        k_scr[0:seq, :] = k_ref[...]
        k_scr[seq:n_keys, :] = kc_ref[...]
        v_scr[0:HEAD, 0:seq] = v_ref[...].astype(F32).T.astype(BF16)
        v_scr[0:HEAD, seq:n_keys] = vc_ref[...].astype(F32).T.astype(BF16)
        v_scr[HEAD:, :] = jnp.ones((v_scr.shape[0] - HEAD, n_keys), BF16)

    lam = (jnp.exp(jnp.sum(lq1_ref[...] * lk1_ref[...], axis=1, keepdims=True))
           - jnp.exp(jnp.sum(lq2_ref[...] * lk2_ref[...], axis=1, keepdims=True)) + LAM_INIT)
    ts = min(ATTN_SUB, q_ref.shape[0])
    lane = lax.broadcasted_iota(jnp.int32, (ts, HEAD), 1)
    zero = jnp.zeros((ts, HEAD), BF16)
    n_sub = q_ref.shape[0] // ts

    def scores(r):
        q = q_ref[r * ts:(r + 1) * ts, :]
        q2 = jnp.concatenate([jnp.where(lane < D_QK, q, zero), jnp.where(lane >= D_QK, q, zero)], axis=0)
        ss = [_dot_nt(k_scr[a:b, :], q2) for a, b in segs]
        m = functools.reduce(jnp.maximum, [jnp.max(s, axis=0, keepdims=True) for s in ss])
        return ss, m

    def probs(ss, m):
        return [jnp.exp2(s - m).astype(BF16) for s in ss]

    def attend(r, ps):
        ox = functools.reduce(lambda x, y: x + y, [jnp.dot(v_scr[:, a:b], p, preferred_element_type=F32)
                                                   for (a, b), p in zip(segs, ps)])
        on = ox[0:HEAD] / ox[HEAD:HEAD + 1]
        o = (on[:, :ts] - lam * on[:, ts:]).T
        ms = jnp.mean(o * o, axis=1, keepdims=True)
        o_ref[r * ts:(r + 1) * ts, :] = (
            o * lax.rsqrt(ms + EPS) * nw_ref[...] * (1.0 - LAM_INIT)).astype(o_ref.dtype)

    st_a, st_b = {}, {}
    for t in range(n_sub + 2):
        if t < n_sub:
            st_a[t] = scores(t)
        if 0 <= t - 1 < n_sub:
            st_b[t - 1] = probs(*st_a.pop(t - 1))
        if 0 <= t - 2 < n_sub:
            attend(t - 2, st_b.pop(t - 2))


def _attention(qkv, kv_ctx, lq1, lk1, lq2, lk2, norm_w, *, batch, seq, ctx_len, tq):
    nq = seq // tq
    vec = lambda n: pl.BlockSpec((1, n), lambda b, h, i: (0, 0))
    return pl.pallas_call(
        _attn_kernel,
        grid=(batch, N_HEADS, nq),
        in_specs=[pl.BlockSpec((tq, HEAD), lambda b, h, i: (b * nq + i, h)),
                  pl.BlockSpec((seq, HEAD), lambda b, h, i: (b, N_HEADS + h)),
                  pl.BlockSpec((seq, HEAD), lambda b, h, i: (b, 2 * N_HEADS + h)),
                  pl.BlockSpec((ctx_len, HEAD), lambda b, h, i: (b, h)),
                  pl.BlockSpec((ctx_len, HEAD), lambda b, h, i: (b, N_HEADS + h)),
                  vec(D_QK), vec(D_QK), vec(D_QK), vec(D_QK), vec(HEAD)],
        out_specs=pl.BlockSpec((tq, HEAD), lambda b, h, i: (b * nq + i, h)),
        out_shape=jax.ShapeDtypeStruct((batch * seq, GROUP_W), BF16),
        scratch_shapes=[pltpu.VMEM((seq + ctx_len, HEAD), BF16),
                        pltpu.VMEM((HEAD + ATTN_ONES_ROWS, seq + ctx_len), BF16)],
        compiler_params=_cparams(("arbitrary", "arbitrary", "arbitrary")),
        name="diff_attn",
    )(qkv, qkv, qkv, kv_ctx, kv_ctx, lq1, lk1, lq2, lk2, norm_w)


def _split3(x):
    hi = x.astype(BF16)
    r1 = x - hi.astype(F32)
    mid = r1.astype(BF16)
    lo = (r1 - mid.astype(F32)).astype(BF16)
    return hi, mid, lo


def _cumsum_rows(tri, x):
    hi, mid, lo = _split3(x)
    return (jnp.dot(tri, hi, preferred_element_type=F32)
            + jnp.dot(tri, mid, preferred_element_type=F32)
            + jnp.dot(tri, lo, preferred_element_type=F32))


def _tri(n, lower):
    r = lax.broadcasted_iota(jnp.int32, (n, n), 0)
    c = lax.broadcasted_iota(jnp.int32, (n, n), 1)
    return jnp.where((r >= c) if lower else (r <= c), 1.0, 0.0).astype(BF16)


def _hgrn_ctx_state(lg2f, v, forward):
    n = lg2f.shape[0]
    b = _cumsum_rows(_tri(n, forward), lg2f)
    b_end = b[n - 1:n] if forward else b[0:1]
    k_hat = ((1.0 - jnp.exp2(lg2f)) * jnp.exp2(b_end - b)).astype(BF16)
    return _dot_tn(v, k_hat)


def _chunk_cumsum(x, forward):
    g, n, c = x.shape
    row = lax.broadcasted_iota(jnp.int32, (1, n, c), 1)
    s = 1
    while s < n:
        if s % F32_SUBLANES == 0:
            z = jnp.zeros((g, s, c), x.dtype)
            shifted = (jnp.concatenate([z, x[:, :n - s]], axis=1) if forward
                       else jnp.concatenate([x[:, s:], z], axis=1))
        else:
            rolled = pltpu.roll(x, s if forward else n - s, 1)
            shifted = jnp.where((row >= s) if forward else (row < n - s), rolled, 0.0)
        x = x + shifted
        s *= 2
    return x


def _hgrn_stream(q_ref, lg_ref, v_ref, rows, state, forward, out_ref):
    L = HG_CHUNK
    n_sub = L // HG_SUB
    G = (rows.stop - rows.start) // L
    x3 = lg_ref[rows, :].reshape(G, L, HEAD)
    b = _chunk_cumsum(x3, forward)
    b_end = b[:, L - 1:L] if forward else b[:, 0:1]
    k = 1.0 - jnp.exp2(x3)
    qf = q_ref[rows, :].reshape(G, L, HEAD).astype(F32)
    q_in = (qf * jnp.exp2(b)).astype(BF16)
    k_hat = (k * jnp.exp2(b_end - b)).astype(BF16)
    e_end = jnp.exp2(b_end)

    refs = [b[:, j * HG_SUB + HG_SUB // 2:j * HG_SUB + HG_SUB // 2 + 1] for j in range(n_sub)]
    ref_own = jnp.concatenate([jnp.broadcast_to(r, (G, HG_SUB, HEAD)) for r in refs], axis=1)
    k_own = (k * jnp.exp2(ref_own - b)).astype(BF16)

    def zero_rows(n):
        return [jnp.zeros((G, n, HEAD), BF16)] if n else []

    qs, ks = [], []
    for j in range(n_sub):
        lo_r, hi_r = j * HG_SUB, (j + 1) * HG_SUB
        live = slice(lo_r, L) if forward else slice(0, hi_r)
        piece = (qf[:, live] * jnp.exp2(b[:, live] - refs[j])).astype(BF16)
        qs.append(jnp.concatenate(zero_rows(lo_r) + [piece] if forward else [piece] + zero_rows(L - hi_r), axis=1))
        ks.append(jnp.concatenate(zero_rows(lo_r) + [k_own[:, lo_r:hi_r]] + zero_rows(L - hi_r), axis=1))
    q_cat = jnp.concatenate(qs, axis=2)
    k_cat = jnp.concatenate(ks, axis=2)
    v3 = v_ref[rows, :].reshape(G, L, HEAD)
    yield
    a_raw = [_dot_nt(q_cat[c], k_cat[c]) for c in range(G)]
    ds = [_dot_tn(v3[c], k_hat[c]) for c in range(G)]
    yield
    ar = lax.broadcasted_iota(jnp.int32, (L, L), 0)
    ac = lax.broadcasted_iota(jnp.int32, (L, L), 1)
    causal = (ar >= ac) if forward else (ar <= ac)
    o_intra = [jnp.dot(jnp.where(causal, a_raw[c], 0.0).astype(BF16), v3[c], preferred_element_type=F32)
               for c in range(G)]
    yield
    p = state[forward]
    p_before = [None] * G
    for c in (range(G) if forward else reversed(range(G))):
        p_before[c] = p.astype(BF16)
        p = p * e_end[c] + ds[c]
    state[forward] = p
    yield
    out_ref[rows, :] = jnp.concatenate([o_intra[c] + _dot_nt(q_in[c], p_before[c]) for c in range(G)], axis=0)


def _advance(gens):
    for gen in list(gens):
        if next(gen, "done") == "done":
            gens.remove(gen)


def _hgrn_head(q_ref, lf_ref, lb_ref, v_ref, g_ref, lfc_ref, lbc_ref, vc_ref, nw_ref, o_ref, of_scr, ob_scr,
               lanes):
    seq = q_ref.shape[0]
    rows = min(HG_GROUP * HG_CHUNK, seq)
    ng = seq // rows
    vc = vc_ref[:, lanes]
    state = {True: _hgrn_ctx_state(lfc_ref[:, lanes], vc, True),
             False: _hgrn_ctx_state(lbc_ref[:, lanes], vc, False)}
    yield
    q_h, lf_h, lb_h, v_h = q_ref.at[:, lanes], lf_ref.at[:, lanes], lb_ref.at[:, lanes], v_ref.at[:, lanes]
    of_h, ob_h = of_scr.at[:, lanes], ob_scr.at[:, lanes]
    streams = []
    for g in range(ng):
        rf = slice(g * rows, (g + 1) * rows)
        rb = slice((ng - 1 - g) * rows, (ng - g) * rows)
        streams.append(_hgrn_stream(q_h, lf_h, v_h, rf, state, True, of_h))
        streams.append(_hgrn_stream(q_h, lb_h, v_h, rb, state, False, ob_h))
    live = []
    while streams or live:
        if streams:
            live.append(streams.pop(0))
        _advance(live)
        yield
    o = of_h[...] + ob_h[...]
    ms = jnp.mean(o * o, axis=1, keepdims=True)
    o_ref[:, lanes] = (o * lax.rsqrt(ms + EPS) * nw_ref[...] * g_ref[:, lanes].astype(F32)).astype(o_ref.dtype)


def _hgrn_kernel(q_ref, lf_ref, lb_ref, v_ref, g_ref, lfc_ref, lbc_ref, vc_ref, nw_ref, o_ref, of_scr, ob_scr):
    n_heads = q_ref.shape[1] // HEAD
    heads = [_hgrn_head(q_ref, lf_ref, lb_ref, v_ref, g_ref, lfc_ref, lbc_ref, vc_ref, nw_ref, o_ref,
                        of_scr, ob_scr, slice(h * HEAD, (h + 1) * HEAD)) for h in range(n_heads)]
    live, tick = [], 0
    while heads or live:
        if heads and tick % HG_HEAD_LAG == 0:
            live.append(heads.pop(0))
        _advance(live)
        tick += 1


def _hgrn(hq, logf, hq_ctx, logf_ctx, norm_w, *, batch, seq, ctx_len):
    w = HG_HEADS_PER_STEP * HEAD
    nb = GROUP_W // w
    return pl.pallas_call(
        _hgrn_kernel,
        grid=(batch, nb),
        in_specs=[pl.BlockSpec((seq, w), lambda b, h: (b, h)),
                  pl.BlockSpec((seq, w), lambda b, h: (b, h)),
                  pl.BlockSpec((seq, w), lambda b, h: (b, nb + h)),
                  pl.BlockSpec((seq, w), lambda b, h: (b, nb + h)),
                  pl.BlockSpec((seq, w), lambda b, h: (b, 2 * nb + h)),
                  pl.BlockSpec((ctx_len, w), lambda b, h: (b, h)),
                  pl.BlockSpec((ctx_len, w), lambda b, h: (b, nb + h)),
                  pl.BlockSpec((ctx_len, w), lambda b, h: (b, 2 * nb + h)),
                  pl.BlockSpec((1, HEAD), lambda b, h: (0, 0))],
        out_specs=pl.BlockSpec((seq, w), lambda b, h: (b, h)),
        out_shape=jax.ShapeDtypeStruct((batch * seq, GROUP_W), BF16),
        scratch_shapes=[pltpu.VMEM((seq, w), F32), pltpu.VMEM((seq, w), F32)],
        compiler_params=_cparams(("arbitrary", "arbitrary")),
        name="hgrn2",
    )(hq, logf, logf, hq, hq, logf_ctx, logf_ctx, hq_ctx, norm_w)


def _layer_norm(z, g, b):
    mu = jnp.mean(z, axis=1, keepdims=True)
    zc = z - mu
    var = jnp.mean(zc * zc, axis=1, keepdims=True)
    return zc * lax.rsqrt(var + EPS) * g + b


def _outproj_kernel(oa_ref, oh_ref, x_ref, w_ref, gt_ref, zc_ref, rstd_ref):
    y = (jnp.dot(oa_ref[...], w_ref[:GROUP_W, :], preferred_element_type=F32)
         + jnp.dot(oh_ref[...], w_ref[GROUP_W:, :], preferred_element_type=F32))
    z = ALPHA * x_ref[...] + gt_ref[0] * y
    zc = z - jnp.mean(z, axis=1, keepdims=True)
    zc_ref[...] = zc
    rstd_ref[...] = lax.rsqrt(jnp.mean(zc * zc, axis=1, keepdims=True) + EPS)


def _outproj(o_da, o_hg, x2d, w_out, gt1, *, seq, tm):
    m, d = x2d.shape
    tiles_per_seq = seq // tm
    return pl.pallas_call(
        _outproj_kernel,
        grid=(m // tm,),
        in_specs=[pl.BlockSpec((tm, GROUP_W), lambda i: (i, 0)),
                  pl.BlockSpec((tm, GROUP_W), lambda i: (i, 0)),
                  pl.BlockSpec((tm, d), lambda i: (i, 0)),
                  pl.BlockSpec((d, d), lambda i: (0, 0), pipeline_mode=pl.Buffered(1)),
                  pl.BlockSpec((1, 1, d), lambda i: (i // tiles_per_seq, 0, 0))],
        out_specs=[pl.BlockSpec((tm, d), lambda i: (i, 0)), pl.BlockSpec((tm, 1), lambda i: (i, 0))],
        out_shape=[jax.ShapeDtypeStruct((m, d), F32), jax.ShapeDtypeStruct((m, 1), F32)],
        compiler_params=_cparams(("arbitrary",)),
        name="outproj_ln1",
    )(o_da, o_hg, x2d, w_out, gt1)


def _ffn_kernel(zc_ref, rstd_ref, w1_ref, w2_ref, sh_ref, sc_ref, gt_ref, g1_ref, b1_ref, g2_ref, b2_ref,
                o_ref, acc_scr, h_scr, u_scr):
    j = pl.program_id(1)
    last = pl.num_programs(1) - 1
    d = zc_ref.shape[1]

    def partial_sum():
        a = jnp.maximum(jnp.dot(u_scr[...], w1_ref[...], preferred_element_type=F32), 0.0)
        return jnp.dot((a * a).astype(BF16), w2_ref[...], preferred_element_type=F32)

    @pl.when(j == 0)
    def _():
        rstd = rstd_ref[...]
        for k in range(d // FFN_NORM_SUB):
            cols = slice(k * FFN_NORM_SUB, (k + 1) * FFN_NORM_SUB)
            h = zc_ref[:, cols] * rstd * g1_ref[:, cols] + b1_ref[:, cols]
            h_scr[:, cols] = h
            u_scr[:, cols] = (h * (1.0 + sc_ref[0][:, cols]) + sh_ref[0][:, cols]).astype(BF16)
        acc_scr[...] = partial_sum()

    @pl.when((j > 0) & (j < last))
    def _():
        acc_scr[...] += partial_sum()

    @pl.when(j == last)
    def _():
        ff = acc_scr[...] + partial_sum()
        o_ref[...] = _layer_norm(ALPHA * h_scr[...] + gt_ref[0] * ff, g2_ref[...], b2_ref[...])


def _ffn(zc, rstd, w1, w2, sh2, sc2, gt2, ln1_g, ln1_b, ln2_g, ln2_b, *, seq, tm, tf):
    m, d = zc.shape
    dff = w1.shape[1]
    assert dff // tf >= 2
    tiles_per_seq = seq // tm
    mod = pl.BlockSpec((1, 1, d), lambda i, j: (i // tiles_per_seq, 0, 0))
    vec = pl.BlockSpec((1, d), lambda i, j: (0, 0))
    return pl.pallas_call(
        _ffn_kernel,
        grid=(m // tm, dff // tf),
        in_specs=[pl.BlockSpec((tm, d), lambda i, j: (i, 0)),
                  pl.BlockSpec((tm, 1), lambda i, j: (i, 0)),
                  pl.BlockSpec((d, tf), lambda i, j: (0, j)),
                  pl.BlockSpec((tf, d), lambda i, j: (j, 0)),
                  mod, mod, mod, vec, vec, vec, vec],
        out_specs=pl.BlockSpec((tm, d), lambda i, j: (i, 0)),
        out_shape=jax.ShapeDtypeStruct((m, d), F32),
        scratch_shapes=[pltpu.VMEM((tm, d), F32), pltpu.VMEM((tm, d), F32), pltpu.VMEM((tm, d), BF16)],
        compiler_params=_cparams(("arbitrary", "arbitrary")),
        name="ffn_ln2",
    )(zc, rstd, w1, w2, sh2, sc2, gt2, ln1_g, ln1_b, ln2_g, ln2_b)


def kernel(x, c, ctx, c_ctx, w_ada, b_ada, w_in, lam_q1, lam_k1, lam_q2, lam_k2, da_norm_w,
           hg_lb_fwd, hg_lb_bwd, hg_norm_w, w_out, ln1_g, ln1_b, w_ff1, w_ff2, ln2_g, ln2_b):
    batch, seq, d = x.shape
    ctx_len = ctx.shape[1]
    assert d == D_MODEL and w_in.shape[0] == DEPTH and seq % GRID_W == 0
    l = 0

    n_cond = batch + 1
    pad = (-n_cond) % 8
    cond = jnp.concatenate([c, c_ctx[None], jnp.zeros((pad, d), F32)], axis=0)
    mod, w_in_b = _adaln(cond, w_ada[l], b_ada[l][None], w_in[l])
    mod = mod.reshape(n_cond + pad, N_ADA, 1, d)
    sh1, sc1, gt1, sh2, sc2, gt2 = [mod[:batch, j] for j in range(N_ADA)]
    csh1, csc1 = mod[batch:batch + 1, 0], mod[batch:batch + 1, 1]

    x2d = x.reshape(batch * seq, d)
    ctx2d = ctx.reshape(batch * ctx_len, d)
    lbs = jnp.stack([hg_lb_fwd.reshape(DEPTH + 1, GROUP_W), hg_lb_bwd.reshape(DEPTH + 1, GROUP_W)])

    tm = min(512, seq)
    tmi = min(256, seq)
    tmc = min(256, ctx_len)
    rope_q = _rope_tables(seq, D_QK ** -0.5 * LOG2E)
    rope_k = _rope_tables(seq, 1.0)
    qkv, hq, logf, w_out_b, w_ff1_b, w_ff2_b = _inproj(
        x2d, sh1, sc1, w_in_b, (0, 1, 2, 3, 6, 7, 4, 5),
        ("rope", "rope", "copy", "silu", "copy", "silu", "logf", "logf"),
        ((BF16, 3), (BF16, 3), (F32, 2)),
        rows_per_mod=seq, tm=tmi, ropes=(rope_q, rope_k), seq=seq, lbs=lbs,
        casts=(w_out[l], w_ff1[l], w_ff2[l]), name="inproj")
    m_ctx = batch * ctx_len
    kvh_ctx, logf_ctx = _inproj(
        ctx2d, csh1, csc1, w_in_b, (1, 2, 6, 4, 5), ("copy", "copy", "copy", "logf", "logf"),
        ((BF16, 3), (F32, 2)), rows_per_mod=m_ctx, tm=tmc, lbs=lbs, name="inproj_ctx")

    o_da = _attention(qkv, kvh_ctx, lam_q1[l][None], lam_k1[l][None], lam_q2[l][None], lam_k2[l][None],
                      da_norm_w[l][None], batch=batch, seq=seq, ctx_len=ctx_len, tq=min(2048, seq))
    o_hg = _hgrn(hq, logf, kvh_ctx, logf_ctx, hg_norm_w[l][None], batch=batch, seq=seq, ctx_len=ctx_len)

    zc, rstd = _outproj(o_da, o_hg, x2d, w_out_b, gt1, seq=seq, tm=tm)
    out = _ffn(zc, rstd, w_ff1_b, w_ff2_b, sh2, sc2, gt2, ln1_g[l][None], ln1_b[l][None],
               ln2_g[l][None], ln2_b[l][None], seq=seq, tm=tm, tf=1024)
    return out.reshape(batch, seq, d)
```

```python
import functools
import math

import jax
import jax.numpy as jnp
import numpy as np
from jax import lax
from jax.experimental import pallas as pl
from jax.experimental.pallas import tpu as pltpu

F32 = jnp.float32
BF16 = jnp.bfloat16

D_MODEL = 2048
GRID_W = 64
HEAD = 128
N_HEADS = 8
GROUP_W = N_HEADS * HEAD
N_GROUPS = 8
D_QK = 64
ROT_AXIS = 32
ROPE_BASE = 10000.0
D_FF = 4 * D_MODEL
N_ADA = 6
EPS = 1e-5
DEPTH = 1
ALPHA = (2.0 * DEPTH) ** 0.25
LAM_INIT = 0.8 - 0.6 * math.exp(-0.3 * 0)
HG_CHUNK = 64
HG_SUB = 16
HG_GROUP = 8
HG_HEADS_PER_STEP = 2
HG_HEAD_LAG = 7
ATTN_SUB = 256
ADALN_STEPS = 8
ATTN_HEADS_PER_STEP = 2
ATTN_KEY_SEGS = 3
ATTN_ONES_ROWS = 16
FFN_NORM_SUB = 256
INPROJ_SUB = 256
LOG2E = 1.4426950408889634
BF16_SUBLANES = 16
F32_SUBLANES = 8

VMEM_LIMIT = 56 * 1024 * 1024


def _cparams(sem):
    return pltpu.CompilerParams(dimension_semantics=sem, vmem_limit_bytes=VMEM_LIMIT)


def _sigmoid(z):
    return 1.0 / (1.0 + jnp.exp(-z))


def _dot_nt(a, b):
    return lax.dot_general(a, b, (((1,), (1,)), ((), ())), preferred_element_type=F32)


def _dot_tn(a, b):
    return lax.dot_general(a, b, (((0,), (0,)), ((), ())), preferred_element_type=F32)


def _adaln_kernel(cond_ref, w_ref, b_ref, cast_in_ref, o_ref, cast_out_ref):
    c = cond_ref[...]
    s = (c * _sigmoid(c)).astype(BF16)
    o_ref[...] = jnp.dot(s, w_ref[...].astype(BF16), preferred_element_type=F32) + b_ref[...]
    cast_out_ref[...] = cast_in_ref[...].astype(BF16)


def _adaln(cond, w, b, w_cast):
    rows, d = cond.shape
    n = w.shape[1]
    steps = ADALN_STEPS
    tn = n // steps
    slab = w_cast.shape[0] // steps
    assert n % steps == 0 and tn % HEAD == 0 and w_cast.shape[0] % steps == 0 and slab % BF16_SUBLANES == 0
    return pl.pallas_call(
        _adaln_kernel,
        grid=(steps,),
        in_specs=[pl.BlockSpec((rows, d), lambda j: (0, 0)),
                  pl.BlockSpec((d, tn), lambda j: (0, j)),
                  pl.BlockSpec((1, tn), lambda j: (0, j)),
                  pl.BlockSpec((slab, w_cast.shape[1]), lambda j: (j, 0))],
        out_specs=[pl.BlockSpec((rows, tn), lambda j: (0, j)),
                   pl.BlockSpec((slab, w_cast.shape[1]), lambda j: (j, 0))],
        out_shape=[jax.ShapeDtypeStruct((rows, n), F32), jax.ShapeDtypeStruct(w_cast.shape, BF16)],
        compiler_params=_cparams(("arbitrary",)),
        name="adaln",
    )(cond, w, b, w_cast)


def _rope_tables(seq, scale):
    t = np.arange(seq)
    row = (t // GRID_W).astype(np.float64)
    col = (t % GRID_W).astype(np.float64)
    lane = np.arange(HEAD)
    inv_freq = ROPE_BASE ** (-np.arange(0, ROT_AXIS, 2, dtype=np.float64) / ROT_AXIS)
    freq = inv_freq[lane % (ROT_AXIS // 2)]
    use_col = ((lane % D_QK) // ROT_AXIS) == 1
    pos = np.where(use_col[None, :], col[:, None], row[:, None])
    ang = pos * freq[None, :]
    first_half = (lane % ROT_AXIS) < (ROT_AXIS // 2)
    cos, sin = np.cos(ang), np.sin(ang)
    t1 = np.where(first_half[None, :], -sin, 0.0)
    t2 = np.where(first_half[None, :], 0.0, sin)
    return jnp.asarray((np.stack([cos, t1, t2]) * scale).astype(np.float32))


def _inproj_kernel(*refs, groups, kinds, places, n_rope, n_out, n_cast):
    x_ref, sh_ref, sc_ref, w_ref = refs[:4]
    rope_refs = refs[4:4 + n_rope]
    n_lb = 1 if "logf" in kinds else 0
    lb_ref = refs[4 + n_rope] if n_lb else None
    cast_in = refs[4 + n_rope + n_lb:4 + n_rope + n_lb + n_cast]
    out_refs = refs[-1 - n_out - n_cast:-1 - n_cast]
    cast_out = refs[-1 - n_cast:-1]
    u_scr = refs[-1]
    n_sub = GROUP_W // INPROJ_SUB
    heads_per_sub = INPROJ_SUB // HEAD
    first_logf = kinds.index("logf") if "logf" in kinds else None

    u_scr[...] = (x_ref[...] * (1.0 + sc_ref[0]) + sh_ref[0]).astype(BF16)

    def sub_dot(g, k):
        c0 = groups[g] * GROUP_W + k * INPROJ_SUB
        return jnp.dot(u_scr[...], w_ref[:, c0:c0 + INPROJ_SUB], preferred_element_type=F32)

    def epilogue(g, k, acc, rope_ref):
        kind = kinds[g]
        o_ref = out_refs[places[g][0]]
        c0 = places[g][1] * GROUP_W + k * INPROJ_SUB
        if kind == "rope":
            for h in range(heads_per_sub):
                xh = acc[:, h * HEAD:(h + 1) * HEAD]
                y = (xh * rope_ref[0]
                     + pltpu.roll(xh, HEAD - ROT_AXIS // 2, 1) * rope_ref[1]
                     + pltpu.roll(xh, ROT_AXIS // 2, 1) * rope_ref[2])
                o_ref[:, c0 + h * HEAD:c0 + (h + 1) * HEAD] = y.astype(o_ref.dtype)
        elif kind == "copy":
            o_ref[:, c0:c0 + INPROJ_SUB] = acc.astype(o_ref.dtype)
        elif kind == "silu":
            o_ref[:, c0:c0 + INPROJ_SUB] = (acc * _sigmoid(acc)).astype(o_ref.dtype)
        elif kind == "logf":
            a = lb_ref[g - first_logf][:, k * INPROJ_SUB:(k + 1) * INPROJ_SUB]
            e = jnp.exp(a - jnp.max(a, axis=0, keepdims=True))
            lb = e[0:1] / jnp.sum(e, axis=0, keepdims=True)
            o_ref[:, c0:c0 + INPROJ_SUB] = jnp.log2(lb + (1.0 - lb) * _sigmoid(acc)).astype(o_ref.dtype)
        else:
            raise ValueError(kind)

    rope_of, ri = {}, 0
    for g, kind in enumerate(kinds):
        if kind == "rope":
            rope_of[g] = rope_refs[ri]
            ri += 1
    jobs = [(g, k) for g in range(len(kinds)) for k in range(n_sub)]
    cast_at = {(c * len(jobs)) // n_cast: c for c in range(n_cast)}
    nxt = sub_dot(*jobs[0])
    for idx, (g, k) in enumerate(jobs):
        acc = nxt
        if idx + 1 < len(jobs):
            nxt = sub_dot(*jobs[idx + 1])
        epilogue(g, k, acc, rope_of.get(g))
        if idx in cast_at:
            c = cast_at[idx]
            cast_out[c][...] = cast_in[c][...].astype(BF16)


def _inproj(x2d, sh, sc, w, groups, kinds, outs, *, rows_per_mod, tm, ropes=(), seq=None, lbs=None, casts=(),
            name):
    m, d = x2d.shape
    ng = len(groups)
    assert m % tm == 0 and rows_per_mod % tm == 0 and sum(n for _, n in outs) == ng
    tiles_per_mod = rows_per_mod // tm
    in_specs = [pl.BlockSpec((tm, d), lambda i: (i, 0)),
                pl.BlockSpec((1, 1, d), lambda i: (i // tiles_per_mod, 0, 0)),
                pl.BlockSpec((1, 1, d), lambda i: (i // tiles_per_mod, 0, 0)),
                pl.BlockSpec(w.shape, lambda i: (0, 0), pipeline_mode=pl.Buffered(1))]
    args = [x2d, sh, sc, w]
    for tab in ropes:
        tiles_per_seq = seq // tm
        in_specs.append(pl.BlockSpec((3, tm, HEAD), lambda i: (0, i % tiles_per_seq, 0)))
        args.append(tab)
    if lbs is not None:
        first_logf = kinds.index("logf")
        assert all(k == "logf" for k in kinds[first_logf:]) and lbs.shape[0] == ng - first_logf
        in_specs.append(pl.BlockSpec(lbs.shape, lambda i: (0, 0, 0)))
        args.append(lbs)
    out_specs, out_shape, places = [], [], []
    for o, (dtype, n) in enumerate(outs):
        out_specs.append(pl.BlockSpec((tm, n * GROUP_W), lambda i: (i, 0)))
        out_shape.append(jax.ShapeDtypeStruct((m, n * GROUP_W), dtype))
        places += [(o, p) for p in range(n)]
    steps = m // tm
    for cw in casts:
        slab = cw.shape[0] // steps
        assert cw.shape[0] % steps == 0 and slab % BF16_SUBLANES == 0
        in_specs.append(pl.BlockSpec((slab, cw.shape[1]), lambda i: (i, 0)))
        args.append(cw)
        out_specs.append(pl.BlockSpec((slab, cw.shape[1]), lambda i: (i, 0)))
        out_shape.append(jax.ShapeDtypeStruct(cw.shape, BF16))
    return pl.pallas_call(
        functools.partial(_inproj_kernel, groups=tuple(groups), kinds=tuple(kinds), places=tuple(places),
                          n_rope=len(ropes), n_out=len(outs), n_cast=len(casts)),
        grid=(steps,),
        in_specs=in_specs,
        out_specs=out_specs,
        out_shape=out_shape,
        scratch_shapes=[pltpu.VMEM((tm, d), BF16)],
        compiler_params=_cparams(("arbitrary",)),
        name=name,
    )(*args)


def _attn_kernel(q_ref, k_ref, v_ref, kc_ref, vc_ref, lq1_ref, lk1_ref, lq2_ref, lk2_ref, nw_ref, o_ref,
                 k_scr, v_scr):
    seq, ctx_len = k_ref.shape[0], kc_ref.shape[0]
    n_keys = seq + ctx_len
    n_heads = q_ref.shape[1] // HEAD
    seg_len = max(256, -(-n_keys // ATTN_KEY_SEGS) // 256 * 256)
    segs = [(a, min(a + seg_len, n_keys)) for a in range(0, n_keys, seg_len)]

    @pl.when(pl.program_id(2) == 0)
    def _():
        for h in range(n_heads):
            hl = slice(h * HEAD, (h + 1) * HEAD)
            k_scr[h, 0:seq, :] = k_ref[:, hl]
            k_scr[h, seq:n_keys, :] = kc_ref[:, hl]
            v_scr[h, 0:HEAD, 0:seq] = v_ref[:, hl].astype(F32).T.astype(BF16)
            v_scr[h, 0:HEAD, seq:n_keys] = vc_ref[:, hl].astype(F32).T.astype(BF16)
            v_scr[h, HEAD:, :] = jnp.ones((v_scr.shape[1] - HEAD, n_keys), BF16)

    lam = (jnp.exp(jnp.sum(lq1_ref[...] * lk1_ref[...], axis=1, keepdims=True))
           - jnp.exp(jnp.sum(lq2_ref[...] * lk2_ref[...], axis=1, keepdims=True)) + LAM_INIT)
    ts = min(ATTN_SUB, q_ref.shape[0])
    lane = lax.broadcasted_iota(jnp.int32, (ts, HEAD), 1)
    zero = jnp.zeros((ts, HEAD), BF16)
    n_sub = q_ref.shape[0] // ts

    def scores(h, r):
        q = q_ref[r * ts:(r + 1) * ts, h * HEAD:(h + 1) * HEAD]
        q2 = jnp.concatenate([jnp.where(lane < D_QK, q, zero), jnp.where(lane >= D_QK, q, zero)], axis=0)
        ss = [_dot_nt(k_scr[h, a:b, :], q2) for a, b in segs]
        m = functools.reduce(jnp.maximum, [jnp.max(s, axis=0, keepdims=True) for s in ss])
        return ss, m

    def probs(ss, m):
        return [jnp.exp2(s - m).astype(BF16) for s in ss]

    def attend(h, r, ps):
        ox = functools.reduce(lambda x, y: x + y, [jnp.dot(v_scr[h, :, a:b], p, preferred_element_type=F32)
                                                   for (a, b), p in zip(segs, ps)])
        on = ox[0:HEAD] / ox[HEAD:HEAD + 1]
        o = (on[:, :ts] - lam * on[:, ts:]).T
        ms = jnp.mean(o * o, axis=1, keepdims=True)
        o_ref[r * ts:(r + 1) * ts, h * HEAD:(h + 1) * HEAD] = (
            o * lax.rsqrt(ms + EPS) * nw_ref[...] * (1.0 - LAM_INIT)).astype(o_ref.dtype)

    items = [(h, r) for h in range(n_heads) for r in range(n_sub)]
    st_a, st_b = {}, {}
    for t in range(len(items) + 2):
        if t < len(items):
            st_a[t] = scores(*items[t])
        if 0 <= t - 1 < len(items):
            st_b[t - 1] = probs(*st_a.pop(t - 1))
        if 0 <= t - 2 < len(items):
            attend(*items[t - 2], st_b.pop(t - 2))


def _attention(qkv, kv_ctx, lq1, lk1, lq2, lk2, norm_w, *, batch, seq, ctx_len, tq):
    nq = seq // tq
    w = ATTN_HEADS_PER_STEP * HEAD
    nb = GROUP_W // w
    vec = lambda n: pl.BlockSpec((1, n), lambda b, h, i: (0, 0))
    return pl.pallas_call(
        _attn_kernel,
        grid=(batch, nb, nq),
        in_specs=[pl.BlockSpec((tq, w), lambda b, h, i: (b * nq + i, h)),
                  pl.BlockSpec((seq, w), lambda b, h, i: (b, nb + h)),
                  pl.BlockSpec((seq, w), lambda b, h, i: (b, 2 * nb + h)),
                  pl.BlockSpec((ctx_len, w), lambda b, h, i: (b, h)),
                  pl.BlockSpec((ctx_len, w), lambda b, h, i: (b, nb + h)),
                  vec(D_QK), vec(D_QK), vec(D_QK), vec(D_QK), vec(HEAD)],
        out_specs=pl.BlockSpec((tq, w), lambda b, h, i: (b * nq + i, h)),
        out_shape=jax.ShapeDtypeStruct((batch * seq, GROUP_W), BF16),
        scratch_shapes=[pltpu.VMEM((ATTN_HEADS_PER_STEP, seq + ctx_len, HEAD), BF16),
                        pltpu.VMEM((ATTN_HEADS_PER_STEP, HEAD + ATTN_ONES_ROWS, seq + ctx_len), BF16)],
        compiler_params=_cparams(("arbitrary", "arbitrary", "arbitrary")),
        name="diff_attn",
    )(qkv, qkv, qkv, kv_ctx, kv_ctx, lq1, lk1, lq2, lk2, norm_w)


def _split3(x):
    hi = x.astype(BF16)
    r1 = x - hi.astype(F32)
    mid = r1.astype(BF16)
    lo = (r1 - mid.astype(F32)).astype(BF16)
    return hi, mid, lo


def _cumsum_rows(tri, x):
    hi, mid, lo = _split3(x)
    return (jnp.dot(tri, hi, preferred_element_type=F32)
            + jnp.dot(tri, mid, preferred_element_type=F32)
            + jnp.dot(tri, lo, preferred_element_type=F32))


def _tri(n, lower):
    r = lax.broadcasted_iota(jnp.int32, (n, n), 0)
    c = lax.broadcasted_iota(jnp.int32, (n, n), 1)
    return jnp.where((r >= c) if lower else (r <= c), 1.0, 0.0).astype(BF16)


def _hgrn_ctx_state(lg2f, v, forward):
    n = lg2f.shape[0]
    b = _cumsum_rows(_tri(n, forward), lg2f)
    b_end = b[n - 1:n] if forward else b[0:1]
    k_hat = ((1.0 - jnp.exp2(lg2f)) * jnp.exp2(b_end - b)).astype(BF16)
    return _dot_tn(v, k_hat)


def _chunk_cumsum(x, forward):
    g, n, c = x.shape
    row = lax.broadcasted_iota(jnp.int32, (1, n, c), 1)
    s = 1
    while s < n:
        if s % F32_SUBLANES == 0:
            z = jnp.zeros((g, s, c), x.dtype)
            shifted = (jnp.concatenate([z, x[:, :n - s]], axis=1) if forward
                       else jnp.concatenate([x[:, s:], z], axis=1))
        else:
            rolled = pltpu.roll(x, s if forward else n - s, 1)
            shifted = jnp.where((row >= s) if forward else (row < n - s), rolled, 0.0)
        x = x + shifted
        s *= 2
    return x


def _hgrn_stream(q_ref, lg_ref, v_ref, rows, state, forward, out_ref):
    L = HG_CHUNK
    n_sub = L // HG_SUB
    G = (rows.stop - rows.start) // L
    x3 = lg_ref[rows, :].reshape(G, L, HEAD)
    b = _chunk_cumsum(x3, forward)
    b_end = b[:, L - 1:L] if forward else b[:, 0:1]
    k = 1.0 - jnp.exp2(x3)
    qf = q_ref[rows, :].reshape(G, L, HEAD).astype(F32)
    q_in = (qf * jnp.exp2(b)).astype(BF16)
    k_hat = (k * jnp.exp2(b_end - b)).astype(BF16)
    e_end = jnp.exp2(b_end)

    refs = [b[:, j * HG_SUB + HG_SUB // 2:j * HG_SUB + HG_SUB // 2 + 1] for j in range(n_sub)]
    ref_own = jnp.concatenate([jnp.broadcast_to(r, (G, HG_SUB, HEAD)) for r in refs], axis=1)
    k_own = (k * jnp.exp2(ref_own - b)).astype(BF16)

    def zero_rows(n):
        return [jnp.zeros((G, n, HEAD), BF16)] if n else []

    qs, ks = [], []
    for j in range(n_sub):
        lo_r, hi_r = j * HG_SUB, (j + 1) * HG_SUB
        live = slice(lo_r, L) if forward else slice(0, hi_r)
        piece = (qf[:, live] * jnp.exp2(b[:, live] - refs[j])).astype(BF16)
        qs.append(jnp.concatenate(zero_rows(lo_r) + [piece] if forward else [piece] + zero_rows(L - hi_r), axis=1))
        ks.append(jnp.concatenate(zero_rows(lo_r) + [k_own[:, lo_r:hi_r]] + zero_rows(L - hi_r), axis=1))
    q_cat = jnp.concatenate(qs, axis=2)
    k_cat = jnp.concatenate(ks, axis=2)
    v3 = v_ref[rows, :].reshape(G, L, HEAD)
    yield
    a_raw = [_dot_nt(q_cat[c], k_cat[c]) for c in range(G)]
    ds = [_dot_tn(v3[c], k_hat[c]) for c in range(G)]
    yield
    ar = lax.broadcasted_iota(jnp.int32, (L, L), 0)
    ac = lax.broadcasted_iota(jnp.int32, (L, L), 1)
    causal = (ar >= ac) if forward else (ar <= ac)
    o_intra = [jnp.dot(jnp.where(causal, a_raw[c], 0.0).astype(BF16), v3[c], preferred_element_type=F32)
               for c in range(G)]
    yield
    p = state[forward]
    p_before = [None] * G
    for c in (range(G) if forward else reversed(range(G))):
        p_before[c] = p.astype(BF16)
        p = p * e_end[c] + ds[c]
    state[forward] = p
    yield
    out_ref[rows, :] = jnp.concatenate([o_intra[c] + _dot_nt(q_in[c], p_before[c]) for c in range(G)], axis=0)


def _advance(gens):
    for gen in list(gens):
        if next(gen, "done") == "done":
            gens.remove(gen)


def _hgrn_head(q_ref, lf_ref, lb_ref, v_ref, g_ref, lfc_ref, lbc_ref, vc_ref, nw_ref, o_ref, of_scr, ob_scr,
               lanes):
    seq = q_ref.shape[0]
    rows = min(HG_GROUP * HG_CHUNK, seq)
    ng = seq // rows
    vc = vc_ref[:, lanes]
    state = {True: _hgrn_ctx_state(lfc_ref[:, lanes], vc, True),
             False: _hgrn_ctx_state(lbc_ref[:, lanes], vc, False)}
    yield
    q_h, lf_h, lb_h, v_h = q_ref.at[:, lanes], lf_ref.at[:, lanes], lb_ref.at[:, lanes], v_ref.at[:, lanes]
    of_h, ob_h = of_scr.at[:, lanes], ob_scr.at[:, lanes]
    streams = []
    for g in range(ng):
        rf = slice(g * rows, (g + 1) * rows)
        rb = slice((ng - 1 - g) * rows, (ng - g) * rows)
        streams.append(_hgrn_stream(q_h, lf_h, v_h, rf, state, True, of_h))
        streams.append(_hgrn_stream(q_h, lb_h, v_h, rb, state, False, ob_h))
    live = []
    while streams or live:
        if streams:
            live.append(streams.pop(0))
        _advance(live)
        yield
    o = of_h[...] + ob_h[...]
    ms = jnp.mean(o * o, axis=1, keepdims=True)
    o_ref[:, lanes] = (o * lax.rsqrt(ms + EPS) * nw_ref[...] * g_ref[:, lanes].astype(F32)).astype(o_ref.dtype)


def _hgrn_kernel(q_ref, lf_ref, lb_ref, v_ref, g_ref, lfc_ref, lbc_ref, vc_ref, nw_ref, o_ref, of_scr, ob_scr):
    n_heads = q_ref.shape[1] // HEAD
    heads = [_hgrn_head(q_ref, lf_ref, lb_ref, v_ref, g_ref, lfc_ref, lbc_ref, vc_ref, nw_ref, o_ref,
                        of_scr, ob_scr, slice(h * HEAD, (h + 1) * HEAD)) for h in range(n_heads)]
    live, tick = [], 0
    while heads or live:
        if heads and tick % HG_HEAD_LAG == 0:
            live.append(heads.pop(0))
        _advance(live)
        tick += 1


def _hgrn(hq, logf, hq_ctx, logf_ctx, norm_w, *, batch, seq, ctx_len):
    w = HG_HEADS_PER_STEP * HEAD
    nb = GROUP_W // w
    return pl.pallas_call(
        _hgrn_kernel,
        grid=(batch, nb),
        in_specs=[pl.BlockSpec((seq, w), lambda b, h: (b, h)),
                  pl.BlockSpec((seq, w), lambda b, h: (b, h)),
                  pl.BlockSpec((seq, w), lambda b, h: (b, nb + h)),
                  pl.BlockSpec((seq, w), lambda b, h: (b, nb + h)),
                  pl.BlockSpec((seq, w), lambda b, h: (b, 2 * nb + h)),
                  pl.BlockSpec((ctx_len, w), lambda b, h: (b, h)),
                  pl.BlockSpec((ctx_len, w), lambda b, h: (b, nb + h)),
                  pl.BlockSpec((ctx_len, w), lambda b, h: (b, 2 * nb + h)),
                  pl.BlockSpec((1, HEAD), lambda b, h: (0, 0))],
        out_specs=pl.BlockSpec((seq, w), lambda b, h: (b, h)),
        out_shape=jax.ShapeDtypeStruct((batch * seq, GROUP_W), BF16),
        scratch_shapes=[pltpu.VMEM((seq, w), F32), pltpu.VMEM((seq, w), F32)],
        compiler_params=_cparams(("arbitrary", "arbitrary")),
        name="hgrn2",
    )(hq, logf, logf, hq, hq, logf_ctx, logf_ctx, hq_ctx, norm_w)


def _layer_norm(z, g, b):
    mu = jnp.mean(z, axis=1, keepdims=True)
    zc = z - mu
    var = jnp.mean(zc * zc, axis=1, keepdims=True)
    return zc * lax.rsqrt(var + EPS) * g + b


def _outproj_kernel(oa_ref, oh_ref, x_ref, w_ref, gt_ref, zc_ref, rstd_ref):
    y = (jnp.dot(oa_ref[...], w_ref[:GROUP_W, :], preferred_element_type=F32)
         + jnp.dot(oh_ref[...], w_ref[GROUP_W:, :], preferred_element_type=F32))
    z = ALPHA * x_ref[...] + gt_ref[0] * y
    zc = z - jnp.mean(z, axis=1, keepdims=True)
    zc_ref[...] = zc
    rstd_ref[...] = lax.rsqrt(jnp.mean(zc * zc, axis=1, keepdims=True) + EPS)


def _outproj(o_da, o_hg, x2d, w_out, gt1, *, seq, tm):
    m, d = x2d.shape
    tiles_per_seq = seq // tm
    return pl.pallas_call(
        _outproj_kernel,
        grid=(m // tm,),
        in_specs=[pl.BlockSpec((tm, GROUP_W), lambda i: (i, 0)),
                  pl.BlockSpec((tm, GROUP_W), lambda i: (i, 0)),
                  pl.BlockSpec((tm, d), lambda i: (i, 0)),
                  pl.BlockSpec((d, d), lambda i: (0, 0), pipeline_mode=pl.Buffered(1)),
                  pl.BlockSpec((1, 1, d), lambda i: (i // tiles_per_seq, 0, 0))],
        out_specs=[pl.BlockSpec((tm, d), lambda i: (i, 0)), pl.BlockSpec((tm, 1), lambda i: (i, 0))],
        out_shape=[jax.ShapeDtypeStruct((m, d), F32), jax.ShapeDtypeStruct((m, 1), F32)],
        compiler_params=_cparams(("arbitrary",)),
        name="outproj_ln1",
    )(o_da, o_hg, x2d, w_out, gt1)


def _ffn_kernel(zc_ref, rstd_ref, w1_ref, w2_ref, sh_ref, sc_ref, gt_ref, g1_ref, b1_ref, g2_ref, b2_ref,
                o_ref, acc_scr, h_scr, u_scr):
    j = pl.program_id(1)
    last = pl.num_programs(1) - 1
    d = zc_ref.shape[1]

    def partial_sum():
        a = jnp.maximum(jnp.dot(u_scr[...], w1_ref[...], preferred_element_type=F32), 0.0)
        return jnp.dot((a * a).astype(BF16), w2_ref[...], preferred_element_type=F32)

    @pl.when(j == 0)
    def _():
        rstd = rstd_ref[...]
        for k in range(d // FFN_NORM_SUB):
            cols = slice(k * FFN_NORM_SUB, (k + 1) * FFN_NORM_SUB)
            h = zc_ref[:, cols] * rstd * g1_ref[:, cols] + b1_ref[:, cols]
            h_scr[:, cols] = h
            u_scr[:, cols] = (h * (1.0 + sc_ref[0][:, cols]) + sh_ref[0][:, cols]).astype(BF16)
        acc_scr[...] = partial_sum()

    @pl.when((j > 0) & (j < last))
    def _():
        acc_scr[...] += partial_sum()

    @pl.when(j == last)
    def _():
        ff = acc_scr[...] + partial_sum()
        o_ref[...] = _layer_norm(ALPHA * h_scr[...] + gt_ref[0] * ff, g2_ref[...], b2_ref[...])


def _ffn(zc, rstd, w1, w2, sh2, sc2, gt2, ln1_g, ln1_b, ln2_g, ln2_b, *, seq, tm, tf):
    m, d = zc.shape
    dff = w1.shape[1]
    assert dff // tf >= 2
    tiles_per_seq = seq // tm
    mod = pl.BlockSpec((1, 1, d), lambda i, j: (i // tiles_per_seq, 0, 0))
    vec = pl.BlockSpec((1, d), lambda i, j: (0, 0))
    return pl.pallas_call(
        _ffn_kernel,
        grid=(m // tm, dff // tf),
        in_specs=[pl.BlockSpec((tm, d), lambda i, j: (i, 0)),
                  pl.BlockSpec((tm, 1), lambda i, j: (i, 0)),
                  pl.BlockSpec((d, tf), lambda i, j: (0, j)),
                  pl.BlockSpec((tf, d), lambda i, j: (j, 0)),
                  mod, mod, mod, vec, vec, vec, vec],
        out_specs=pl.BlockSpec((tm, d), lambda i, j: (i, 0)),
        out_shape=jax.ShapeDtypeStruct((m, d), F32),
        scratch_shapes=[pltpu.VMEM((tm, d), F32), pltpu.VMEM((tm, d), F32), pltpu.VMEM((tm, d), BF16)],
        compiler_params=_cparams(("arbitrary", "arbitrary")),
        name="ffn_ln2",
    )(zc, rstd, w1, w2, sh2, sc2, gt2, ln1_g, ln1_b, ln2_g, ln2_b)


def kernel(x, c, ctx, c_ctx, w_ada, b_ada, w_in, lam_q1, lam_k1, lam_q2, lam_k2, da_norm_w,
           hg_lb_fwd, hg_lb_bwd, hg_norm_w, w_out, ln1_g, ln1_b, w_ff1, w_ff2, ln2_g, ln2_b):
    batch, seq, d = x.shape
    ctx_len = ctx.shape[1]
    assert d == D_MODEL and w_in.shape[0] == DEPTH and seq % GRID_W == 0
    l = 0

    n_cond = batch + 1
    pad = (-n_cond) % 8
    cond = jnp.concatenate([c, c_ctx[None], jnp.zeros((pad, d), F32)], axis=0)
    mod, w_in_b = _adaln(cond, w_ada[l], b_ada[l][None], w_in[l])
    mod = mod.reshape(n_cond + pad, N_ADA, 1, d)
    sh1, sc1, gt1, sh2, sc2, gt2 = [mod[:batch, j] for j in range(N_ADA)]
    csh1, csc1 = mod[batch:batch + 1, 0], mod[batch:batch + 1, 1]

    x2d = x.reshape(batch * seq, d)
    ctx2d = ctx.reshape(batch * ctx_len, d)
    lbs = jnp.stack([hg_lb_fwd.reshape(DEPTH + 1, GROUP_W), hg_lb_bwd.reshape(DEPTH + 1, GROUP_W)])

    tm = min(512, seq)
    tmi = min(256, seq)
    tmc = min(256, ctx_len)
    rope_q = _rope_tables(seq, D_QK ** -0.5 * LOG2E)
    rope_k = _rope_tables(seq, 1.0)
    qkv, hq, logf, w_out_b, w_ff1_b, w_ff2_b = _inproj(
        x2d, sh1, sc1, w_in_b, (0, 1, 2, 3, 6, 7, 4, 5),
        ("rope", "rope", "copy", "silu", "copy", "silu", "logf", "logf"),
        ((BF16, 3), (BF16, 3), (F32, 2)),
        rows_per_mod=seq, tm=tmi, ropes=(rope_q, rope_k), seq=seq, lbs=lbs,
        casts=(w_out[l], w_ff1[l], w_ff2[l]), name="inproj")
    m_ctx = batch * ctx_len
    kvh_ctx, logf_ctx = _inproj(
        ctx2d, csh1, csc1, w_in_b, (1, 2, 6, 4, 5), ("copy", "copy", "copy", "logf", "logf"),
        ((BF16, 3), (F32, 2)), rows_per_mod=m_ctx, tm=tmc, lbs=lbs, name="inproj_ctx")

    o_da = _attention(qkv, kvh_ctx, lam_q1[l][None], lam_k1[l][None], lam_q2[l][None], lam_k2[l][None],
                      da_norm_w[l][None], batch=batch, seq=seq, ctx_len=ctx_len, tq=min(2048, seq))
    o_hg = _hgrn(hq, logf, kvh_ctx, logf_ctx, hg_norm_w[l][None], batch=batch, seq=seq, ctx_len=ctx_len)

    zc, rstd = _outproj(o_da, o_hg, x2d, w_out_b, gt1, seq=seq, tm=tm)
    out = _ffn(zc, rstd, w_ff1_b, w_ff2_b, sh2, sc2, gt2, ln1_g[l][None], ln1_b[l][None],
               ln2_g[l][None], ln2_b[l][None], seq=seq, tm=tm, tf=1024)
    return out.reshape(batch, seq, d)
```

```python
import functools
import math

import jax
import jax.numpy as jnp
import numpy as np
from jax import lax
from jax.experimental import pallas as pl
from jax.experimental.pallas import tpu as pltpu

F32 = jnp.float32
BF16 = jnp.bfloat16

D_MODEL = 2048
GRID_W = 64
HEAD = 128
N_HEADS = 8
GROUP_W = N_HEADS * HEAD
N_GROUPS = 8
D_QK = 64
ROT_AXIS = 32
ROPE_BASE = 10000.0
D_FF = 4 * D_MODEL
N_ADA = 6
EPS = 1e-5
DEPTH = 1
ALPHA = (2.0 * DEPTH) ** 0.25
LAM_INIT = 0.8 - 0.6 * math.exp(-0.3 * 0)
HG_CHUNK = 64
HG_SUB = 16
HG_GROUP = 8
HG_HEADS_PER_STEP = 2
HG_HEAD_LAG = 7
ATTN_SUB = 256
ADALN_STEPS = 8
ATTN_HEADS_PER_STEP = 4
ATTN_KEY_SEGS = 3
ATTN_ONES_ROWS = 16
FFN_NORM_SUB = 256
INPROJ_SUB = 256
LOG2E = 1.4426950408889634
BF16_SUBLANES = 16
F32_SUBLANES = 8

VMEM_LIMIT = 56 * 1024 * 1024


def _cparams(sem):
    return pltpu.CompilerParams(dimension_semantics=sem, vmem_limit_bytes=VMEM_LIMIT)


def _sigmoid(z):
    return 1.0 / (1.0 + jnp.exp(-z))


def _dot_nt(a, b):
    return lax.dot_general(a, b, (((1,), (1,)), ((), ())), preferred_element_type=F32)


def _dot_tn(a, b):
    return lax.dot_general(a, b, (((0,), (0,)), ((), ())), preferred_element_type=F32)


def _adaln_kernel(cond_ref, w_ref, b_ref, cast_in_ref, o_ref, cast_out_ref):
    c = cond_ref[...]
    s = (c * _sigmoid(c)).astype(BF16)
    o_ref[...] = jnp.dot(s, w_ref[...].astype(BF16), preferred_element_type=F32) + b_ref[...]
    cast_out_ref[...] = cast_in_ref[...].astype(BF16)


def _adaln(cond, w, b, w_cast):
    rows, d = cond.shape
    n = w.shape[1]
    steps = ADALN_STEPS
    tn = n // steps
    slab = w_cast.shape[0] // steps
    assert n % steps == 0 and tn % HEAD == 0 and w_cast.shape[0] % steps == 0 and slab % BF16_SUBLANES == 0
    return pl.pallas_call(
        _adaln_kernel,
        grid=(steps,),
        in_specs=[pl.BlockSpec((rows, d), lambda j: (0, 0)),
                  pl.BlockSpec((d, tn), lambda j: (0, j)),
                  pl.BlockSpec((1, tn), lambda j: (0, j)),
                  pl.BlockSpec((slab, w_cast.shape[1]), lambda j: (j, 0))],
        out_specs=[pl.BlockSpec((rows, tn), lambda j: (0, j)),
                   pl.BlockSpec((slab, w_cast.shape[1]), lambda j: (j, 0))],
        out_shape=[jax.ShapeDtypeStruct((rows, n), F32), jax.ShapeDtypeStruct(w_cast.shape, BF16)],
        compiler_params=_cparams(("arbitrary",)),
        name="adaln",
    )(cond, w, b, w_cast)


def _rope_tables(seq, scale):
    t = np.arange(seq)
    row = (t // GRID_W).astype(np.float64)
    col = (t % GRID_W).astype(np.float64)
    lane = np.arange(HEAD)
    inv_freq = ROPE_BASE ** (-np.arange(0, ROT_AXIS, 2, dtype=np.float64) / ROT_AXIS)
    freq = inv_freq[lane % (ROT_AXIS // 2)]
    use_col = ((lane % D_QK) // ROT_AXIS) == 1
    pos = np.where(use_col[None, :], col[:, None], row[:, None])
    ang = pos * freq[None, :]
    first_half = (lane % ROT_AXIS) < (ROT_AXIS // 2)
    cos, sin = np.cos(ang), np.sin(ang)
    t1 = np.where(first_half[None, :], -sin, 0.0)
    t2 = np.where(first_half[None, :], 0.0, sin)
    return jnp.asarray((np.stack([cos, t1, t2]) * scale).astype(np.float32))


def _inproj_kernel(*refs, groups, kinds, places, n_rope, n_out, n_cast):
    x_ref, sh_ref, sc_ref, w_ref = refs[:4]
    rope_refs = refs[4:4 + n_rope]
    n_lb = 1 if "logf" in kinds else 0
    lb_ref = refs[4 + n_rope] if n_lb else None
    cast_in = refs[4 + n_rope + n_lb:4 + n_rope + n_lb + n_cast]
    out_refs = refs[-1 - n_out - n_cast:-1 - n_cast]
    cast_out = refs[-1 - n_cast:-1]
    u_scr = refs[-1]
    n_sub = GROUP_W // INPROJ_SUB
    heads_per_sub = INPROJ_SUB // HEAD
    first_logf = kinds.index("logf") if "logf" in kinds else None

    u_scr[...] = (x_ref[...] * (1.0 + sc_ref[0]) + sh_ref[0]).astype(BF16)

    def sub_dot(g, k):
        c0 = groups[g] * GROUP_W + k * INPROJ_SUB
        return jnp.dot(u_scr[...], w_ref[:, c0:c0 + INPROJ_SUB], preferred_element_type=F32)

    def epilogue(g, k, acc, rope_ref):
        kind = kinds[g]
        o_ref = out_refs[places[g][0]]
        c0 = places[g][1] * GROUP_W + k * INPROJ_SUB
        if kind == "rope":
            for h in range(heads_per_sub):
                xh = acc[:, h * HEAD:(h + 1) * HEAD]
                y = (xh * rope_ref[0]
                     + pltpu.roll(xh, HEAD - ROT_AXIS // 2, 1) * rope_ref[1]
                     + pltpu.roll(xh, ROT_AXIS // 2, 1) * rope_ref[2])
                o_ref[:, c0 + h * HEAD:c0 + (h + 1) * HEAD] = y.astype(o_ref.dtype)
        elif kind == "copy":
            o_ref[:, c0:c0 + INPROJ_SUB] = acc.astype(o_ref.dtype)
        elif kind == "silu":
            o_ref[:, c0:c0 + INPROJ_SUB] = (acc * _sigmoid(acc)).astype(o_ref.dtype)
        elif kind == "logf":
            a = lb_ref[g - first_logf][:, k * INPROJ_SUB:(k + 1) * INPROJ_SUB]
            e = jnp.exp(a - jnp.max(a, axis=0, keepdims=True))
            lb = e[0:1] / jnp.sum(e, axis=0, keepdims=True)
            o_ref[:, c0:c0 + INPROJ_SUB] = jnp.log2(lb + (1.0 - lb) * _sigmoid(acc)).astype(o_ref.dtype)
        else:
            raise ValueError(kind)

    rope_of, ri = {}, 0
    for g, kind in enumerate(kinds):
        if kind == "rope":
            rope_of[g] = rope_refs[ri]
            ri += 1
    jobs = [(g, k) for g in range(len(kinds)) for k in range(n_sub)]
    cast_at = {(c * len(jobs)) // n_cast: c for c in range(n_cast)}
    nxt = sub_dot(*jobs[0])
    for idx, (g, k) in enumerate(jobs):
        acc = nxt
        if idx + 1 < len(jobs):
            nxt = sub_dot(*jobs[idx + 1])
        epilogue(g, k, acc, rope_of.get(g))
        if idx in cast_at:
            c = cast_at[idx]
            cast_out[c][...] = cast_in[c][...].astype(BF16)


def _inproj(x2d, sh, sc, w, groups, kinds, outs, *, rows_per_mod, tm, ropes=(), seq=None, lbs=None, casts=(),
            name):
    m, d = x2d.shape
    ng = len(groups)
    assert m % tm == 0 and rows_per_mod % tm == 0 and sum(n for _, n in outs) == ng
    tiles_per_mod = rows_per_mod // tm
    in_specs = [pl.BlockSpec((tm, d), lambda i: (i, 0)),
                pl.BlockSpec((1, 1, d), lambda i: (i // tiles_per_mod, 0, 0)),
                pl.BlockSpec((1, 1, d), lambda i: (i // tiles_per_mod, 0, 0)),
                pl.BlockSpec(w.shape, lambda i: (0, 0), pipeline_mode=pl.Buffered(1))]
    args = [x2d, sh, sc, w]
    for tab in ropes:
        tiles_per_seq = seq // tm
        in_specs.append(pl.BlockSpec((3, tm, HEAD), lambda i: (0, i % tiles_per_seq, 0)))
        args.append(tab)
    if lbs is not None:
        first_logf = kinds.index("logf")
        assert all(k == "logf" for k in kinds[first_logf:]) and lbs.shape[0] == ng - first_logf
        in_specs.append(pl.BlockSpec(lbs.shape, lambda i: (0, 0, 0)))
        args.append(lbs)
    out_specs, out_shape, places = [], [], []
    for o, (dtype, n) in enumerate(outs):
        out_specs.append(pl.BlockSpec((tm, n * GROUP_W), lambda i: (i, 0)))
        out_shape.append(jax.ShapeDtypeStruct((m, n * GROUP_W), dtype))
        places += [(o, p) for p in range(n)]
    steps = m // tm
    for cw in casts:
        slab = cw.shape[0] // steps
        assert cw.shape[0] % steps == 0 and slab % BF16_SUBLANES == 0
        in_specs.append(pl.BlockSpec((slab, cw.shape[1]), lambda i: (i, 0)))
        args.append(cw)
        out_specs.append(pl.BlockSpec((slab, cw.shape[1]), lambda i: (i, 0)))
        out_shape.append(jax.ShapeDtypeStruct(cw.shape, BF16))
    return pl.pallas_call(
        functools.partial(_inproj_kernel, groups=tuple(groups), kinds=tuple(kinds), places=tuple(places),
                          n_rope=len(ropes), n_out=len(outs), n_cast=len(casts)),
        grid=(steps,),
        in_specs=in_specs,
        out_specs=out_specs,
        out_shape=out_shape,
        scratch_shapes=[pltpu.VMEM((tm, d), BF16)],
        compiler_params=_cparams(("arbitrary",)),
        name=name,
    )(*args)


def _attn_kernel(q_ref, k_ref, v_ref, kc_ref, vc_ref, lq1_ref, lk1_ref, lq2_ref, lk2_ref, nw_ref, o_ref,
                 k_scr, v_scr):
    seq, ctx_len = k_ref.shape[0], kc_ref.shape[0]
    n_keys = seq + ctx_len
    n_heads = q_ref.shape[1] // HEAD
    seg_len = max(256, -(-n_keys // ATTN_KEY_SEGS) // 256 * 256)
    segs = [(a, min(a + seg_len, n_keys)) for a in range(0, n_keys, seg_len)]

    @pl.when(pl.program_id(2) == 0)
    def _():
        for h in range(n_heads):
            hl = slice(h * HEAD, (h + 1) * HEAD)
            k_scr[h, 0:seq, :] = k_ref[:, hl]
            k_scr[h, seq:n_keys, :] = kc_ref[:, hl]
            v_scr[h, 0:HEAD, 0:seq] = v_ref[:, hl].astype(F32).T.astype(BF16)
            v_scr[h, 0:HEAD, seq:n_keys] = vc_ref[:, hl].astype(F32).T.astype(BF16)
            v_scr[h, HEAD:, :] = jnp.ones((v_scr.shape[1] - HEAD, n_keys), BF16)

    lam = (jnp.exp(jnp.sum(lq1_ref[...] * lk1_ref[...], axis=1, keepdims=True))
           - jnp.exp(jnp.sum(lq2_ref[...] * lk2_ref[...], axis=1, keepdims=True)) + LAM_INIT)
    ts = min(ATTN_SUB, q_ref.shape[0])
    lane = lax.broadcasted_iota(jnp.int32, (ts, HEAD), 1)
    zero = jnp.zeros((ts, HEAD), BF16)
    n_sub = q_ref.shape[0] // ts

    def scores(h, r):
        q = q_ref[r * ts:(r + 1) * ts, h * HEAD:(h + 1) * HEAD]
        q2 = jnp.concatenate([jnp.where(lane < D_QK, q, zero), jnp.where(lane >= D_QK, q, zero)], axis=0)
        ss = [_dot_nt(k_scr[h, a:b, :], q2) for a, b in segs]
        m = functools.reduce(jnp.maximum, [jnp.max(s, axis=0, keepdims=True) for s in ss])
        return ss, m

    def probs(ss, m):
        return [jnp.exp2(s - m).astype(BF16) for s in ss]

    def attend(h, r, ps):
        ox = functools.reduce(lambda x, y: x + y, [jnp.dot(v_scr[h, :, a:b], p, preferred_element_type=F32)
                                                   for (a, b), p in zip(segs, ps)])
        on = ox[0:HEAD] / ox[HEAD:HEAD + 1]
        o = (on[:, :ts] - lam * on[:, ts:]).T
        ms = jnp.mean(o * o, axis=1, keepdims=True)
        o_ref[r * ts:(r + 1) * ts, h * HEAD:(h + 1) * HEAD] = (
            o * lax.rsqrt(ms + EPS) * nw_ref[...] * (1.0 - LAM_INIT)).astype(o_ref.dtype)

    items = [(h, r) for h in range(n_heads) for r in range(n_sub)]
    st_a, st_b = {}, {}
    for t in range(len(items) + 2):
        if t < len(items):
            st_a[t] = scores(*items[t])
        if 0 <= t - 1 < len(items):
            st_b[t - 1] = probs(*st_a.pop(t - 1))
        if 0 <= t - 2 < len(items):
            attend(*items[t - 2], st_b.pop(t - 2))


def _attention(qkv, kv_ctx, lq1, lk1, lq2, lk2, norm_w, *, batch, seq, ctx_len, tq):
    nq = seq // tq
    w = ATTN_HEADS_PER_STEP * HEAD
    nb = GROUP_W // w
    vec = lambda n: pl.BlockSpec((1, n), lambda b, h, i: (0, 0))
    return pl.pallas_call(
        _attn_kernel,
        grid=(batch, nb, nq),
        in_specs=[pl.BlockSpec((tq, w), lambda b, h, i: (b * nq + i, h)),
                  pl.BlockSpec((seq, w), lambda b, h, i: (b, nb + h)),
                  pl.BlockSpec((seq, w), lambda b, h, i: (b, 2 * nb + h)),
                  pl.BlockSpec((ctx_len, w), lambda b, h, i: (b, h)),
                  pl.BlockSpec((ctx_len, w), lambda b, h, i: (b, nb + h)),
                  vec(D_QK), vec(D_QK), vec(D_QK), vec(D_QK), vec(HEAD)],
        out_specs=pl.BlockSpec((tq, w), lambda b, h, i: (b * nq + i, h)),
        out_shape=jax.ShapeDtypeStruct((batch * seq, GROUP_W), BF16),
        scratch_shapes=[pltpu.VMEM((ATTN_HEADS_PER_STEP, seq + ctx_len, HEAD), BF16),
                        pltpu.VMEM((ATTN_HEADS_PER_STEP, HEAD + ATTN_ONES_ROWS, seq + ctx_len), BF16)],
        compiler_params=_cparams(("arbitrary", "arbitrary", "arbitrary")),
        name="diff_attn",
    )(qkv, qkv, qkv, kv_ctx, kv_ctx, lq1, lk1, lq2, lk2, norm_w)


def _split3(x):
    hi = x.astype(BF16)
    r1 = x - hi.astype(F32)
    mid = r1.astype(BF16)
    lo = (r1 - mid.astype(F32)).astype(BF16)
    return hi, mid, lo


def _cumsum_rows(tri, x):
    hi, mid, lo = _split3(x)
    return (jnp.dot(tri, hi, preferred_element_type=F32)
            + jnp.dot(tri, mid, preferred_element_type=F32)
            + jnp.dot(tri, lo, preferred_element_type=F32))


def _tri(n, lower):
    r = lax.broadcasted_iota(jnp.int32, (n, n), 0)
    c = lax.broadcasted_iota(jnp.int32, (n, n), 1)
    return jnp.where((r >= c) if lower else (r <= c), 1.0, 0.0).astype(BF16)


def _hgrn_ctx_state(lg2f, v, forward):
    n = lg2f.shape[0]
    b = _cumsum_rows(_tri(n, forward), lg2f)
    b_end = b[n - 1:n] if forward else b[0:1]
    k_hat = ((1.0 - jnp.exp2(lg2f)) * jnp.exp2(b_end - b)).astype(BF16)
    return _dot_tn(v, k_hat)


def _chunk_cumsum(x, forward):
    g, n, c = x.shape
    row = lax.broadcasted_iota(jnp.int32, (1, n, c), 1)
    s = 1
    while s < n:
        if s % F32_SUBLANES == 0:
            z = jnp.zeros((g, s, c), x.dtype)
            shifted = (jnp.concatenate([z, x[:, :n - s]], axis=1) if forward
                       else jnp.concatenate([x[:, s:], z], axis=1))
        else:
            rolled = pltpu.roll(x, s if forward else n - s, 1)
            shifted = jnp.where((row >= s) if forward else (row < n - s), rolled, 0.0)
        x = x + shifted
        s *= 2
    return x


def _hgrn_stream(q_ref, lg_ref, v_ref, rows, state, forward, out_ref):
    L = HG_CHUNK
    n_sub = L // HG_SUB
    G = (rows.stop - rows.start) // L
    x3 = lg_ref[rows, :].reshape(G, L, HEAD)
    b = _chunk_cumsum(x3, forward)
    b_end = b[:, L - 1:L] if forward else b[:, 0:1]
    k = 1.0 - jnp.exp2(x3)
    qf = q_ref[rows, :].reshape(G, L, HEAD).astype(F32)
    q_in = (qf * jnp.exp2(b)).astype(BF16)
    k_hat = (k * jnp.exp2(b_end - b)).astype(BF16)
    e_end = jnp.exp2(b_end)

    refs = [b[:, j * HG_SUB + HG_SUB // 2:j * HG_SUB + HG_SUB // 2 + 1] for j in range(n_sub)]
    ref_own = jnp.concatenate([jnp.broadcast_to(r, (G, HG_SUB, HEAD)) for r in refs], axis=1)
    k_own = (k * jnp.exp2(ref_own - b)).astype(BF16)

    def zero_rows(n):
        return [jnp.zeros((G, n, HEAD), BF16)] if n else []

    qs, ks = [], []
    for j in range(n_sub):
        lo_r, hi_r = j * HG_SUB, (j + 1) * HG_SUB
        live = slice(lo_r, L) if forward else slice(0, hi_r)
        piece = (qf[:, live] * jnp.exp2(b[:, live] - refs[j])).astype(BF16)
        qs.append(jnp.concatenate(zero_rows(lo_r) + [piece] if forward else [piece] + zero_rows(L - hi_r), axis=1))
        ks.append(jnp.concatenate(zero_rows(lo_r) + [k_own[:, lo_r:hi_r]] + zero_rows(L - hi_r), axis=1))
    q_cat = jnp.concatenate(qs, axis=2)
    k_cat = jnp.concatenate(ks, axis=2)
    v3 = v_ref[rows, :].reshape(G, L, HEAD)
    yield
    a_raw = [_dot_nt(q_cat[c], k_cat[c]) for c in range(G)]
    ds = [_dot_tn(v3[c], k_hat[c]) for c in range(G)]
    yield
    ar = lax.broadcasted_iota(jnp.int32, (L, L), 0)
    ac = lax.broadcasted_iota(jnp.int32, (L, L), 1)
    causal = (ar >= ac) if forward else (ar <= ac)
    o_intra = [jnp.dot(jnp.where(causal, a_raw[c], 0.0).astype(BF16), v3[c], preferred_element_type=F32)
               for c in range(G)]
    yield
    p = state[forward]
    p_before = [None] * G
    for c in (range(G) if forward else reversed(range(G))):
        p_before[c] = p.astype(BF16)
        p = p * e_end[c] + ds[c]
    state[forward] = p
    yield
    out_ref[rows, :] = jnp.concatenate([o_intra[c] + _dot_nt(q_in[c], p_before[c]) for c in range(G)], axis=0)


def _advance(gens):
    for gen in list(gens):
        if next(gen, "done") == "done":
            gens.remove(gen)


def _hgrn_head(q_ref, lf_ref, lb_ref, v_ref, g_ref, lfc_ref, lbc_ref, vc_ref, nw_ref, o_ref, of_scr, ob_scr,
               lanes):
    seq = q_ref.shape[0]
    rows = min(HG_GROUP * HG_CHUNK, seq)
    ng = seq // rows
    vc = vc_ref[:, lanes]
    state = {True: _hgrn_ctx_state(lfc_ref[:, lanes], vc, True),
             False: _hgrn_ctx_state(lbc_ref[:, lanes], vc, False)}
    yield
    q_h, lf_h, lb_h, v_h = q_ref.at[:, lanes], lf_ref.at[:, lanes], lb_ref.at[:, lanes], v_ref.at[:, lanes]
    of_h, ob_h = of_scr.at[:, lanes], ob_scr.at[:, lanes]
    streams = []
    for g in range(ng):
        rf = slice(g * rows, (g + 1) * rows)
        rb = slice((ng - 1 - g) * rows, (ng - g) * rows)
        streams.append(_hgrn_stream(q_h, lf_h, v_h, rf, state, True, of_h))
        streams.append(_hgrn_stream(q_h, lb_h, v_h, rb, state, False, ob_h))
    live = []
    while streams or live:
        if streams:
            live.append(streams.pop(0))
        _advance(live)
        yield
    o = of_h[...] + ob_h[...]
    ms = jnp.mean(o * o, axis=1, keepdims=True)
    o_ref[:, lanes] = (o * lax.rsqrt(ms + EPS) * nw_ref[...] * g_ref[:, lanes].astype(F32)).astype(o_ref.dtype)


def _hgrn_kernel(q_ref, lf_ref, lb_ref, v_ref, g_ref, lfc_ref, lbc_ref, vc_ref, nw_ref, o_ref, of_scr, ob_scr):
    n_heads = q_ref.shape[1] // HEAD
    heads = [_hgrn_head(q_ref, lf_ref, lb_ref, v_ref, g_ref, lfc_ref, lbc_ref, vc_ref, nw_ref, o_ref,
                        of_scr, ob_scr, slice(h * HEAD, (h + 1) * HEAD)) for h in range(n_heads)]
    live, tick = [], 0
    while heads or live:
        if heads and tick % HG_HEAD_LAG == 0:
            live.append(heads.pop(0))
        _advance(live)
        tick += 1


def _hgrn(hq, logf, hq_ctx, logf_ctx, norm_w, *, batch, seq, ctx_len):
    w = HG_HEADS_PER_STEP * HEAD
    nb = GROUP_W // w
    return pl.pallas_call(
        _hgrn_kernel,
        grid=(batch, nb),
        in_specs=[pl.BlockSpec((seq, w), lambda b, h: (b, h)),
                  pl.BlockSpec((seq, w), lambda b, h: (b, h)),
                  pl.BlockSpec((seq, w), lambda b, h: (b, nb + h)),
                  pl.BlockSpec((seq, w), lambda b, h: (b, nb + h)),
                  pl.BlockSpec((seq, w), lambda b, h: (b, 2 * nb + h)),
                  pl.BlockSpec((ctx_len, w), lambda b, h: (b, h)),
                  pl.BlockSpec((ctx_len, w), lambda b, h: (b, nb + h)),
                  pl.BlockSpec((ctx_len, w), lambda b, h: (b, 2 * nb + h)),
                  pl.BlockSpec((1, HEAD), lambda b, h: (0, 0))],
        out_specs=pl.BlockSpec((seq, w), lambda b, h: (b, h)),
        out_shape=jax.ShapeDtypeStruct((batch * seq, GROUP_W), BF16),
        scratch_shapes=[pltpu.VMEM((seq, w), F32), pltpu.VMEM((seq, w), F32)],
        compiler_params=_cparams(("arbitrary", "arbitrary")),
        name="hgrn2",
    )(hq, logf, logf, hq, hq, logf_ctx, logf_ctx, hq_ctx, norm_w)


def _layer_norm(z, g, b):
    mu = jnp.mean(z, axis=1, keepdims=True)
    zc = z - mu
    var = jnp.mean(zc * zc, axis=1, keepdims=True)
    return zc * lax.rsqrt(var + EPS) * g + b


def _outproj_kernel(oa_ref, oh_ref, x_ref, w_ref, gt_ref, zc_ref, rstd_ref):
    y = (jnp.dot(oa_ref[...], w_ref[:GROUP_W, :], preferred_element_type=F32)
         + jnp.dot(oh_ref[...], w_ref[GROUP_W:, :], preferred_element_type=F32))
    z = ALPHA * x_ref[...] + gt_ref[0] * y
    zc = z - jnp.mean(z, axis=1, keepdims=True)
    zc_ref[...] = zc
    rstd_ref[...] = lax.rsqrt(jnp.mean(zc * zc, axis=1, keepdims=True) + EPS)


def _outproj(o_da, o_hg, x2d, w_out, gt1, *, seq, tm):
    m, d = x2d.shape
    tiles_per_seq = seq // tm
    return pl.pallas_call(
        _outproj_kernel,
        grid=(m // tm,),
        in_specs=[pl.BlockSpec((tm, GROUP_W), lambda i: (i, 0)),
                  pl.BlockSpec((tm, GROUP_W), lambda i: (i, 0)),
                  pl.BlockSpec((tm, d), lambda i: (i, 0)),
                  pl.BlockSpec((d, d), lambda i: (0, 0), pipeline_mode=pl.Buffered(1)),
                  pl.BlockSpec((1, 1, d), lambda i: (i // tiles_per_seq, 0, 0))],
        out_specs=[pl.BlockSpec((tm, d), lambda i: (i, 0)), pl.BlockSpec((tm, 1), lambda i: (i, 0))],
        out_shape=[jax.ShapeDtypeStruct((m, d), F32), jax.ShapeDtypeStruct((m, 1), F32)],
        compiler_params=_cparams(("arbitrary",)),
        name="outproj_ln1",
    )(o_da, o_hg, x2d, w_out, gt1)


def _ffn_kernel(zc_ref, rstd_ref, w1_ref, w2_ref, sh_ref, sc_ref, gt_ref, g1_ref, b1_ref, g2_ref, b2_ref,
                o_ref, acc_scr, h_scr, u_scr):
    j = pl.program_id(1)
    last = pl.num_programs(1) - 1
    d = zc_ref.shape[1]

    def partial_sum():
        a = jnp.maximum(jnp.dot(u_scr[...], w1_ref[...], preferred_element_type=F32), 0.0)
        return jnp.dot((a * a).astype(BF16), w2_ref[...], preferred_element_type=F32)

    @pl.when(j == 0)
    def _():
        rstd = rstd_ref[...]
        for k in range(d // FFN_NORM_SUB):
            cols = slice(k * FFN_NORM_SUB, (k + 1) * FFN_NORM_SUB)
            h = zc_ref[:, cols] * rstd * g1_ref[:, cols] + b1_ref[:, cols]
            h_scr[:, cols] = h
            u_scr[:, cols] = (h * (1.0 + sc_ref[0][:, cols]) + sh_ref[0][:, cols]).astype(BF16)
        acc_scr[...] = partial_sum()

    @pl.when((j > 0) & (j < last))
    def _():
        acc_scr[...] += partial_sum()

    @pl.when(j == last)
    def _():
        ff = acc_scr[...] + partial_sum()
        o_ref[...] = _layer_norm(ALPHA * h_scr[...] + gt_ref[0] * ff, g2_ref[...], b2_ref[...])


def _ffn(zc, rstd, w1, w2, sh2, sc2, gt2, ln1_g, ln1_b, ln2_g, ln2_b, *, seq, tm, tf):
    m, d = zc.shape
    dff = w1.shape[1]
    assert dff // tf >= 2
    tiles_per_seq = seq // tm
    mod = pl.BlockSpec((1, 1, d), lambda i, j: (i // tiles_per_seq, 0, 0))
    vec = pl.BlockSpec((1, d), lambda i, j: (0, 0))
    return pl.pallas_call(
        _ffn_kernel,
        grid=(m // tm, dff // tf),
        in_specs=[pl.BlockSpec((tm, d), lambda i, j: (i, 0)),
                  pl.BlockSpec((tm, 1), lambda i, j: (i, 0)),
                  pl.BlockSpec((d, tf), lambda i, j: (0, j)),
                  pl.BlockSpec((tf, d), lambda i, j: (j, 0)),
                  mod, mod, mod, vec, vec, vec, vec],
        out_specs=pl.BlockSpec((tm, d), lambda i, j: (i, 0)),
        out_shape=jax.ShapeDtypeStruct((m, d), F32),
        scratch_shapes=[pltpu.VMEM((tm, d), F32), pltpu.VMEM((tm, d), F32), pltpu.VMEM((tm, d), BF16)],
        compiler_params=_cparams(("arbitrary", "arbitrary")),
        name="ffn_ln2",
    )(zc, rstd, w1, w2, sh2, sc2, gt2, ln1_g, ln1_b, ln2_g, ln2_b)


def kernel(x, c, ctx, c_ctx, w_ada, b_ada, w_in, lam_q1, lam_k1, lam_q2, lam_k2, da_norm_w,
           hg_lb_fwd, hg_lb_bwd, hg_norm_w, w_out, ln1_g, ln1_b, w_ff1, w_ff2, ln2_g, ln2_b):
    batch, seq, d = x.shape
    ctx_len = ctx.shape[1]
    assert d == D_MODEL and w_in.shape[0] == DEPTH and seq % GRID_W == 0
    l = 0

    n_cond = batch + 1
    pad = (-n_cond) % 8
    cond = jnp.concatenate([c, c_ctx[None], jnp.zeros((pad, d), F32)], axis=0)
    mod, w_in_b = _adaln(cond, w_ada[l], b_ada[l][None], w_in[l])
    mod = mod.reshape(n_cond + pad, N_ADA, 1, d)
    sh1, sc1, gt1, sh2, sc2, gt2 = [mod[:batch, j] for j in range(N_ADA)]
    csh1, csc1 = mod[batch:batch + 1, 0], mod[batch:batch + 1, 1]

    x2d = x.reshape(batch * seq, d)
    ctx2d = ctx.reshape(batch * ctx_len, d)
    lbs = jnp.stack([hg_lb_fwd.reshape(DEPTH + 1, GROUP_W), hg_lb_bwd.reshape(DEPTH + 1, GROUP_W)])

    tm = min(512, seq)
    tmi = min(256, seq)
    tmc = min(256, ctx_len)
    rope_q = _rope_tables(seq, D_QK ** -0.5 * LOG2E)
    rope_k = _rope_tables(seq, 1.0)
    qkv, hq, logf, w_out_b, w_ff1_b, w_ff2_b = _inproj(
        x2d, sh1, sc1, w_in_b, (0, 1, 2, 3, 6, 7, 4, 5),
        ("rope", "rope", "copy", "silu", "copy", "silu", "logf", "logf"),
        ((BF16, 3), (BF16, 3), (F32, 2)),
        rows_per_mod=seq, tm=tmi, ropes=(rope_q, rope_k), seq=seq, lbs=lbs,
        casts=(w_out[l], w_ff1[l], w_ff2[l]), name="inproj")
    m_ctx = batch * ctx_len
    kvh_ctx, logf_ctx = _inproj(
        ctx2d, csh1, csc1, w_in_b, (1, 2, 6, 4, 5), ("copy", "copy", "copy", "logf", "logf"),
        ((BF16, 3), (F32, 2)), rows_per_mod=m_ctx, tm=tmc, lbs=lbs, name="inproj_ctx")

    o_da = _attention(qkv, kvh_ctx, lam_q1[l][None], lam_k1[l][None], lam_q2[l][None], lam_k2[l][None],
                      da_norm_w[l][None], batch=batch, seq=seq, ctx_len=ctx_len, tq=min(2048, seq))
    o_hg = _hgrn(hq, logf, kvh_ctx, logf_ctx, hg_norm_w[l][None], batch=batch, seq=seq, ctx_len=ctx_len)

    zc, rstd = _outproj(o_da, o_hg, x2d, w_out_b, gt1, seq=seq, tm=tm)
    out = _ffn(zc, rstd, w_ff1_b, w_ff2_b, sh2, sc2, gt2, ln1_g[l][None], ln1_b[l][None],
               ln2_g[l][None], ln2_b[l][None], seq=seq, tm=tm, tf=1024)
    return out.reshape(batch, seq, d)
```

```python
import functools
import math

import jax
import jax.numpy as jnp
import numpy as np
from jax import lax
from jax.experimental import pallas as pl
from jax.experimental.pallas import tpu as pltpu

F32 = jnp.float32
BF16 = jnp.bfloat16

D_MODEL = 2048
GRID_W = 64
HEAD = 128
N_HEADS = 8
GROUP_W = N_HEADS * HEAD
N_GROUPS = 8
D_QK = 64
ROT_AXIS = 32
ROPE_BASE = 10000.0
N_ADA = 6
EPS = 1e-5
DEPTH = 1
ALPHA = (2.0 * DEPTH) ** 0.25
LAM_INIT = 0.8 - 0.6 * math.exp(-0.3 * 0)
HG_CHUNK = 64
HG_SUB = 16
HG_GROUP = 8
HG_HEADS_PER_STEP = 2
HG_HEAD_LAG = 7
ATTN_SUB = 256
ADALN_STEPS = 8
ATTN_HEADS_PER_STEP = 4
ATTN_KEY_SEGS = 3
ATTN_ONES_ROWS = 16
FFN_NORM_SUB = 256
INPROJ_SUB = 256
LOG2E = 1.4426950408889634
BF16_SUBLANES = 16
F32_SUBLANES = 8

VMEM_LIMIT = 56 * 1024 * 1024


def _cparams(sem):
    return pltpu.CompilerParams(dimension_semantics=sem, vmem_limit_bytes=VMEM_LIMIT)


def _sigmoid(z):
    return 1.0 / (1.0 + jnp.exp(-z))


def _dot_nt(a, b):
    return lax.dot_general(a, b, (((1,), (1,)), ((), ())), preferred_element_type=F32)


def _dot_tn(a, b):
    return lax.dot_general(a, b, (((0,), (0,)), ((), ())), preferred_element_type=F32)


def _adaln_kernel(cond_ref, w_ref, b_ref, cast_in_ref, o_ref, cast_out_ref):
    c = cond_ref[...]
    s = (c * _sigmoid(c)).astype(BF16)
    o_ref[...] = jnp.dot(s, w_ref[...].astype(BF16), preferred_element_type=F32) + b_ref[...]
    cast_out_ref[...] = cast_in_ref[...].astype(BF16)


def _adaln(cond, w, b, w_cast):
    rows, d = cond.shape
    n = w.shape[1]
    steps = ADALN_STEPS
    tn = n // steps
    slab = w_cast.shape[0] // steps
    assert n % steps == 0 and tn % HEAD == 0 and w_cast.shape[0] % steps == 0 and slab % BF16_SUBLANES == 0
    return pl.pallas_call(
        _adaln_kernel,
        grid=(steps,),
        in_specs=[pl.BlockSpec((rows, d), lambda j: (0, 0)),
                  pl.BlockSpec((d, tn), lambda j: (0, j)),
                  pl.BlockSpec((1, tn), lambda j: (0, j)),
                  pl.BlockSpec((slab, w_cast.shape[1]), lambda j: (j, 0))],
        out_specs=[pl.BlockSpec((rows, tn), lambda j: (0, j)),
                   pl.BlockSpec((slab, w_cast.shape[1]), lambda j: (j, 0))],
        out_shape=[jax.ShapeDtypeStruct((rows, n), F32), jax.ShapeDtypeStruct(w_cast.shape, BF16)],
        compiler_params=_cparams(("arbitrary",)),
        name="adaln",
    )(cond, w, b, w_cast)


def _rope_tables(seq, scale):
    t = np.arange(seq)
    row = (t // GRID_W).astype(np.float64)
    col = (t % GRID_W).astype(np.float64)
    lane = np.arange(HEAD)
    inv_freq = ROPE_BASE ** (-np.arange(0, ROT_AXIS, 2, dtype=np.float64) / ROT_AXIS)
    freq = inv_freq[lane % (ROT_AXIS // 2)]
    use_col = ((lane % D_QK) // ROT_AXIS) == 1
    pos = np.where(use_col[None, :], col[:, None], row[:, None])
    ang = pos * freq[None, :]
    first_half = (lane % ROT_AXIS) < (ROT_AXIS // 2)
    cos, sin = np.cos(ang), np.sin(ang)
    t1 = np.where(first_half[None, :], -sin, 0.0)
    t2 = np.where(first_half[None, :], 0.0, sin)
    return jnp.asarray((np.stack([cos, t1, t2]) * scale).astype(np.float32))


def _inproj_kernel(*refs, groups, kinds, places, n_rope, n_out, n_cast):
    x_ref, sh_ref, sc_ref, w_ref = refs[:4]
    rope_refs = refs[4:4 + n_rope]
    n_lb = 1 if "logf" in kinds else 0
    lb_ref = refs[4 + n_rope] if n_lb else None
    cast_in = refs[4 + n_rope + n_lb:4 + n_rope + n_lb + n_cast]
    out_refs = refs[-1 - n_out - n_cast:-1 - n_cast]
    cast_out = refs[-1 - n_cast:-1]
    u_scr = refs[-1]
    n_sub = GROUP_W // INPROJ_SUB
    heads_per_sub = INPROJ_SUB // HEAD
    first_logf = kinds.index("logf") if "logf" in kinds else None

    u_scr[...] = (x_ref[...] * (1.0 + sc_ref[0]) + sh_ref[0]).astype(BF16)

    def sub_dot(g, k):
        c0 = groups[g] * GROUP_W + k * INPROJ_SUB
        return jnp.dot(u_scr[...], w_ref[:, c0:c0 + INPROJ_SUB], preferred_element_type=F32)

    def epilogue(g, k, acc, rope_ref):
        kind = kinds[g]
        o_ref = out_refs[places[g][0]]
        c0 = places[g][1] * GROUP_W + k * INPROJ_SUB
        if kind == "rope":
            for h in range(heads_per_sub):
                xh = acc[:, h * HEAD:(h + 1) * HEAD]
                y = (xh * rope_ref[0]
                     + pltpu.roll(xh, HEAD - ROT_AXIS // 2, 1) * rope_ref[1]
                     + pltpu.roll(xh, ROT_AXIS // 2, 1) * rope_ref[2])
                o_ref[:, c0 + h * HEAD:c0 + (h + 1) * HEAD] = y.astype(o_ref.dtype)
        elif kind == "copy":
            o_ref[:, c0:c0 + INPROJ_SUB] = acc.astype(o_ref.dtype)
        elif kind == "silu":
            o_ref[:, c0:c0 + INPROJ_SUB] = (acc * _sigmoid(acc)).astype(o_ref.dtype)
        elif kind == "logf":
            a = lb_ref[g - first_logf][:, k * INPROJ_SUB:(k + 1) * INPROJ_SUB]
            e = jnp.exp(a - jnp.max(a, axis=0, keepdims=True))
            lb = e[0:1] / jnp.sum(e, axis=0, keepdims=True)
            o_ref[:, c0:c0 + INPROJ_SUB] = jnp.log2(lb + (1.0 - lb) * _sigmoid(acc)).astype(o_ref.dtype)
        else:
            raise ValueError(kind)

    rope_of, ri = {}, 0
    for g, kind in enumerate(kinds):
        if kind == "rope":
            rope_of[g] = rope_refs[ri]
            ri += 1
    jobs = [(g, k) for g in range(len(kinds)) for k in range(n_sub)]
    cast_at = {(c * len(jobs)) // n_cast: c for c in range(n_cast)}
    nxt = sub_dot(*jobs[0])
    for idx, (g, k) in enumerate(jobs):
        acc = nxt
        if idx + 1 < len(jobs):
            nxt = sub_dot(*jobs[idx + 1])
        epilogue(g, k, acc, rope_of.get(g))
        if idx in cast_at:
            c = cast_at[idx]
            cast_out[c][...] = cast_in[c][...].astype(BF16)


def _inproj(x2d, sh, sc, w, groups, kinds, outs, *, rows_per_mod, tm, ropes=(), seq=None, lbs=None, casts=(),
            name):
    m, d = x2d.shape
    ng = len(groups)
    assert m % tm == 0 and rows_per_mod % tm == 0 and sum(n for _, n in outs) == ng
    tiles_per_mod = rows_per_mod // tm
    in_specs = [pl.BlockSpec((tm, d), lambda i: (i, 0)),
                pl.BlockSpec((1, 1, d), lambda i: (i // tiles_per_mod, 0, 0)),
                pl.BlockSpec((1, 1, d), lambda i: (i // tiles_per_mod, 0, 0)),
                pl.BlockSpec(w.shape, lambda i: (0, 0), pipeline_mode=pl.Buffered(1))]
    args = [x2d, sh, sc, w]
    for tab in ropes:
        tiles_per_seq = seq // tm
        in_specs.append(pl.BlockSpec((3, tm, HEAD), lambda i: (0, i % tiles_per_seq, 0)))
        args.append(tab)
    if lbs is not None:
        first_logf = kinds.index("logf")
        assert all(k == "logf" for k in kinds[first_logf:]) and lbs.shape[0] == ng - first_logf
        in_specs.append(pl.BlockSpec(lbs.shape, lambda i: (0, 0, 0)))
        args.append(lbs)
    out_specs, out_shape, places = [], [], []
    for o, (dtype, n) in enumerate(outs):
        out_specs.append(pl.BlockSpec((tm, n * GROUP_W), lambda i: (i, 0)))
        out_shape.append(jax.ShapeDtypeStruct((m, n * GROUP_W), dtype))
        places += [(o, p) for p in range(n)]
    steps = m // tm
    for cw in casts:
        slab = cw.shape[0] // steps
        assert cw.shape[0] % steps == 0 and slab % BF16_SUBLANES == 0
        in_specs.append(pl.BlockSpec((slab, cw.shape[1]), lambda i: (i, 0)))
        args.append(cw)
        out_specs.append(pl.BlockSpec((slab, cw.shape[1]), lambda i: (i, 0)))
        out_shape.append(jax.ShapeDtypeStruct(cw.shape, BF16))
    return pl.pallas_call(
        functools.partial(_inproj_kernel, groups=tuple(groups), kinds=tuple(kinds), places=tuple(places),
                          n_rope=len(ropes), n_out=len(outs), n_cast=len(casts)),
        grid=(steps,),
        in_specs=in_specs,
        out_specs=out_specs,
        out_shape=out_shape,
        scratch_shapes=[pltpu.VMEM((tm, d), BF16)],
        compiler_params=_cparams(("arbitrary",)),
        name=name,
    )(*args)


def _attn_kernel(q_ref, k_ref, v_ref, kc_ref, vc_ref, lq1_ref, lk1_ref, lq2_ref, lk2_ref, nw_ref, o_ref,
                 k_scr, v_scr):
    seq, ctx_len = k_ref.shape[0], kc_ref.shape[0]
    n_keys = seq + ctx_len
    n_heads = q_ref.shape[1] // HEAD
    seg_len = max(256, -(-n_keys // ATTN_KEY_SEGS) // 256 * 256)
    segs = [(a, min(a + seg_len, n_keys)) for a in range(0, n_keys, seg_len)]

    @pl.when(pl.program_id(2) == 0)
    def _():
        for h in range(n_heads):
            hl = slice(h * HEAD, (h + 1) * HEAD)
            k_scr[h, 0:seq, :] = k_ref[:, hl]
            k_scr[h, seq:n_keys, :] = kc_ref[:, hl]
            v_scr[h, 0:HEAD, 0:seq] = v_ref[:, hl].astype(F32).T.astype(BF16)
            v_scr[h, 0:HEAD, seq:n_keys] = vc_ref[:, hl].astype(F32).T.astype(BF16)
            v_scr[h, HEAD:, :] = jnp.ones((v_scr.shape[1] - HEAD, n_keys), BF16)

    lam = (jnp.exp(jnp.sum(lq1_ref[...] * lk1_ref[...], axis=1, keepdims=True))
           - jnp.exp(jnp.sum(lq2_ref[...] * lk2_ref[...], axis=1, keepdims=True)) + LAM_INIT)
    ts = min(ATTN_SUB, q_ref.shape[0])
    lane = lax.broadcasted_iota(jnp.int32, (ts, HEAD), 1)
    zero = jnp.zeros((ts, HEAD), BF16)
    n_sub = q_ref.shape[0] // ts

    def scores(h, r):
        q = q_ref[r * ts:(r + 1) * ts, h * HEAD:(h + 1) * HEAD]
        q2 = jnp.concatenate([jnp.where(lane < D_QK, q, zero), jnp.where(lane >= D_QK, q, zero)], axis=0)
        ss = [_dot_nt(k_scr[h, a:b, :], q2) for a, b in segs]
        m = functools.reduce(jnp.maximum, [jnp.max(s, axis=0, keepdims=True) for s in ss])
        return ss, m

    def probs(ss, m):
        return [jnp.exp2(s - m).astype(BF16) for s in ss]

    def attend(h, r, ps):
        ox = functools.reduce(lambda x, y: x + y, [jnp.dot(v_scr[h, :, a:b], p, preferred_element_type=F32)
                                                   for (a, b), p in zip(segs, ps)])
        on = ox[0:HEAD] / ox[HEAD:HEAD + 1]
        o = (on[:, :ts] - lam * on[:, ts:]).T
        ms = jnp.mean(o * o, axis=1, keepdims=True)
        o_ref[r * ts:(r + 1) * ts, h * HEAD:(h + 1) * HEAD] = (
            o * lax.rsqrt(ms + EPS) * nw_ref[...] * (1.0 - LAM_INIT)).astype(o_ref.dtype)

    items = [(h, r) for h in range(n_heads) for r in range(n_sub)]
    st_a, st_b = {}, {}
    for t in range(len(items) + 2):
        if t < len(items):
            st_a[t] = scores(*items[t])
        if 0 <= t - 1 < len(items):
            st_b[t - 1] = probs(*st_a.pop(t - 1))
        if 0 <= t - 2 < len(items):
            attend(*items[t - 2], st_b.pop(t - 2))


def _attention(qkv, kv_ctx, lq1, lk1, lq2, lk2, norm_w, *, batch, seq, ctx_len, tq):
    nq = seq // tq
    w = ATTN_HEADS_PER_STEP * HEAD
    nb = GROUP_W // w
    vec = lambda n: pl.BlockSpec((1, n), lambda b, h, i: (0, 0))
    return pl.pallas_call(
        _attn_kernel,
        grid=(batch, nb, nq),
        in_specs=[pl.BlockSpec((tq, w), lambda b, h, i: (b * nq + i, h)),
                  pl.BlockSpec((seq, w), lambda b, h, i: (b, nb + h)),
                  pl.BlockSpec((seq, w), lambda b, h, i: (b, 2 * nb + h)),
                  pl.BlockSpec((ctx_len, w), lambda b, h, i: (b, h)),
                  pl.BlockSpec((ctx_len, w), lambda b, h, i: (b, nb + h)),
                  vec(D_QK), vec(D_QK), vec(D_QK), vec(D_QK), vec(HEAD)],
        out_specs=pl.BlockSpec((tq, w), lambda b, h, i: (b * nq + i, h)),
        out_shape=jax.ShapeDtypeStruct((batch * seq, GROUP_W), BF16),
        scratch_shapes=[pltpu.VMEM((ATTN_HEADS_PER_STEP, seq + ctx_len, HEAD), BF16),
                        pltpu.VMEM((ATTN_HEADS_PER_STEP, HEAD + ATTN_ONES_ROWS, seq + ctx_len), BF16)],
        compiler_params=_cparams(("arbitrary", "arbitrary", "arbitrary")),
        name="diff_attn",
    )(qkv, qkv, qkv, kv_ctx, kv_ctx, lq1, lk1, lq2, lk2, norm_w)


def _split3(x):
    hi = x.astype(BF16)
    r1 = x - hi.astype(F32)
    mid = r1.astype(BF16)
    lo = (r1 - mid.astype(F32)).astype(BF16)
    return hi, mid, lo


def _cumsum_rows(tri, x):
    hi, mid, lo = _split3(x)
    return (jnp.dot(tri, hi, preferred_element_type=F32)
            + jnp.dot(tri, mid, preferred_element_type=F32)
            + jnp.dot(tri, lo, preferred_element_type=F32))


def _tri(n, lower):
    r = lax.broadcasted_iota(jnp.int32, (n, n), 0)
    c = lax.broadcasted_iota(jnp.int32, (n, n), 1)
    return jnp.where((r >= c) if lower else (r <= c), 1.0, 0.0).astype(BF16)


def _hgrn_ctx_state(lg2f, v, forward):
    n = lg2f.shape[0]
    b = _cumsum_rows(_tri(n, forward), lg2f)
    b_end = b[n - 1:n] if forward else b[0:1]
    k_hat = ((1.0 - jnp.exp2(lg2f)) * jnp.exp2(b_end - b)).astype(BF16)
    return _dot_tn(v, k_hat)


def _chunk_cumsum(x, forward):
    g, n, c = x.shape
    row = lax.broadcasted_iota(jnp.int32, (1, n, c), 1)
    s = 1
    while s < n:
        if s % F32_SUBLANES == 0:
            z = jnp.zeros((g, s, c), x.dtype)
            shifted = (jnp.concatenate([z, x[:, :n - s]], axis=1) if forward
                       else jnp.concatenate([x[:, s:], z], axis=1))
        else:
            rolled = pltpu.roll(x, s if forward else n - s, 1)
            shifted = jnp.where((row >= s) if forward else (row < n - s), rolled, 0.0)
        x = x + shifted
        s *= 2
    return x


def _hgrn_stream(q_ref, lg_ref, v_ref, rows, state, forward, out_ref):
    L = HG_CHUNK
    n_sub = L // HG_SUB
    G = (rows.stop - rows.start) // L
    x3 = lg_ref[rows, :].reshape(G, L, HEAD)
    b = _chunk_cumsum(x3, forward)
    b_end = b[:, L - 1:L] if forward else b[:, 0:1]
    k = 1.0 - jnp.exp2(x3)
    qf = q_ref[rows, :].reshape(G, L, HEAD).astype(F32)
    q_in = (qf * jnp.exp2(b)).astype(BF16)
    k_hat = (k * jnp.exp2(b_end - b)).astype(BF16)
    e_end = jnp.exp2(b_end)

    refs = [b[:, j * HG_SUB + HG_SUB // 2:j * HG_SUB + HG_SUB // 2 + 1] for j in range(n_sub)]
    ref_own = jnp.concatenate([jnp.broadcast_to(r, (G, HG_SUB, HEAD)) for r in refs], axis=1)
    k_own = (k * jnp.exp2(ref_own - b)).astype(BF16)

    def zero_rows(n):
        return [jnp.zeros((G, n, HEAD), BF16)] if n else []

    qs, ks = [], []
    for j in range(n_sub):
        lo_r, hi_r = j * HG_SUB, (j + 1) * HG_SUB
        live = slice(lo_r, L) if forward else slice(0, hi_r)
        piece = (qf[:, live] * jnp.exp2(b[:, live] - refs[j])).astype(BF16)
        qs.append(jnp.concatenate(zero_rows(lo_r) + [piece] if forward else [piece] + zero_rows(L - hi_r), axis=1))
        ks.append(jnp.concatenate(zero_rows(lo_r) + [k_own[:, lo_r:hi_r]] + zero_rows(L - hi_r), axis=1))
    q_cat = jnp.concatenate(qs, axis=2)
    k_cat = jnp.concatenate(ks, axis=2)
    v3 = v_ref[rows, :].reshape(G, L, HEAD)
    yield
    a_raw = [_dot_nt(q_cat[c], k_cat[c]) for c in range(G)]
    ds = [_dot_tn(v3[c], k_hat[c]) for c in range(G)]
    yield
    ar = lax.broadcasted_iota(jnp.int32, (L, L), 0)
    ac = lax.broadcasted_iota(jnp.int32, (L, L), 1)
    causal = (ar >= ac) if forward else (ar <= ac)
    o_intra = [jnp.dot(jnp.where(causal, a_raw[c], 0.0).astype(BF16), v3[c], preferred_element_type=F32)
               for c in range(G)]
    yield
    p = state[forward]
    p_before = [None] * G
    for c in (range(G) if forward else reversed(range(G))):
        p_before[c] = p.astype(BF16)
        p = p * e_end[c] + ds[c]
    state[forward] = p
    yield
    out_ref[rows, :] = jnp.concatenate([o_intra[c] + _dot_nt(q_in[c], p_before[c]) for c in range(G)], axis=0)


def _advance(gens):
    for gen in list(gens):
        if next(gen, "done") == "done":
            gens.remove(gen)


def _hgrn_head(q_ref, lf_ref, lb_ref, v_ref, g_ref, lfc_ref, lbc_ref, vc_ref, nw_ref, o_ref, of_scr, ob_scr,
               lanes):
    seq = q_ref.shape[0]
    rows = min(HG_GROUP * HG_CHUNK, seq)
    ng = seq // rows
    vc = vc_ref[:, lanes]
    state = {True: _hgrn_ctx_state(lfc_ref[:, lanes], vc, True),
             False: _hgrn_ctx_state(lbc_ref[:, lanes], vc, False)}
    yield
    q_h, lf_h, lb_h, v_h = q_ref.at[:, lanes], lf_ref.at[:, lanes], lb_ref.at[:, lanes], v_ref.at[:, lanes]
    of_h, ob_h = of_scr.at[:, lanes], ob_scr.at[:, lanes]
    streams = []
    for g in range(ng):
        rf = slice(g * rows, (g + 1) * rows)
        rb = slice((ng - 1 - g) * rows, (ng - g) * rows)
        streams.append(_hgrn_stream(q_h, lf_h, v_h, rf, state, True, of_h))
        streams.append(_hgrn_stream(q_h, lb_h, v_h, rb, state, False, ob_h))
    live = []
    while streams or live:
        if streams:
            live.append(streams.pop(0))
        _advance(live)
        yield
    o = of_h[...] + ob_h[...]
    ms = jnp.mean(o * o, axis=1, keepdims=True)
    o_ref[:, lanes] = (o * lax.rsqrt(ms + EPS) * nw_ref[...] * g_ref[:, lanes].astype(F32)).astype(o_ref.dtype)


def _hgrn_kernel(q_ref, lf_ref, lb_ref, v_ref, g_ref, lfc_ref, lbc_ref, vc_ref, nw_ref, o_ref, of_scr, ob_scr):
    n_heads = q_ref.shape[1] // HEAD
    heads = [_hgrn_head(q_ref, lf_ref, lb_ref, v_ref, g_ref, lfc_ref, lbc_ref, vc_ref, nw_ref, o_ref,
                        of_scr, ob_scr, slice(h * HEAD, (h + 1) * HEAD)) for h in range(n_heads)]
    live, tick = [], 0
    while heads or live:
        if heads and tick % HG_HEAD_LAG == 0:
            live.append(heads.pop(0))
        _advance(live)
        tick += 1


def _hgrn(hq, logf, hq_ctx, logf_ctx, norm_w, *, batch, seq, ctx_len):
    w = HG_HEADS_PER_STEP * HEAD
    nb = GROUP_W // w
    return pl.pallas_call(
        _hgrn_kernel,
        grid=(batch, nb),
        in_specs=[pl.BlockSpec((seq, w), lambda b, h: (b, h)),
                  pl.BlockSpec((seq, w), lambda b, h: (b, h)),
                  pl.BlockSpec((seq, w), lambda b, h: (b, nb + h)),
                  pl.BlockSpec((seq, w), lambda b, h: (b, nb + h)),
                  pl.BlockSpec((seq, w), lambda b, h: (b, 2 * nb + h)),
                  pl.BlockSpec((ctx_len, w), lambda b, h: (b, h)),
                  pl.BlockSpec((ctx_len, w), lambda b, h: (b, nb + h)),
                  pl.BlockSpec((ctx_len, w), lambda b, h: (b, 2 * nb + h)),
                  pl.BlockSpec((1, HEAD), lambda b, h: (0, 0))],
        out_specs=pl.BlockSpec((seq, w), lambda b, h: (b, h)),
        out_shape=jax.ShapeDtypeStruct((batch * seq, GROUP_W), BF16),
        scratch_shapes=[pltpu.VMEM((seq, w), F32), pltpu.VMEM((seq, w), F32)],
        compiler_params=_cparams(("arbitrary", "arbitrary")),
        name="hgrn2",
    )(hq, logf, logf, hq, hq, logf_ctx, logf_ctx, hq_ctx, norm_w)


def _layer_norm(z, g, b):
    mu = jnp.mean(z, axis=1, keepdims=True)
    zc = z - mu
    var = jnp.mean(zc * zc, axis=1, keepdims=True)
    return zc * lax.rsqrt(var + EPS) * g + b


def _outproj_kernel(oa_ref, oh_ref, x_ref, w_ref, gt_ref, zc_ref, rstd_ref):
    y = (jnp.dot(oa_ref[...], w_ref[:GROUP_W, :], preferred_element_type=F32)
         + jnp.dot(oh_ref[...], w_ref[GROUP_W:, :], preferred_element_type=F32))
    z = ALPHA * x_ref[...] + gt_ref[0] * y
    zc = z - jnp.mean(z, axis=1, keepdims=True)
    zc_ref[...] = zc
    rstd_ref[...] = lax.rsqrt(jnp.mean(zc * zc, axis=1, keepdims=True) + EPS)


def _outproj(o_da, o_hg, x2d, w_out, gt1, *, seq, tm):
    m, d = x2d.shape
    tiles_per_seq = seq // tm
    return pl.pallas_call(
        _outproj_kernel,
        grid=(m // tm,),
        in_specs=[pl.BlockSpec((tm, GROUP_W), lambda i: (i, 0)),
                  pl.BlockSpec((tm, GROUP_W), lambda i: (i, 0)),
                  pl.BlockSpec((tm, d), lambda i: (i, 0)),
                  pl.BlockSpec((d, d), lambda i: (0, 0), pipeline_mode=pl.Buffered(1)),
                  pl.BlockSpec((1, 1, d), lambda i: (i // tiles_per_seq, 0, 0))],
        out_specs=[pl.BlockSpec((tm, d), lambda i: (i, 0)), pl.BlockSpec((tm, 1), lambda i: (i, 0))],
        out_shape=[jax.ShapeDtypeStruct((m, d), F32), jax.ShapeDtypeStruct((m, 1), F32)],
        compiler_params=_cparams(("arbitrary",)),
        name="outproj_ln1",
    )(o_da, o_hg, x2d, w_out, gt1)


def _ffn_kernel(zc_ref, rstd_ref, w1_ref, w2_ref, sh_ref, sc_ref, gt_ref, g1_ref, b1_ref, g2_ref, b2_ref,
                o_ref, acc_scr, h_scr, u_scr):
    j = pl.program_id(1)
    last = pl.num_programs(1) - 1
    d = zc_ref.shape[1]

    def partial_sum():
        a = jnp.maximum(jnp.dot(u_scr[...], w1_ref[...], preferred_element_type=F32), 0.0)
        return jnp.dot((a * a).astype(BF16), w2_ref[...], preferred_element_type=F32)

    @pl.when(j == 0)
    def _():
        rstd = rstd_ref[...]
        for k in range(d // FFN_NORM_SUB):
            cols = slice(k * FFN_NORM_SUB, (k + 1) * FFN_NORM_SUB)
            h = zc_ref[:, cols] * rstd * g1_ref[:, cols] + b1_ref[:, cols]
            h_scr[:, cols] = h
            u_scr[:, cols] = (h * (1.0 + sc_ref[0][:, cols]) + sh_ref[0][:, cols]).astype(BF16)
        acc_scr[...] = partial_sum()

    @pl.when((j > 0) & (j < last))
    def _():
        acc_scr[...] += partial_sum()

    @pl.when(j == last)
    def _():
        ff = acc_scr[...] + partial_sum()
        o_ref[...] = _layer_norm(ALPHA * h_scr[...] + gt_ref[0] * ff, g2_ref[...], b2_ref[...])


def _ffn(zc, rstd, w1, w2, sh2, sc2, gt2, ln1_g, ln1_b, ln2_g, ln2_b, *, seq, tm, tf):
    m, d = zc.shape
    dff = w1.shape[1]
    assert dff // tf >= 2
    tiles_per_seq = seq // tm
    mod = pl.BlockSpec((1, 1, d), lambda i, j: (i // tiles_per_seq, 0, 0))
    vec = pl.BlockSpec((1, d), lambda i, j: (0, 0))
    return pl.pallas_call(
        _ffn_kernel,
        grid=(m // tm, dff // tf),
        in_specs=[pl.BlockSpec((tm, d), lambda i, j: (i, 0)),
                  pl.BlockSpec((tm, 1), lambda i, j: (i, 0)),
                  pl.BlockSpec((d, tf), lambda i, j: (0, j)),
                  pl.BlockSpec((tf, d), lambda i, j: (j, 0)),
                  mod, mod, mod, vec, vec, vec, vec],
        out_specs=pl.BlockSpec((tm, d), lambda i, j: (i, 0)),
        out_shape=jax.ShapeDtypeStruct((m, d), F32),
        scratch_shapes=[pltpu.VMEM((tm, d), F32), pltpu.VMEM((tm, d), F32), pltpu.VMEM((tm, d), BF16)],
        compiler_params=_cparams(("arbitrary", "arbitrary")),
        name="ffn_ln2",
    )(zc, rstd, w1, w2, sh2, sc2, gt2, ln1_g, ln1_b, ln2_g, ln2_b)


def kernel(x, c, ctx, c_ctx, w_ada, b_ada, w_in, lam_q1, lam_k1, lam_q2, lam_k2, da_norm_w,
           hg_lb_fwd, hg_lb_bwd, hg_norm_w, w_out, ln1_g, ln1_b, w_ff1, w_ff2, ln2_g, ln2_b):
    batch, seq, d = x.shape
    ctx_len = ctx.shape[1]
    assert d == D_MODEL and w_in.shape[0] == DEPTH and seq % GRID_W == 0
    l = 0

    n_cond = batch + 1
    pad = (-n_cond) % 8
    cond = jnp.concatenate([c, c_ctx[None], jnp.zeros((pad, d), F32)], axis=0)
    mod, w_in_b = _adaln(cond, w_ada[l], b_ada[l][None], w_in[l])
    mod = mod.reshape(n_cond + pad, N_ADA, 1, d)
    sh1, sc1, gt1, sh2, sc2, gt2 = [mod[:batch, j] for j in range(N_ADA)]
    csh1, csc1 = mod[batch:batch + 1, 0], mod[batch:batch + 1, 1]

    x2d = x.reshape(batch * seq, d)
    ctx2d = ctx.reshape(batch * ctx_len, d)
    lbs = jnp.stack([hg_lb_fwd.reshape(DEPTH + 1, GROUP_W), hg_lb_bwd.reshape(DEPTH + 1, GROUP_W)])

    tm = min(512, seq)
    tmi = min(256, seq)
    tmc = min(256, ctx_len)
    rope_q = _rope_tables(seq, D_QK ** -0.5 * LOG2E)
    rope_k = _rope_tables(seq, 1.0)
    qkv, hq, logf, w_out_b, w_ff1_b, w_ff2_b = _inproj(
        x2d, sh1, sc1, w_in_b, (0, 1, 2, 3, 6, 7, 4, 5),
        ("rope", "rope", "copy", "silu", "copy", "silu", "logf", "logf"),
        ((BF16, 3), (BF16, 3), (F32, 2)),
        rows_per_mod=seq, tm=tmi, ropes=(rope_q, rope_k), seq=seq, lbs=lbs,
        casts=(w_out[l], w_ff1[l], w_ff2[l]), name="inproj")
    m_ctx = batch * ctx_len
    kvh_ctx, logf_ctx = _inproj(
        ctx2d, csh1, csc1, w_in_b, (1, 2, 6, 4, 5), ("copy", "copy", "copy", "logf", "logf"),
        ((BF16, 3), (F32, 2)), rows_per_mod=m_ctx, tm=tmc, lbs=lbs, name="inproj_ctx")

    o_da = _attention(qkv, kvh_ctx, lam_q1[l][None], lam_k1[l][None], lam_q2[l][None], lam_k2[l][None],
                      da_norm_w[l][None], batch=batch, seq=seq, ctx_len=ctx_len, tq=min(2048, seq))
    o_hg = _hgrn(hq, logf, kvh_ctx, logf_ctx, hg_norm_w[l][None], batch=batch, seq=seq, ctx_len=ctx_len)

    zc, rstd = _outproj(o_da, o_hg, x2d, w_out_b, gt1, seq=seq, tm=tm)
    out = _ffn(zc, rstd, w_ff1_b, w_ff2_b, sh2, sc2, gt2, ln1_g[l][None], ln1_b[l][None],
               ln2_g[l][None], ln2_b[l][None], seq=seq, tm=tm, tf=1024)
    return out.reshape(batch, seq, d)
```
